```python
import math
import jax
import jax.numpy as jnp
from jax import lax
import numpy as np

D_MODEL = 2048
BATCH = 16
SEQ = 2048
DEPTH = 2

GRID_W = 64
CTX_LEN = 256
N_MIXERS = 2
N_A = (DEPTH + 1) // 2
N_B = DEPTH // 2
NORM_EPS = 1e-6
D_FF = 4 * D_MODEL

ML_HEADS = 8
ML_DV = D_MODEL // ML_HEADS
ML_DK = ML_DV // 2
ML_QK = ML_HEADS * ML_DK
ML_V = ML_HEADS * ML_DV
ML_IN = 2 * ML_QK + 2 * ML_V + 4 * ML_HEADS
ML_CHUNK = 64
IGATE_BIAS = -2.0
FGATE_LO = 3.0
FGATE_HI = 6.0

DA_HEADS = 8
DA_DH = D_MODEL // (2 * DA_HEADS)
DA_Q = DA_HEADS * 2 * DA_DH
DA_IN = 3 * DA_Q
Q_BLOCK = 128
ROPE_BASE = 10000.0

kernel_name = 'hybrid_mlstm_diffattn_dit_block'

F32 = jnp.float32


def rmsnorm(x, g):
    xf = x.astype(F32)
    y = xf * lax.rsqrt(jnp.mean(xf * xf, axis=-1, keepdims=True) + NORM_EPS)
    return (y * g.astype(F32)).astype(x.dtype)


def ada_mod(cond, w, b):
    m = jax.nn.silu(cond) @ w + b
    return jnp.split(m[..., None, :], 6, axis=-1)


def modulate(x, g, shift, scale):
    return rmsnorm(x, g) * (1 + scale) + shift


def sq_relu_mlp(h, w1, w2):
    return jnp.square(jax.nn.relu(h @ w1)) @ w2


def axial_rope(t, n_lat):
    rows = n_lat // GRID_W
    row = jnp.repeat(jnp.arange(rows, dtype=F32), GRID_W)
    col = jnp.tile(jnp.arange(GRID_W, dtype=F32), rows)
    nf = DA_DH // 4
    freq = ROPE_BASE ** (-jnp.arange(nf, dtype=F32) / nf)
    ang = jnp.stack([row[:, None] * freq, col[:, None] * freq], axis=1)
    cos = jnp.cos(ang)[:, None, None]
    sin = jnp.sin(ang)[:, None, None]
    tf = t.astype(F32).reshape(*t.shape[:-1], 2, 2, nf)
    t1, t2 = tf[..., 0, :], tf[..., 1, :]
    out = jnp.stack([t1 * cos - t2 * sin, t2 * cos + t1 * sin], axis=-2)
    return out.reshape(t.shape).astype(t.dtype)


def mlstm_zero_state(B):
    return (jnp.zeros((B, ML_HEADS, ML_DK, ML_DV), F32),
            jnp.zeros((B, ML_HEADS, ML_DK), F32),
            jnp.zeros((B, ML_HEADS), F32))


def mlstm_chunk_scan(q, k, v, ig, lf, state):
    B, H, T, _ = q.shape
    nc = T // ML_CHUNK

    def to_chunks(a):
        return jnp.moveaxis(a.reshape(B, H, nc, ML_CHUNK, *a.shape[3:]), 2, 0)

    xs = (to_chunks(q), to_chunks(k), to_chunks(v), to_chunks(ig), to_chunks(lf))
    lower = jnp.tril(jnp.ones((ML_CHUNK, ML_CHUNK), dtype=bool))

    def step(carry, inp):
        C, n, m = carry
        qc, kc, vc, igc, lfc = inp
        b = jnp.cumsum(lfc, axis=-1)
        d = b[..., :, None] - b[..., None, :] + igc[..., None, :]
        d = jnp.where(lower, d, -jnp.inf)
        inter = b + m[..., None]
        m_t = jnp.maximum(inter, jnp.max(d, axis=-1))
        w_inter = jnp.exp(inter - m_t)
        s = jnp.einsum('bhtd,bhsd->bhts', qc, kc) * jnp.exp(d - m_t[..., None])
        num = w_inter[..., None] * jnp.einsum('bhtd,bhde->bhte', qc, C) + jnp.einsum('bhts,bhse->bhte', s, vc)
        den = w_inter * jnp.einsum('bhtd,bhd->bht', qc, n) + jnp.sum(s, axis=-1)
        h = num / jnp.maximum(jnp.abs(den), jnp.exp(-m_t))[..., None]
        b_end = b[..., -1]
        g = b_end[..., None] - b + igc
        m_new = jnp.maximum(b_end + m, jnp.max(g, axis=-1))
        a = jnp.exp(b_end + m - m_new)
        w = jnp.exp(g - m_new[..., None])
        C_new = a[..., None, None] * C + jnp.einsum('bhs,bhsd,bhse->bhde', w, kc, vc)
        n_new = a[..., None] * n + jnp.einsum('bhs,bhsd->bhd', w, kc)
        return (C_new, n_new, m_new), h

    final, h = lax.scan(step, state, xs)
    return jnp.moveaxis(h, 0, 2).reshape(B, H, T, -1), final


def mlstm_project(h, w_in, b_gates):
    B, T, _ = h.shape
    p = (h @ w_in).astype(F32)
    q, k, v, og, g = jnp.split(p, [ML_QK, 2 * ML_QK, 2 * ML_QK + ML_V, 2 * ML_QK + 2 * ML_V], axis=-1)

    def heads(a, dh):
        return a.reshape(B, T, ML_HEADS, dh).transpose(0, 2, 1, 3)

    q = heads(q, ML_DK) * (ML_DK ** -0.5)
    k = heads(k, ML_DK)
    v = heads(v, ML_DV)
    g = (g + b_gates.astype(F32)).reshape(B, T, 4, ML_HEADS).transpose(2, 0, 3, 1)
    return (q, k, v, og, g[0], jax.nn.log_sigmoid(g[1]), g[2], jax.nn.log_sigmoid(g[3]))


def mlstm_bidir(p, state_f, state_b):
    q, k, v, _, ig_f, lf_f, ig_b, lf_b = p
    hf, sf = mlstm_chunk_scan(q, k, v, ig_f, lf_f, state_f)

    def flip(a):
        return jnp.flip(a, axis=2)

    hb, sb = mlstm_chunk_scan(flip(q), flip(k), flip(v), flip(ig_b), flip(lf_b), state_b)
    return hf + flip(hb), sf, sb


def mlstm_out(hs, og, head_g, w_out, dtype):
    B, H, T, _ = hs.shape
    hn = hs * lax.rsqrt(jnp.mean(hs * hs, axis=-1, keepdims=True) + NORM_EPS)
    hn = hn.transpose(0, 2, 1, 3).reshape(B, T, ML_V) * head_g.astype(F32)
    return (hn * jax.nn.sigmoid(og)).astype(dtype) @ w_out


def mlstm_mixer(hl, hc, w_in, b_gates, head_g, w_out, need_ctx):
    pc = mlstm_project(hc, w_in, b_gates)
    pl = mlstm_project(hl, w_in, b_gates)
    z = mlstm_zero_state(hc.shape[0])
    hsc, sf, sb = mlstm_bidir(pc, z, z)
    hsl, _, _ = mlstm_bidir(pl, sf, sb)
    yl = mlstm_out(hsl, pl[3], head_g, w_out, hl.dtype)
    yc = mlstm_out(hsc, pc[3], head_g, w_out, hc.dtype) if need_ctx else None
    return yl, yc


def diff_attention_mixer(hl, hc, w_in, lam_qk, g_sub, w_out, layer_idx, need_ctx):
    lam_init = 0.8 - 0.6 * math.exp(-0.3 * layer_idx)
    lq = lam_qk.astype(F32)
    lam = jnp.exp(jnp.sum(lq[0] * lq[1])) - jnp.exp(jnp.sum(lq[2] * lq[3])) + lam_init
    scale = DA_DH ** -0.5

    def proj(h):
        B, T, _ = h.shape
        q, k, v = jnp.split(h @ w_in, 3, axis=-1)
        return (q.reshape(B, T, DA_HEADS, 2, DA_DH), k.reshape(B, T, DA_HEADS, 2, DA_DH),
                v.reshape(B, T, DA_HEADS, 2 * DA_DH))

    def attend(qb, kk, vv):
        s = jnp.einsum('bqhcd,bkhcd->bhcqk', qb.astype(F32), kk) * scale
        p = jax.nn.softmax(s, axis=-1)
        a = p[:, :, 0] - lam * p[:, :, 1]
        return jnp.einsum('bhqk,bkhe->bqhe', a, vv)

    def out(o, dtype):
        B, T = o.shape[:2]
        on = o * lax.rsqrt(jnp.mean(o * o, axis=-1, keepdims=True) + NORM_EPS) * g_sub.astype(F32) * (1 - lam_init)
        return on.reshape(B, T, DA_Q).astype(dtype) @ w_out

    B, S, _ = hl.shape
    ql, kl, vl = proj(hl)
    ql = axial_rope(ql, S)
    kl = axial_rope(kl, S)
    qc, kc, vc = proj(hc)
    k_all = jnp.concatenate([kl, kc], axis=1).astype(F32)
    v_all = jnp.concatenate([vl, vc], axis=1).astype(F32)
    nb = S // Q_BLOCK
    qblocks = jnp.moveaxis(ql.reshape(B, nb, Q_BLOCK, DA_HEADS, 2, DA_DH), 1, 0)
    ol = lax.map(lambda qb: attend(qb, k_all, v_all), qblocks)
    ol = jnp.moveaxis(ol, 0, 1).reshape(B, S, DA_HEADS, 2 * DA_DH)
    yl = out(ol, hl.dtype)
    yc = out(attend(qc, kc.astype(F32), vc.astype(F32)), hc.dtype) if need_ctx else None
    return yl, yc


def setup_inputs(seed: int = 0) -> dict:
    key = jax.random.key(seed)
    ks = jax.random.split(key, 20)
    D = D_MODEL

    def nrm(k, shape, s):
        return jax.random.normal(k, shape, F32) * s

    gate_base = jnp.concatenate([jnp.full((ML_HEADS,), IGATE_BIAS, dtype=F32),
                                 jnp.linspace(FGATE_LO, FGATE_HI, ML_HEADS, dtype=F32)] * 2)
    return {
        'x': nrm(ks[0], (BATCH, SEQ, D), 1.0),
        'c': nrm(ks[1], (BATCH, D), 1.0),
        'ctx': nrm(ks[2], (BATCH, CTX_LEN, D), 1.0),
        'c_ctx': nrm(ks[3], (D,), 1.0),
        'ada_w': nrm(ks[4], (DEPTH, D, 6 * D), 0.5 * D ** -0.5),
        'ada_b': nrm(ks[5], (DEPTH, 6 * D), 0.02),
        'norm_g': 1.0 + nrm(ks[6], (DEPTH, 4, D), 0.02),
        'mlp_w1': nrm(ks[7], (DEPTH, D, D_FF), D ** -0.5),
        'mlp_w2': nrm(ks[8], (DEPTH, D_FF, D), D_FF ** -0.5),
        'ml_w_in': nrm(ks[9], (N_A, D, ML_IN), D ** -0.5),
        'ml_b_gates': gate_base + nrm(ks[10], (N_A, 4 * ML_HEADS), 0.1),
        'ml_head_g': 1.0 + nrm(ks[11], (N_A, ML_V), 0.02),
        'ml_w_out': nrm(ks[12], (N_A, ML_V, D), ML_V ** -0.5),
        'da_w_in': nrm(ks[13], (N_B, D, DA_IN), D ** -0.5),
        'da_lambda': nrm(ks[14], (N_B, 4, DA_DH), 0.1),
        'da_sub_g': 1.0 + nrm(ks[15], (N_B, 2 * DA_DH), 0.02),
        'da_w_out': nrm(ks[16], (N_B, DA_Q, D), DA_Q ** -0.5),
    }


def reference(x, c, ctx, c_ctx, ada_w, ada_b, norm_g, mlp_w1, mlp_w2, ml_w_in, ml_b_gates, ml_head_g,
              ml_w_out, da_w_in, da_lambda, da_sub_g, da_w_out):
    xl, xc = x, ctx
    for i in range(DEPTH):
        last = i == DEPTH - 1
        j = i // N_MIXERS
        ml = ada_mod(c, ada_w[i], ada_b[i])
        mc = ada_mod(c_ctx, ada_w[i], ada_b[i])
        hl = modulate(xl, norm_g[i, 0], ml[0], ml[1])
        hc = modulate(xc, norm_g[i, 0], mc[0], mc[1])
        if i % N_MIXERS == 0:
            yl, yc = mlstm_mixer(hl, hc, ml_w_in[j], ml_b_gates[j], ml_head_g[j], ml_w_out[j], not last)
        else:
            yl, yc = diff_attention_mixer(hl, hc, da_w_in[j], da_lambda[j], da_sub_g[j], da_w_out[j], i, not last)
        xl = xl + ml[2] * rmsnorm(yl, norm_g[i, 1])
        fl = sq_relu_mlp(modulate(xl, norm_g[i, 2], ml[3], ml[4]), mlp_w1[i], mlp_w2[i])
        xl = xl + ml[5] * rmsnorm(fl, norm_g[i, 3])
        if not last:
            xc = xc + mc[2] * rmsnorm(yc, norm_g[i, 1])
            fc = sq_relu_mlp(modulate(xc, norm_g[i, 2], mc[3], mc[4]), mlp_w1[i], mlp_w2[i])
            xc = xc + mc[5] * rmsnorm(fc, norm_g[i, 3])
    return xl
```

```python
import functools
import math

import jax
import jax.numpy as jnp
from jax import lax
from jax.experimental import pallas as pl
from jax.experimental.pallas import tpu as pltpu

F32 = jnp.float32
BF16 = jnp.bfloat16

NORM_EPS = 1e-6
GRID_W = 64
ROPE_BASE = 10000.0
ML_HEADS = 8
ML_CHUNK = 64
DA_HEADS = 8
N_MOD = 6
COND_ROWS = 32
LANES = 128
GATE_ROWS = 8

VMEM_CAP_BYTES = 56 * 1024 * 1024


def _vmem_limit(nbytes):
    return int(min(VMEM_CAP_BYTES, max(16 * 1024 * 1024, nbytes)))


def _params(n_grid, vmem_bytes):
    return pltpu.CompilerParams(
        dimension_semantics=("arbitrary",) * n_grid,
        vmem_limit_bytes=_vmem_limit(vmem_bytes))


def _tile(n, want):
    if n <= want:
        return n
    t = want
    while t >= 8:
        if n % t == 0 and t % 8 == 0:
            return t
        t -= 8
    return n


def _ada_kernel(c_ref, w_ref, b_ref, o_ref):
    cf = c_ref[...]
    s = (cf * jax.nn.sigmoid(cf)).astype(BF16)
    o_ref[...] = jnp.dot(s, w_ref[...].astype(BF16), preferred_element_type=F32) + b_ref[...]


def _ada(cond, ada_w, ada_b):
    depth, d, n = ada_w.shape
    tn = _tile(n, 1024)
    return pl.pallas_call(
        _ada_kernel,
        grid=(depth, n // tn),
        in_specs=[
            pl.BlockSpec((COND_ROWS, d), lambda l, j: (0, 0)),
            pl.BlockSpec((None, d, tn), lambda l, j: (l, 0, j)),
            pl.BlockSpec((None, 1, tn), lambda l, j: (l, 0, j)),
        ],
        out_specs=pl.BlockSpec((None, COND_ROWS, tn), lambda l, j: (l, 0, j)),
        out_shape=jax.ShapeDtypeStruct((depth, COND_ROWS, n), F32),
        compiler_params=_params(2, 2 * d * tn * 4 + 4 * COND_ROWS * (d + tn) * 4 + d * tn * 2),
        name="ada",
    )(cond, ada_w, ada_b.reshape(depth, 1, n))


def _mod_spec(layer, piece, row_fn, d, n_grid):
    if n_grid == 1:
        return pl.BlockSpec((None, None, 1, d), lambda i: (layer, row_fn(i), 0, piece))
    return pl.BlockSpec((None, None, 1, d), lambda i, j: (layer, row_fn(i), 0, piece))


def _modulated(x_ref, g_ref, sh_ref, sc_ref):
    xf = x_ref[...]
    ms = jnp.mean(xf * xf, axis=-1, keepdims=True)
    y = xf * lax.rsqrt(ms + NORM_EPS) * g_ref[...]
    return (y * (1.0 + sc_ref[...]) + sh_ref[...]).astype(BF16)


def _swap_halves_32(a, first_half):
    fwd = pltpu.roll(a, 3 * LANES // 4, axis=1)
    bwd = pltpu.roll(a, LANES // 4, axis=1)
    return jnp.where(first_half, fwd, bwd)


def _proj_kernel(*refs, with_gates, n_rope_tiles):
    x_ref, sh_ref, sc_ref, g_ref, w_ref = refs[:5]
    k = 5
    if with_gates:
        wg_ref, bg_ref = refs[k:k + 2]
        k += 2
    if n_rope_tiles:
        cos_ref, sin_ref = refs[k:k + 2]
        k += 2
    o_ref = refs[k]
    k += 1
    if with_gates:
        og_ref = refs[k]
        k += 1
    h_scr = refs[k]
    j = pl.program_id(1)

    @pl.when(j == 0)
    def _():
        h_scr[...] = _modulated(x_ref, g_ref, sh_ref, sc_ref)
        if with_gates:
            og_ref[...] = jnp.dot(h_scr[...], wg_ref[...], preferred_element_type=F32) + bg_ref[...]

    acc = jnp.dot(h_scr[...], w_ref[...], preferred_element_type=F32)
    if n_rope_tiles:
        @pl.when(j < n_rope_tiles)
        def _():
            cos = cos_ref[...]
            sin = sin_ref[...]
            lane = lax.broadcasted_iota(jnp.int32, cos.shape, 1)
            first_half = (lane % (LANES // 2)) < (LANES // 4)
            for c in range(acc.shape[1] // LANES):
                a = acc[:, c * LANES:(c + 1) * LANES]
                r = a * cos + _swap_halves_32(a, first_half) * sin
                o_ref[:, c * LANES:(c + 1) * LANES] = r.astype(o_ref.dtype)

        @pl.when(j >= n_rope_tiles)
        def _():
            o_ref[...] = acc.astype(o_ref.dtype)
    else:
        o_ref[...] = acc.astype(o_ref.dtype)


def _proj(x, mods, layer, row_fn, g, w, *, tm, tn, gates=None, rope=None, rope_cols=0):
    m, d = x.shape
    n = w.shape[1]
    tm = _tile(m, tm)
    tn = _tile(n, tn)
    in_specs = [
        pl.BlockSpec((tm, d), lambda i, j: (i, 0)),
        _mod_spec(layer, 0, row_fn, d, 2),
        _mod_spec(layer, 1, row_fn, d, 2),
        pl.BlockSpec((1, d), lambda i, j: (0, 0)),
        pl.BlockSpec((d, tn), lambda i, j: (0, j)),
    ]
    args = [x, mods, mods, g, w]
    out_specs = [pl.BlockSpec((tm, tn), lambda i, j: (i, j))]
    out_shape = [jax.ShapeDtypeStruct((m, n), BF16)]
    if gates is not None:
        wg, bg = gates
        in_specs += [pl.BlockSpec((d, LANES), lambda i, j: (0, 0)),
                     pl.BlockSpec((1, LANES), lambda i, j: (0, 0))]
        args += [wg, bg]
        out_specs.append(pl.BlockSpec((tm, LANES), lambda i, j: (i, 0)))
        out_shape.append(jax.ShapeDtypeStruct((m, LANES), F32))
    n_rope_tiles = 0
    if rope is not None:
        cos, sin = rope
        assert rope_cols % tn == 0 and cos.shape[0] % tm == 0
        n_rope_tiles = rope_cols // tn
        n_pos_tiles = cos.shape[0] // tm
        in_specs += [pl.BlockSpec((tm, LANES), lambda i, j: (i % n_pos_tiles, 0)),
                     pl.BlockSpec((tm, LANES), lambda i, j: (i % n_pos_tiles, 0))]
        args += [cos, sin]
    vmem = 2 * tm * d * 4 + tm * d * 2 + 2 * d * tn * 2 + 2 * tm * tn * 2 + 3 * tm * tn * 4 \
        + 4 * tm * LANES * 4 + 2 * d * LANES * 2 + 3 * tm * d * 4
    out = pl.pallas_call(
        functools.partial(_proj_kernel, with_gates=gates is not None, n_rope_tiles=n_rope_tiles),
        grid=(m // tm, n // tn),
        in_specs=in_specs,
        out_specs=out_specs,
        out_shape=out_shape,
        scratch_shapes=[pltpu.VMEM((tm, d), BF16)],
        compiler_params=_params(2, vmem),
        name="proj",
    )(*args)
    return out if gates is not None else out[0]


def _outproj_kernel(y_ref, w_ref, x_ref, gate_ref, g_ref, o_ref):
    t = jnp.dot(y_ref[...], w_ref[...], preferred_element_type=F32)
    ms = jnp.mean(t * t, axis=-1, keepdims=True)
    o_ref[...] = x_ref[...] + gate_ref[...] * (t * lax.rsqrt(ms + NORM_EPS) * g_ref[...])


def _outproj(y, w, x, mods, layer, row_fn, g, *, tm):
    m, d = x.shape
    kdim = y.shape[1]
    tm = _tile(m, tm)
    vmem = 2 * tm * kdim * 2 + 2 * kdim * d * 2 + 4 * tm * d * 4 + 3 * tm * d * 4
    return pl.pallas_call(
        _outproj_kernel,
        grid=(m // tm,),
        in_specs=[
            pl.BlockSpec((tm, kdim), lambda i: (i, 0)),
            pl.BlockSpec((kdim, d), lambda i: (0, 0)),
            pl.BlockSpec((tm, d), lambda i: (i, 0)),
            _mod_spec(layer, 2, row_fn, d, 1),
            pl.BlockSpec((1, d), lambda i: (0, 0)),
        ],
        out_specs=pl.BlockSpec((tm, d), lambda i: (i, 0)),
        out_shape=jax.ShapeDtypeStruct((m, d), F32),
        compiler_params=_params(1, vmem),
        name="outproj",
    )(y, w, x, mods, g)


def _mlp_kernel(x_ref, sh_ref, sc_ref, gate_ref, g_in_ref, g_out_ref, w1_ref, w2_ref, o_ref,
                h_scr, acc_scr):
    j = pl.program_id(1)

    @pl.when(j == 0)
    def _():
        h_scr[...] = _modulated(x_ref, g_in_ref, sh_ref, sc_ref)
        acc_scr[...] = jnp.zeros_like(acc_scr)

    a = jnp.dot(h_scr[...], w1_ref[...], preferred_element_type=F32)
    a = jnp.maximum(a, 0.0)
    a = (a * a).astype(BF16)
    acc_scr[...] += jnp.dot(a, w2_ref[...], preferred_element_type=F32)

    @pl.when(j == pl.num_programs(1) - 1)
    def _():
        f = acc_scr[...]
        ms = jnp.mean(f * f, axis=-1, keepdims=True)
        o_ref[...] = x_ref[...] + gate_ref[...] * (f * lax.rsqrt(ms + NORM_EPS) * g_out_ref[...])


def _mlp(x, mods, layer, row_fn, g_in, g_out, w1, w2, *, tm, tf):
    m, d = x.shape
    dff = w1.shape[1]
    tm = _tile(m, tm)
    tf = _tile(dff, tf)
    vmem = 4 * tm * d * 4 + tm * d * 2 + tm * d * 4 + 4 * d * tf * 2 + 3 * tm * tf * 4 + 2 * tm * d * 4
    return pl.pallas_call(
        _mlp_kernel,
        grid=(m // tm, dff // tf),
        in_specs=[
            pl.BlockSpec((tm, d), lambda i, j: (i, 0)),
            _mod_spec(layer, 3, row_fn, d, 2),
            _mod_spec(layer, 4, row_fn, d, 2),
            _mod_spec(layer, 5, row_fn, d, 2),
            pl.BlockSpec((1, d), lambda i, j: (0, 0)),
            pl.BlockSpec((1, d), lambda i, j: (0, 0)),
            pl.BlockSpec((d, tf), lambda i, j: (0, j)),
            pl.BlockSpec((tf, d), lambda i, j: (j, 0)),
        ],
        out_specs=pl.BlockSpec((tm, d), lambda i, j: (i, 0)),
        out_shape=jax.ShapeDtypeStruct((m, d), F32),
        scratch_shapes=[pltpu.VMEM((tm, d), BF16), pltpu.VMEM((tm, d), F32)],
        compiler_params=_params(2, vmem),
        name="mlp",
    )(x, mods, mods, mods, g_in, g_out, w1, w2)


def _split3(x):
    x1 = x.astype(BF16)
    r1 = x - x1.astype(F32)
    x2 = r1.astype(BF16)
    x3 = (r1 - x2.astype(F32)).astype(BF16)
    return x1, x2, x3


def _dot_nt(a, b):
    return lax.dot_general(a, b, (((1,), (1,)), ((), ())), preferred_element_type=F32)


def _dot_tn(a, b):
    return lax.dot_general(a, b, (((0,), (0,)), ((), ())), preferred_element_type=F32)


def _log_sigmoid(x):
    return jnp.minimum(x, 0.0) - jnp.log(1.0 + jnp.exp(-jnp.abs(x)))


def _mlstm_kernel(qc_ref, kc_ref, vc_ref, ogc_ref, grc_ref,
                  ql_ref, kl_ref, vl_ref, ogl_ref, grl_ref, hg_ref,
                  oc_ref, ol_ref, hs_c, hs_l):
    L = ML_CHUNK
    dk = qc_ref.shape[1]
    dv = vc_ref.shape[1]
    ncc = qc_ref.shape[0] // L
    ncl = ql_ref.shape[0] // L

    ti = lax.broadcasted_iota(jnp.int32, (L, L), 0)
    si = lax.broadcasted_iota(jnp.int32, (L, L), 1)
    row8 = lax.broadcasted_iota(jnp.int32, (GATE_ROWS, L), 0)
    eye = (ti == si).astype(BF16)
    head_g = hg_ref[...]

    def chunk(c, carry, refs, backward, finish):
        q_ref, k_ref, v_ref, og_ref, gr_ref, hs_ref, o_ref = refs
        C, n, m = carry
        mask = (si >= ti) if backward else (si <= ti)
        u_mat = ((ti >= si) if backward else (ti <= si)).astype(BF16)
        cum_t = mask.astype(BF16)
        rows = pl.ds(pl.multiple_of(c * L, L), L)
        qb = q_ref[rows, :]
        kb = k_ref[rows, :]
        vb = v_ref[rows, :]

        g_raw = gr_ref[c]
        x = jnp.where(row8 % 2 == 1, _log_sigmoid(g_raw), g_raw)
        xs = _split3(x)
        b_rows = sum(jnp.dot(p, u_mat, preferred_element_type=F32) for p in xs)
        b_cols = sum(_dot_nt(cum_t, p) for p in xs)
        x_cols = sum(_dot_nt(eye, p) for p in xs)
        r = 2 if backward else 0
        ig_row = x[r:r + 1, :]
        lf_row = x[r + 1:r + 2, :]
        b_row = b_rows[r + 1:r + 2, :]
        b_col = b_cols[:, r + 1:r + 2]
        ig_col = x_cols[:, r:r + 1]

        c_row = ig_row - b_row
        dmat = jnp.where(mask, b_col + c_row, -jnp.inf)
        m_intra = jnp.max(dmat, axis=-1, keepdims=True)
        inter = b_col + m
        m_t = jnp.maximum(inter, m_intra)
        dexp = jnp.exp(dmat - m_t)
        w_inter = jnp.exp(inter - m_t)

        s = _dot_nt(qb, kb) * dexp
        qn = jnp.sum(qb.astype(F32) * n, axis=-1, keepdims=True)
        den = w_inter * qn + jnp.sum(s, axis=-1, keepdims=True)
        num = w_inter * jnp.dot(qb, C.astype(BF16), preferred_element_type=F32) \
            + jnp.dot(s.astype(BF16), vb, preferred_element_type=F32)
        h = num * (1.0 / jnp.maximum(jnp.abs(den), jnp.exp(-m_t)))

        b_end = jnp.sum(lf_row, axis=-1, keepdims=True)
        g_col = b_end - b_col + ig_col
        m_new = jnp.maximum(b_end + m, jnp.max(g_col, axis=0, keepdims=True))
        a = jnp.exp(b_end + m - m_new)
        w_col = jnp.exp(g_col - m_new)
        C_new = a * C + _dot_tn(kb, (w_col * vb.astype(F32)).astype(BF16))
        n_new = a * n + jnp.sum(w_col * kb.astype(F32), axis=0, keepdims=True)

        if finish:
            hs = hs_ref[rows, :] + h
            hn = hs * lax.rsqrt(jnp.mean(hs * hs, axis=-1, keepdims=True) + NORM_EPS) * head_g
            o_ref[rows, :] = (hn * jax.nn.sigmoid(og_ref[rows, :].astype(F32))).astype(o_ref.dtype)
        else:
            hs_ref[rows, :] = h
        return C_new, n_new, m_new

    refs_c = (qc_ref, kc_ref, vc_ref, ogc_ref, grc_ref, hs_c, oc_ref)
    refs_l = (ql_ref, kl_ref, vl_ref, ogl_ref, grl_ref, hs_l, ol_ref)
    zero = (jnp.zeros((dk, dv), F32), jnp.zeros((1, dk), F32), jnp.zeros((1, 1), F32))

    st = lax.fori_loop(0, ncc, lambda c, s: chunk(c, s, refs_c, False, False), zero)
    lax.fori_loop(0, ncl, lambda c, s: chunk(c, s, refs_l, False, False), st)
    st = lax.fori_loop(0, ncc, lambda c, s: chunk(ncc - 1 - c, s, refs_c, True, True), zero)
    lax.fori_loop(0, ncl, lambda c, s: chunk(ncl - 1 - c, s, refs_l, True, True), st)


def _gate_rows(gates, nb, t):
    g = gates[:, :4 * ML_HEADS].reshape(nb, t // ML_CHUNK, ML_CHUNK, 4, ML_HEADS)
    g = g.transpose(0, 4, 1, 3, 2)
    return jnp.pad(g, ((0, 0), (0, 0), (0, 0), (0, GATE_ROWS - 4), (0, 0)))


def _mlstm(p_ctx, g_ctx, p_lat, g_lat, head_g, nb, d):
    ctx_len = p_ctx.shape[0] // nb
    seq = p_lat.shape[0] // nb
    dv = d // ML_HEADS
    dk = dv // 2
    h = ML_HEADS
    grc = _gate_rows(g_ctx, nb, ctx_len)
    grl = _gate_rows(g_lat, nb, seq)

    def stream_specs(t):
        return [
            pl.BlockSpec((t, dk), lambda b, i: (b, i)),
            pl.BlockSpec((t, dk), lambda b, i: (b, h + i)),
            pl.BlockSpec((t, dv), lambda b, i: (b, h + i)),
            pl.BlockSpec((t, dv), lambda b, i: (b, 2 * h + i)),
            pl.BlockSpec((None, None, t // ML_CHUNK, GATE_ROWS, ML_CHUNK), lambda b, i: (b, i, 0, 0, 0)),
        ]

    vmem = 2 * (ctx_len + seq) * (2 * dk * 2 + 3 * dv * 2 + LANES * 4) + (ctx_len + seq) * dv * 4 \
        + 8 * 1024 * 1024
    return pl.pallas_call(
        _mlstm_kernel,
        grid=(nb, h),
        in_specs=stream_specs(ctx_len) + stream_specs(seq) + [pl.BlockSpec((1, dv), lambda b, i: (0, i))],
        out_specs=[pl.BlockSpec((ctx_len, dv), lambda b, i: (b, i)),
                   pl.BlockSpec((seq, dv), lambda b, i: (b, i))],
        out_shape=[jax.ShapeDtypeStruct((nb * ctx_len, d), BF16),
                   jax.ShapeDtypeStruct((nb * seq, d), BF16)],
        scratch_shapes=[pltpu.VMEM((ctx_len, dv), F32), pltpu.VMEM((seq, dv), F32)],
        compiler_params=_params(2, vmem),
        name="mlstm",
    )(p_ctx, p_ctx, p_ctx, p_ctx, grc, p_lat, p_lat, p_lat, p_lat, grl, head_g)


def _attn_kernel(q_ref, kl_ref, vl_ref, kc_ref, vc_ref, lam_ref, gs_ref, o_ref, *, lam_init):
    dh = q_ref.shape[1] // 2
    lq = lam_ref[...]
    lam = jnp.exp(jnp.sum(lq[0:1] * lq[1:2], axis=-1, keepdims=True)) \
        - jnp.exp(jnp.sum(lq[2:3] * lq[3:4], axis=-1, keepdims=True)) + lam_init

    def probs(c):
        qc = q_ref[:, c * dh:(c + 1) * dh]
        s_l = _dot_nt(qc, kl_ref[:, c * dh:(c + 1) * dh])
        s_c = _dot_nt(qc, kc_ref[:, c * dh:(c + 1) * dh])
        mx = jnp.maximum(jnp.max(s_l, axis=-1, keepdims=True), jnp.max(s_c, axis=-1, keepdims=True))
        p_l = jnp.exp(s_l - mx)
        p_c = jnp.exp(s_c - mx)
        inv = 1.0 / (jnp.sum(p_l, axis=-1, keepdims=True) + jnp.sum(p_c, axis=-1, keepdims=True))
        return p_l, p_c, inv

    p0_l, p0_c, inv0 = probs(0)
    p1_l, p1_c, inv1 = probs(1)
    w1 = lam * inv1
    a_l = (p0_l * inv0 - p1_l * w1).astype(BF16)
    a_c = (p0_c * inv0 - p1_c * w1).astype(BF16)
    o = jnp.dot(a_l, vl_ref[...], preferred_element_type=F32) \
        + jnp.dot(a_c, vc_ref[...], preferred_element_type=F32)
    on = o * lax.rsqrt(jnp.mean(o * o, axis=-1, keepdims=True) + NORM_EPS) * gs_ref[...] * (1.0 - lam_init)
    o_ref[...] = on.astype(o_ref.dtype)


def _attn(p_lat, p_ctx, lam_qk, g_sub, nb, d, lam_init, *, tq):
    seq = p_lat.shape[0] // nb
    ctx_len = p_ctx.shape[0] // nb
    h = DA_HEADS
    hd = d // h
    tq = _tile(seq, tq)
    nq = seq // tq
    vmem = 4 * (seq + ctx_len) * hd * 2 + 4 * tq * hd * 2 + 8 * tq * (seq + ctx_len) * 4 + 4 * 1024 * 1024
    return pl.pallas_call(
        functools.partial(_attn_kernel, lam_init=lam_init),
        grid=(nb, h, nq),
        in_specs=[
            pl.BlockSpec((tq, hd), lambda b, i, t: (b * nq + t, i)),
            pl.BlockSpec((seq, hd), lambda b, i, t: (b, h + i)),
            pl.BlockSpec((seq, hd), lambda b, i, t: (b, 2 * h + i)),
            pl.BlockSpec((ctx_len, hd), lambda b, i, t: (b, i)),
            pl.BlockSpec((ctx_len, hd), lambda b, i, t: (b, h + i)),
            pl.BlockSpec(lam_qk.shape, lambda b, i, t: (0, 0)),
            pl.BlockSpec((1, hd), lambda b, i, t: (0, 0)),
        ],
        out_specs=pl.BlockSpec((tq, hd), lambda b, i, t: (b * nq + t, i)),
        out_shape=jax.ShapeDtypeStruct((nb * seq, d), BF16),
        compiler_params=_params(3, vmem),
        name="attn",
    )(p_lat, p_lat, p_lat, p_ctx, p_ctx, lam_qk, g_sub)


def _rope_tables(seq, dh):
    nf = dh // 4
    n = jnp.arange(seq)
    row = (n // GRID_W).astype(F32)
    col = (n % GRID_W).astype(F32)
    freq = ROPE_BASE ** (-jnp.arange(nf, dtype=F32) / nf)
    ar = row[:, None] * freq
    ac = col[:, None] * freq
    cos = jnp.concatenate([jnp.cos(ar), jnp.cos(ar), jnp.cos(ac), jnp.cos(ac)], axis=1)
    sin = jnp.concatenate([-jnp.sin(ar), jnp.sin(ar), -jnp.sin(ac), jnp.sin(ac)], axis=1)
    return cos, sin


def kernel(x, c, ctx, c_ctx, ada_w, ada_b, norm_g, mlp_w1, mlp_w2, ml_w_in, ml_b_gates, ml_head_g,
           ml_w_out, da_w_in, da_lambda, da_sub_g, da_w_out):
    nb, seq, d = x.shape
    ctx_len = ctx.shape[1]
    depth = ada_w.shape[0]
    assert depth == 2 and nb < COND_ROWS and d // ML_HEADS == 2 * LANES
    assert seq % ML_CHUNK == 0 and ctx_len % ML_CHUNK == 0 and seq % GRID_W == 0

    xl = x.reshape(nb * seq, d)
    xc = ctx.reshape(nb * ctx_len, d)

    cond = jnp.zeros((COND_ROWS, d), F32).at[:nb].set(c).at[nb].set(c_ctx)
    mods = _ada(cond, ada_w, ada_b).reshape(depth, COND_ROWS, 1, N_MOD * d)

    tm = _tile(seq, 512)
    tmc = _tile(nb * ctx_len, 512)
    lat_tiles_per_batch = seq // tm

    def lat_row(i):
        return i // lat_tiles_per_batch

    def ctx_row(i):
        return nb

    g = norm_g.reshape(depth, 4, 1, d)

    ml_qk = ML_HEADS * (d // ML_HEADS // 2)
    n_main = 2 * ml_qk + 2 * d
    w_in = ml_w_in[0]
    q_scale = (d // ML_HEADS // 2) ** -0.5
    w_main = jnp.concatenate([w_in[:, :ml_qk] * q_scale, w_in[:, ml_qk:n_main]], axis=1).astype(BF16)
    n_gate = 4 * ML_HEADS
    w_gate = jnp.pad(w_in[:, n_main:], ((0, 0), (0, LANES - n_gate))).astype(BF16)
    b_gate = jnp.pad(ml_b_gates[0], (0, LANES - n_gate)).reshape(1, LANES)
    w_out0 = ml_w_out[0].astype(BF16)
    w1_0 = mlp_w1[0].astype(BF16)
    w2_0 = mlp_w2[0].astype(BF16)

    p_lat, g_lat = _proj(xl, mods, 0, lat_row, g[0, 0], w_main, tm=tm, tn=1024, gates=(w_gate, b_gate))
    p_ctx, g_ctx = _proj(xc, mods, 0, ctx_row, g[0, 0], w_main, tm=tmc, tn=1024, gates=(w_gate, b_gate))
    y_ctx, y_lat = _mlstm(p_ctx, g_ctx, p_lat, g_lat, ml_head_g[0].reshape(1, d), nb, d)
    xl = _outproj(y_lat, w_out0, xl, mods, 0, lat_row, g[0, 1], tm=tm)
    xc = _outproj(y_ctx, w_out0, xc, mods, 0, ctx_row, g[0, 1], tm=tmc)
    xl = _mlp(xl, mods, 0, lat_row, g[0, 2], g[0, 3], w1_0, w2_0, tm=tm, tf=512)
    xc = _mlp(xc, mods, 0, ctx_row, g[0, 2], g[0, 3], w1_0, w2_0, tm=tmc, tf=512)

    dh = d // (2 * DA_HEADS)
    w_in = da_w_in[0]
    w_qkv = jnp.concatenate([w_in[:, :d] * dh ** -0.5, w_in[:, d:]], axis=1).astype(BF16)
    w_kv = w_in[:, d:].astype(BF16)
    w_out1 = da_w_out[0].astype(BF16)
    w1_1 = mlp_w1[1].astype(BF16)
    w2_1 = mlp_w2[1].astype(BF16)
    layer_idx = 1
    lam_init = 0.8 - 0.6 * math.exp(-0.3 * layer_idx)

    q_lat = _proj(xl, mods, 1, lat_row, g[1, 0], w_qkv, tm=tm, tn=1024,
                  rope=_rope_tables(seq, dh), rope_cols=2 * d)
    kv_ctx = _proj(xc, mods, 1, ctx_row, g[1, 0], w_kv, tm=tmc, tn=1024)
    y_lat = _attn(q_lat, kv_ctx, da_lambda[0], da_sub_g[0].reshape(1, 2 * dh), nb, d, lam_init, tq=256)
    xl = _outproj(y_lat, w_out1, xl, mods, 1, lat_row, g[1, 1], tm=tm)
    xl = _mlp(xl, mods, 1, lat_row, g[1, 2], g[1, 3], w1_1, w2_1, tm=tm, tf=512)
    return xl.reshape(nb, seq, d)
```

```python
import functools
import math

import jax
import jax.numpy as jnp
from jax import lax
from jax.experimental import pallas as pl
from jax.experimental.pallas import tpu as pltpu

F32 = jnp.float32
BF16 = jnp.bfloat16

NORM_EPS = 1e-6
GRID_W = 64
ROPE_BASE = 10000.0
ML_HEADS = 8
ML_BLOCK = 256
ML_HB = 2
DA_HEADS = 8
ATTN_TQ = 256
ATTN_TK = 512
N_MOD = 6
COND_ROWS = 32
LANES = 128
GATE_ROWS = 8

VMEM_CAP_BYTES = 56 * 1024 * 1024


def _vmem_limit(nbytes):
    return int(min(VMEM_CAP_BYTES, max(16 * 1024 * 1024, nbytes)))


def _params(n_grid, vmem_bytes):
    return pltpu.CompilerParams(
        dimension_semantics=("arbitrary",) * n_grid,
        vmem_limit_bytes=_vmem_limit(vmem_bytes))


def _tile(n, want):
    if n <= want:
        return n
    t = want
    while t >= 8:
        if n % t == 0 and t % 8 == 0:
            return t
        t -= 8
    return n


def _ada_kernel(c_ref, w_ref, b_ref, o_ref):
    cf = c_ref[...]
    s = (cf * jax.nn.sigmoid(cf)).astype(BF16)
    o_ref[...] = jnp.dot(s, w_ref[...].astype(BF16), preferred_element_type=F32) + b_ref[...]


def _ada(cond, ada_w, ada_b):
    depth, d, n = ada_w.shape
    tn = _tile(n, 1024)
    return pl.pallas_call(
        _ada_kernel,
        grid=(depth, n // tn),
        in_specs=[
            pl.BlockSpec((COND_ROWS, d), lambda l, j: (0, 0)),
            pl.BlockSpec((None, d, tn), lambda l, j: (l, 0, j)),
            pl.BlockSpec((None, 1, tn), lambda l, j: (l, 0, j)),
        ],
        out_specs=pl.BlockSpec((None, COND_ROWS, tn), lambda l, j: (l, 0, j)),
        out_shape=jax.ShapeDtypeStruct((depth, COND_ROWS, n), F32),
        compiler_params=_params(2, 2 * d * tn * 4 + 4 * COND_ROWS * (d + tn) * 4 + d * tn * 2),
        name="ada",
    )(cond, ada_w, ada_b.reshape(depth, 1, n))


def _mod_spec(layer, piece, row_fn, d, n_grid):
    if n_grid == 1:
        return pl.BlockSpec((None, None, 1, d), lambda i: (layer, row_fn(i), 0, piece))
    return pl.BlockSpec((None, None, 1, d), lambda i, j: (layer, row_fn(i), 0, piece))


def _modulated(x_ref, g_ref, sh_ref, sc_ref):
    xf = x_ref[...]
    ms = jnp.mean(xf * xf, axis=-1, keepdims=True)
    y = xf * lax.rsqrt(ms + NORM_EPS) * g_ref[...]
    return (y * (1.0 + sc_ref[...]) + sh_ref[...]).astype(BF16)


def _swap_halves_32(a, first_half):
    fwd = pltpu.roll(a, 3 * LANES // 4, axis=1)
    bwd = pltpu.roll(a, LANES // 4, axis=1)
    return jnp.where(first_half, fwd, bwd)


def _proj_kernel(*refs, with_gates, n_rope_tiles):
    x_ref, sh_ref, sc_ref, g_ref, w_ref = refs[:5]
    k = 5
    if with_gates:
        wg_ref, bg_ref = refs[k:k + 2]
        k += 2
    if n_rope_tiles:
        cos_ref, sin_ref = refs[k:k + 2]
        k += 2
    o_ref = refs[k]
    k += 1
    if with_gates:
        og_ref = refs[k]
        k += 1
    h_scr = refs[k]
    j = pl.program_id(1)

    @pl.when(j == 0)
    def _():
        h_scr[...] = _modulated(x_ref, g_ref, sh_ref, sc_ref)
        if with_gates:
            og_ref[...] = jnp.dot(h_scr[...], wg_ref[...], preferred_element_type=F32) + bg_ref[...]

    acc = jnp.dot(h_scr[...], w_ref[...], preferred_element_type=F32)
    if n_rope_tiles:
        @pl.when(j < n_rope_tiles)
        def _():
            cos = cos_ref[...]
            sin = sin_ref[...]
            lane = lax.broadcasted_iota(jnp.int32, cos.shape, 1)
            first_half = (lane % (LANES // 2)) < (LANES // 4)
            for c in range(acc.shape[1] // LANES):
                a = acc[:, c * LANES:(c + 1) * LANES]
                r = a * cos + _swap_halves_32(a, first_half) * sin
                o_ref[:, c * LANES:(c + 1) * LANES] = r.astype(o_ref.dtype)

        @pl.when(j >= n_rope_tiles)
        def _():
            o_ref[...] = acc.astype(o_ref.dtype)
    else:
        o_ref[...] = acc.astype(o_ref.dtype)


def _proj(x, mods, layer, row_fn, g, w, *, tm, tn, gates=None, rope=None, rope_cols=0):
    m, d = x.shape
    n = w.shape[1]
    tm = _tile(m, tm)
    tn = _tile(n, tn)
    in_specs = [
        pl.BlockSpec((tm, d), lambda i, j: (i, 0)),
        _mod_spec(layer, 0, row_fn, d, 2),
        _mod_spec(layer, 1, row_fn, d, 2),
        pl.BlockSpec((1, d), lambda i, j: (0, 0)),
        pl.BlockSpec((d, tn), lambda i, j: (0, j)),
    ]
    args = [x, mods, mods, g, w]
    out_specs = [pl.BlockSpec((tm, tn), lambda i, j: (i, j))]
    out_shape = [jax.ShapeDtypeStruct((m, n), BF16)]
    if gates is not None:
        wg, bg = gates
        in_specs += [pl.BlockSpec((d, LANES), lambda i, j: (0, 0)),
                     pl.BlockSpec((1, LANES), lambda i, j: (0, 0))]
        args += [wg, bg]
        out_specs.append(pl.BlockSpec((tm, LANES), lambda i, j: (i, 0)))
        out_shape.append(jax.ShapeDtypeStruct((m, LANES), F32))
    n_rope_tiles = 0
    if rope is not None:
        cos, sin = rope
        assert rope_cols % tn == 0 and cos.shape[0] % tm == 0
        n_rope_tiles = rope_cols // tn
        n_pos_tiles = cos.shape[0] // tm
        in_specs += [pl.BlockSpec((tm, LANES), lambda i, j: (i % n_pos_tiles, 0)),
                     pl.BlockSpec((tm, LANES), lambda i, j: (i % n_pos_tiles, 0))]
        args += [cos, sin]
    vmem = 2 * tm * d * 4 + tm * d * 2 + 2 * d * tn * 2 + 2 * tm * tn * 2 + 3 * tm * tn * 4 \
        + 4 * tm * LANES * 4 + 2 * d * LANES * 2 + 3 * tm * d * 4
    out = pl.pallas_call(
        functools.partial(_proj_kernel, with_gates=gates is not None, n_rope_tiles=n_rope_tiles),
        grid=(m // tm, n // tn),
        in_specs=in_specs,
        out_specs=out_specs,
        out_shape=out_shape,
        scratch_shapes=[pltpu.VMEM((tm, d), BF16)],
        compiler_params=_params(2, vmem),
        name="proj",
    )(*args)
    return out if gates is not None else out[0]


def _outproj_kernel(y_ref, w_ref, x_ref, gate_ref, g_ref, o_ref):
    t = jnp.dot(y_ref[...], w_ref[...], preferred_element_type=F32)
    ms = jnp.mean(t * t, axis=-1, keepdims=True)
    o_ref[...] = x_ref[...] + gate_ref[...] * (t * lax.rsqrt(ms + NORM_EPS) * g_ref[...])


def _outproj(y, w, x, mods, layer, row_fn, g, *, tm):
    m, d = x.shape
    kdim = y.shape[1]
    tm = _tile(m, tm)
    vmem = 2 * tm * kdim * 2 + 2 * kdim * d * 2 + 4 * tm * d * 4 + 3 * tm * d * 4
    return pl.pallas_call(
        _outproj_kernel,
        grid=(m // tm,),
        in_specs=[
            pl.BlockSpec((tm, kdim), lambda i: (i, 0)),
            pl.BlockSpec((kdim, d), lambda i: (0, 0)),
            pl.BlockSpec((tm, d), lambda i: (i, 0)),
            _mod_spec(layer, 2, row_fn, d, 1),
            pl.BlockSpec((1, d), lambda i: (0, 0)),
        ],
        out_specs=pl.BlockSpec((tm, d), lambda i: (i, 0)),
        out_shape=jax.ShapeDtypeStruct((m, d), F32),
        compiler_params=_params(1, vmem),
        name="outproj",
    )(y, w, x, mods, g)


def _mlp_kernel(x_ref, sh_ref, sc_ref, gate_ref, g_in_ref, g_out_ref, w1_ref, w2_ref, o_ref,
                h_scr, acc_scr):
    j = pl.program_id(1)

    @pl.when(j == 0)
    def _():
        h_scr[...] = _modulated(x_ref, g_in_ref, sh_ref, sc_ref)
        acc_scr[...] = jnp.zeros_like(acc_scr)

    a = jnp.dot(h_scr[...], w1_ref[...], preferred_element_type=F32)
    a = jnp.maximum(a, 0.0)
    a = (a * a).astype(BF16)
    acc_scr[...] += jnp.dot(a, w2_ref[...], preferred_element_type=F32)

    @pl.when(j == pl.num_programs(1) - 1)
    def _():
        f = acc_scr[...]
        ms = jnp.mean(f * f, axis=-1, keepdims=True)
        o_ref[...] = x_ref[...] + gate_ref[...] * (f * lax.rsqrt(ms + NORM_EPS) * g_out_ref[...])


def _mlp(x, mods, layer, row_fn, g_in, g_out, w1, w2, *, tm, tf):
    m, d = x.shape
    dff = w1.shape[1]
    tm = _tile(m, tm)
    tf = _tile(dff, tf)
    vmem = 4 * tm * d * 4 + tm * d * 2 + tm * d * 4 + 4 * d * tf * 2 + 3 * tm * tf * 4 + 2 * tm * d * 4
    return pl.pallas_call(
        _mlp_kernel,
        grid=(m // tm, dff // tf),
        in_specs=[
            pl.BlockSpec((tm, d), lambda i, j: (i, 0)),
            _mod_spec(layer, 3, row_fn, d, 2),
            _mod_spec(layer, 4, row_fn, d, 2),
            _mod_spec(layer, 5, row_fn, d, 2),
            pl.BlockSpec((1, d), lambda i, j: (0, 0)),
            pl.BlockSpec((1, d), lambda i, j: (0, 0)),
            pl.BlockSpec((d, tf), lambda i, j: (0, j)),
            pl.BlockSpec((tf, d), lambda i, j: (j, 0)),
        ],
        out_specs=pl.BlockSpec((tm, d), lambda i, j: (i, 0)),
        out_shape=jax.ShapeDtypeStruct((m, d), F32),
        scratch_shapes=[pltpu.VMEM((tm, d), BF16), pltpu.VMEM((tm, d), F32)],
        compiler_params=_params(2, vmem),
        name="mlp",
    )(x, mods, mods, mods, g_in, g_out, w1, w2)


def _split3(x):
    x1 = x.astype(BF16)
    r1 = x - x1.astype(F32)
    x2 = r1.astype(BF16)
    x3 = (r1 - x2.astype(F32)).astype(BF16)
    return x1, x2, x3


def _dot_nt(a, b):
    return lax.dot_general(a, b, (((1,), (1,)), ((), ())), preferred_element_type=F32)


def _dot_tn(a, b):
    return lax.dot_general(a, b, (((0,), (0,)), ((), ())), preferred_element_type=F32)


def _log_sigmoid(x):
    return jnp.minimum(x, 0.0) - jnp.log(1.0 + jnp.exp(-jnp.abs(x)))


def _gate_scan_kernel(x_ref, o_ref):
    x = x_ref[...]
    L = x.shape[1]
    kind = lax.broadcasted_iota(jnp.int32, x.shape, 0) % 4
    ui = lax.broadcasted_iota(jnp.int32, (L, L), 0)
    si = lax.broadcasted_iota(jnp.int32, (L, L), 1)
    xs = _split3(_log_sigmoid(x))
    fwd = sum(jnp.dot(p, (ui <= si).astype(BF16), preferred_element_type=F32) for p in xs)
    bwd = sum(jnp.dot(p, (ui >= si).astype(BF16), preferred_element_type=F32) for p in xs)
    o_ref[...] = jnp.where(kind == 1, fwd, jnp.where(kind == 3, bwd, x))


def _gate_scan(gates, nb, t, L):
    nc = t // L
    g = gates[:, :4 * ML_HEADS].reshape(nb, nc, L, 4, ML_HEADS)
    g = g.transpose(0, 4, 1, 3, 2).reshape(nb * ML_HEADS * nc * 4, L)
    tr = _tile(g.shape[0], 1024)
    assert tr % 4 == 0
    s = pl.pallas_call(
        _gate_scan_kernel,
        grid=(g.shape[0] // tr,),
        in_specs=[pl.BlockSpec((tr, L), lambda i: (i, 0))],
        out_specs=pl.BlockSpec((tr, L), lambda i: (i, 0)),
        out_shape=jax.ShapeDtypeStruct(g.shape, F32),
        compiler_params=_params(1, 24 * tr * max(L, LANES) * 4),
        name="gate_scan",
    )(g)
    s = s.reshape(nb, ML_HEADS, nc, 4, L)
    rows = jnp.pad(s, ((0, 0), (0, 0), (0, 0), (0, GATE_ROWS - 4), (0, 0)))
    cols = jnp.pad(s.transpose(0, 1, 2, 4, 3), ((0, 0), (0, 0), (0, 0), (0, 0), (0, GATE_ROWS - 4)))
    return rows, cols


def _mlstm_kernel(qc_ref, kc_ref, vc_ref, ogc_ref, grc_ref, gcc_ref,
                  ql_ref, kl_ref, vl_ref, ogl_ref, grl_ref, gcl_ref, hg_ref,
                  oc_ref, ol_ref, hf_c, hb_c, hf_l, hb_l):
    L = grc_ref.shape[-1]
    dk = qc_ref.shape[1] // ML_HB
    dv = vc_ref.shape[1] // ML_HB
    ncc = qc_ref.shape[0] // L
    ncl = ql_ref.shape[0] // L

    ti = lax.broadcasted_iota(jnp.int32, (L, L), 0)
    si = lax.broadcasted_iota(jnp.int32, (L, L), 1)

    def chunk(c, carry, refs, hh, backward):
        q_ref, k_ref, v_ref, gr_ref, gc_ref, hs_ref = refs
        C, n, m = carry
        mask = (si >= ti) if backward else (si <= ti)
        rows = pl.ds(pl.multiple_of(c * L, L), L)
        qb = q_ref[rows, hh * dk:(hh + 1) * dk]
        kb = k_ref[rows, hh * dk:(hh + 1) * dk]
        vb = v_ref[rows, hh * dv:(hh + 1) * dv]

        g_rows = gr_ref[hh, c]
        g_cols = gc_ref[hh, c]
        r = 2 if backward else 0
        ig_row = g_rows[r:r + 1, :]
        b_row = g_rows[r + 1:r + 2, :]
        ig_col = g_cols[:, r:r + 1]
        b_col = g_cols[:, r + 1:r + 2]
        b_end = b_row[:, 0:1] if backward else b_row[:, L - 1:L]

        c_row = ig_row - b_row
        dmat = jnp.where(mask, b_col + c_row, -jnp.inf)
        m_intra = jnp.max(dmat, axis=-1, keepdims=True)
        inter = b_col + m
        m_t = jnp.maximum(inter, m_intra)
        dexp = jnp.exp(dmat - m_t)
        w_inter = jnp.exp(inter - m_t)

        s = _dot_nt(qb, kb) * dexp
        qn = jnp.sum(qb.astype(F32) * n, axis=-1, keepdims=True)
        s_fold = s if L <= LANES else sum(s[:, j:j + LANES] for j in range(0, L, LANES))
        den = w_inter * qn + jnp.sum(s_fold, axis=-1, keepdims=True)
        num = w_inter * jnp.dot(qb, C.astype(BF16), preferred_element_type=F32) \
            + jnp.dot(s.astype(BF16), vb, preferred_element_type=F32)
        hs_ref[rows, hh * dv:(hh + 1) * dv] = num * (1.0 / jnp.maximum(jnp.abs(den), jnp.exp(-m_t)))

        g_col = b_end - b_col + ig_col
        m_new = jnp.maximum(b_end + m, jnp.max(g_col, axis=0, keepdims=True))
        a = jnp.exp(b_end + m - m_new)
        w_col = jnp.exp(g_col - m_new)
        C_new = a * C + _dot_tn(kb, (w_col * vb.astype(F32)).astype(BF16))
        n_new = a * n + jnp.sum(w_col * kb.astype(F32), axis=0, keepdims=True)
        return C_new, n_new, m_new

    def scan_segment(nc, q_ref, k_ref, v_ref, gr_ref, gc_ref, hf_ref, hb_ref, states):
        fwd_refs = (q_ref, k_ref, v_ref, gr_ref, gc_ref, hf_ref)
        bwd_refs = (q_ref, k_ref, v_ref, gr_ref, gc_ref, hb_ref)

        def body(c, st):
            out = []
            for hh in range(ML_HB):
                out.append(chunk(c, st[2 * hh], fwd_refs, hh, False))
                out.append(chunk(nc - 1 - c, st[2 * hh + 1], bwd_refs, hh, True))
            return tuple(out)

        return lax.fori_loop(0, nc, body, states)

    zero = (jnp.zeros((dk, dv), F32), jnp.zeros((1, dk), F32), jnp.zeros((1, 1), F32))
    st = scan_segment(ncc, qc_ref, kc_ref, vc_ref, grc_ref, gcc_ref, hf_c, hb_c, (zero,) * (2 * ML_HB))
    scan_segment(ncl, ql_ref, kl_ref, vl_ref, grl_ref, gcl_ref, hf_l, hb_l, st)

    def finish_segment(nc, hf_ref, hb_ref, og_ref, o_ref):
        def body(c, _):
            rows = pl.ds(pl.multiple_of(c * L, L), L)
            for hh in range(ML_HB):
                cols = slice(hh * dv, (hh + 1) * dv)
                hs = hf_ref[rows, cols] + hb_ref[rows, cols]
                hn = hs * lax.rsqrt(jnp.mean(hs * hs, axis=-1, keepdims=True) + NORM_EPS) * hg_ref[:, cols]
                o_ref[rows, cols] = (hn * jax.nn.sigmoid(og_ref[rows, cols].astype(F32))).astype(o_ref.dtype)
            return 0

        lax.fori_loop(0, nc, body, 0)

    finish_segment(ncc, hf_c, hb_c, ogc_ref, oc_ref)
    finish_segment(ncl, hf_l, hb_l, ogl_ref, ol_ref)


def _mlstm(p_ctx, g_ctx, p_lat, g_lat, head_g, nb, d):
    ctx_len = p_ctx.shape[0] // nb
    seq = p_lat.shape[0] // nb
    dv = ML_HB * (d // ML_HEADS)
    dk = dv // 2
    nh = ML_HEADS // ML_HB
    L = math.gcd(math.gcd(ctx_len, seq), ML_BLOCK)
    grc, gcc = _gate_scan(g_ctx, nb, ctx_len, L)
    grl, gcl = _gate_scan(g_lat, nb, seq, L)

    def stream_specs(t):
        return [
            pl.BlockSpec((t, dk), lambda b, i: (b, i)),
            pl.BlockSpec((t, dk), lambda b, i: (b, nh + i)),
            pl.BlockSpec((t, dv), lambda b, i: (b, nh + i)),
            pl.BlockSpec((t, dv), lambda b, i: (b, 2 * nh + i)),
            pl.BlockSpec((None, ML_HB, t // L, GATE_ROWS, L), lambda b, i: (b, i, 0, 0, 0)),
            pl.BlockSpec((None, ML_HB, t // L, L, GATE_ROWS), lambda b, i: (b, i, 0, 0, 0)),
        ]

    t_all = ctx_len + seq
    gate_tiles = 2 * ML_HB * (t_all // L) * (GATE_ROWS + L) * max(L, LANES) * 4
    vmem = 2 * t_all * (2 * dk * 2 + 3 * dv * 2) + 2 * t_all * dv * 4 + gate_tiles + 8 * 1024 * 1024
    return pl.pallas_call(
        _mlstm_kernel,
        grid=(nb, nh),
        in_specs=stream_specs(ctx_len) + stream_specs(seq) + [pl.BlockSpec((1, dv), lambda b, i: (0, i))],
        out_specs=[pl.BlockSpec((ctx_len, dv), lambda b, i: (b, i)),
                   pl.BlockSpec((seq, dv), lambda b, i: (b, i))],
        out_shape=[jax.ShapeDtypeStruct((nb * ctx_len, d), BF16),
                   jax.ShapeDtypeStruct((nb * seq, d), BF16)],
        scratch_shapes=[pltpu.VMEM((ctx_len, dv), F32), pltpu.VMEM((ctx_len, dv), F32),
                        pltpu.VMEM((seq, dv), F32), pltpu.VMEM((seq, dv), F32)],
        compiler_params=_params(2, vmem),
        name="mlstm",
    )(p_ctx, p_ctx, p_ctx, p_ctx, grc, gcc, p_lat, p_lat, p_lat, p_lat, grl, gcl, head_g)


def _attn_kernel(q_ref, kl_ref, vl_ref, kc_ref, vc_ref, lam_ref, gs_ref, o_ref, *, lam_init):
    dh = q_ref.shape[1] // 2
    lq = lam_ref[...]
    lam = jnp.exp(jnp.sum(lq[0:1] * lq[1:2], axis=-1, keepdims=True)) \
        - jnp.exp(jnp.sum(lq[2:3] * lq[3:4], axis=-1, keepdims=True)) + lam_init

    seq = q_ref.shape[0]
    ctx_len = kc_ref.shape[0]
    tq = _tile(seq, ATTN_TQ)
    tk = _tile(seq, ATTN_TK)
    blocks = [(kl_ref, vl_ref, j * tk, tk) for j in range(seq // tk)] + [(kc_ref, vc_ref, 0, ctx_len)]

    def q_tile(t, _):
        rows = pl.ds(pl.multiple_of(t * tq, tq), tq)
        q0 = q_ref[rows, 0:dh]
        q1 = q_ref[rows, dh:2 * dh]
        m = l = acc = None
        for k_ref, v_ref, start, size in blocks:
            s = jnp.concatenate([_dot_nt(q0, k_ref[start:start + size, 0:dh]),
                                 _dot_nt(q1, k_ref[start:start + size, dh:2 * dh])], axis=0)
            mx = jnp.max(s, axis=-1, keepdims=True)
            m_new = mx if m is None else jnp.maximum(m, mx)
            p = jnp.exp2(s - m_new)
            ps = jnp.sum(p, axis=-1, keepdims=True)
            pv = jnp.dot(p.astype(BF16), v_ref[start:start + size, :], preferred_element_type=F32)
            if m is None:
                l, acc = ps, pv
            else:
                alpha = jnp.exp2(m - m_new)
                l = alpha * l + ps
                acc = alpha * acc + pv
            m = m_new
        a = acc * (1.0 / l)
        o = a[0:tq] - lam * a[tq:2 * tq]
        on = o * lax.rsqrt(jnp.mean(o * o, axis=-1, keepdims=True) + NORM_EPS) * gs_ref[...] * (1.0 - lam_init)
        o_ref[rows, :] = on.astype(o_ref.dtype)
        return 0

    lax.fori_loop(0, seq // tq, q_tile, 0)


def _attn(p_lat, p_ctx, lam_qk, g_sub, nb, d, lam_init):
    seq = p_lat.shape[0] // nb
    ctx_len = p_ctx.shape[0] // nb
    h = DA_HEADS
    hd = d // h
    tq = _tile(seq, ATTN_TQ)
    tk = _tile(seq, ATTN_TK)
    vmem = 2 * (4 * seq + 2 * ctx_len) * hd * 2 + 12 * tq * tk * 4 + 8 * tq * hd * 4 + 4 * 1024 * 1024
    return pl.pallas_call(
        functools.partial(_attn_kernel, lam_init=lam_init),
        grid=(nb, h),
        in_specs=[
            pl.BlockSpec((seq, hd), lambda b, i: (b, i)),
            pl.BlockSpec((seq, hd), lambda b, i: (b, h + i)),
            pl.BlockSpec((seq, hd), lambda b, i: (b, 2 * h + i)),
            pl.BlockSpec((ctx_len, hd), lambda b, i: (b, i)),
            pl.BlockSpec((ctx_len, hd), lambda b, i: (b, h + i)),
            pl.BlockSpec(lam_qk.shape, lambda b, i: (0, 0)),
            pl.BlockSpec((1, hd), lambda b, i: (0, 0)),
        ],
        out_specs=pl.BlockSpec((seq, hd), lambda b, i: (b, i)),
        out_shape=jax.ShapeDtypeStruct((nb * seq, d), BF16),
        compiler_params=_params(2, vmem),
        name="attn",
    )(p_lat, p_lat, p_lat, p_ctx, p_ctx, lam_qk, g_sub)


def _rope_tables(seq, dh):
    nf = dh // 4
    n = jnp.arange(seq)
    row = (n // GRID_W).astype(F32)
    col = (n % GRID_W).astype(F32)
    freq = ROPE_BASE ** (-jnp.arange(nf, dtype=F32) / nf)
    ar = row[:, None] * freq
    ac = col[:, None] * freq
    cos = jnp.concatenate([jnp.cos(ar), jnp.cos(ar), jnp.cos(ac), jnp.cos(ac)], axis=1)
    sin = jnp.concatenate([-jnp.sin(ar), jnp.sin(ar), -jnp.sin(ac), jnp.sin(ac)], axis=1)
    return cos, sin


def kernel(x, c, ctx, c_ctx, ada_w, ada_b, norm_g, mlp_w1, mlp_w2, ml_w_in, ml_b_gates, ml_head_g,
           ml_w_out, da_w_in, da_lambda, da_sub_g, da_w_out):
    nb, seq, d = x.shape
    ctx_len = ctx.shape[1]
    depth = ada_w.shape[0]
    assert depth == 2 and nb < COND_ROWS and d // ML_HEADS == 2 * LANES
    assert seq % 64 == 0 and ctx_len % 64 == 0 and seq % GRID_W == 0

    xl = x.reshape(nb * seq, d)
    xc = ctx.reshape(nb * ctx_len, d)

    cond = jnp.zeros((COND_ROWS, d), F32).at[:nb].set(c).at[nb].set(c_ctx)
    mods = _ada(cond, ada_w, ada_b).reshape(depth, COND_ROWS, 1, N_MOD * d)

    tm = _tile(seq, 512)
    tmc = _tile(nb * ctx_len, 512)
    lat_tiles_per_batch = seq // tm

    def lat_row(i):
        return i // lat_tiles_per_batch

    def ctx_row(i):
        return nb

    g = norm_g.reshape(depth, 4, 1, d)

    ml_qk = ML_HEADS * (d // ML_HEADS // 2)
    n_main = 2 * ml_qk + 2 * d
    w_in = ml_w_in[0]
    q_scale = (d // ML_HEADS // 2) ** -0.5
    w_main = jnp.concatenate([w_in[:, :ml_qk] * q_scale, w_in[:, ml_qk:n_main]], axis=1).astype(BF16)
    n_gate = 4 * ML_HEADS
    w_gate = jnp.pad(w_in[:, n_main:], ((0, 0), (0, LANES - n_gate))).astype(BF16)
    b_gate = jnp.pad(ml_b_gates[0], (0, LANES - n_gate)).reshape(1, LANES)
    w_out0 = ml_w_out[0].astype(BF16)
    w1_0 = mlp_w1[0].astype(BF16)
    w2_0 = mlp_w2[0].astype(BF16)

    p_lat, g_lat = _proj(xl, mods, 0, lat_row, g[0, 0], w_main, tm=tm, tn=1024, gates=(w_gate, b_gate))
    p_ctx, g_ctx = _proj(xc, mods, 0, ctx_row, g[0, 0], w_main, tm=tmc, tn=1024, gates=(w_gate, b_gate))
    y_ctx, y_lat = _mlstm(p_ctx, g_ctx, p_lat, g_lat, ml_head_g[0].reshape(1, d), nb, d)
    xl = _outproj(y_lat, w_out0, xl, mods, 0, lat_row, g[0, 1], tm=tm)
    xc = _outproj(y_ctx, w_out0, xc, mods, 0, ctx_row, g[0, 1], tm=tmc)
    xl = _mlp(xl, mods, 0, lat_row, g[0, 2], g[0, 3], w1_0, w2_0, tm=tm, tf=512)
    xc = _mlp(xc, mods, 0, ctx_row, g[0, 2], g[0, 3], w1_0, w2_0, tm=tmc, tf=512)

    dh = d // (2 * DA_HEADS)
    w_in = da_w_in[0]
    w_qkv = jnp.concatenate([w_in[:, :d] * (dh ** -0.5 * math.log2(math.e)), w_in[:, d:]], axis=1).astype(BF16)
    w_kv = w_in[:, d:].astype(BF16)
    w_out1 = da_w_out[0].astype(BF16)
    w1_1 = mlp_w1[1].astype(BF16)
    w2_1 = mlp_w2[1].astype(BF16)
    layer_idx = 1
    lam_init = 0.8 - 0.6 * math.exp(-0.3 * layer_idx)

    q_lat = _proj(xl, mods, 1, lat_row, g[1, 0], w_qkv, tm=tm, tn=1024,
                  rope=_rope_tables(seq, dh), rope_cols=2 * d)
    kv_ctx = _proj(xc, mods, 1, ctx_row, g[1, 0], w_kv, tm=tmc, tn=1024)
    y_lat = _attn(q_lat, kv_ctx, da_lambda[0], da_sub_g[0].reshape(1, 2 * dh), nb, d, lam_init)
    xl = _outproj(y_lat, w_out1, xl, mods, 1, lat_row, g[1, 1], tm=tm)
    xl = _mlp(xl, mods, 1, lat_row, g[1, 2], g[1, 3], w1_1, w2_1, tm=tm, tf=512)
    return xl.reshape(nb, seq, d)
```

```python
import functools
import math

import jax
import jax.numpy as jnp
from jax import lax
from jax.experimental import pallas as pl
from jax.experimental.pallas import tpu as pltpu

F32 = jnp.float32
BF16 = jnp.bfloat16

NORM_EPS = 1e-6
GRID_W = 64
ROPE_BASE = 10000.0
ML_HEADS = 8
ML_BLOCK = 256
ML_HB = 2
DA_HEADS = 8
ATTN_TQ = 256
ATTN_TK = 512
N_MOD = 6
COND_ROWS = 32
LANES = 128
MXU_COLS = 256
GATE_ROWS = 8

VMEM_CAP_BYTES = 56 * 1024 * 1024


def _vmem_limit(nbytes):
    return int(min(VMEM_CAP_BYTES, max(16 * 1024 * 1024, nbytes)))


def _params(n_grid, vmem_bytes):
    return pltpu.CompilerParams(
        dimension_semantics=("arbitrary",) * n_grid,
        vmem_limit_bytes=_vmem_limit(vmem_bytes))


def _tile(n, want):
    if n <= want:
        return n
    t = want
    while t >= 8:
        if n % t == 0 and t % 8 == 0:
            return t
        t -= 8
    return n


def _ada_kernel(c_ref, w_ref, b_ref, o_ref):
    cf = c_ref[...]
    s = (cf * jax.nn.sigmoid(cf)).astype(BF16)
    o_ref[...] = jnp.dot(s, w_ref[...].astype(BF16), preferred_element_type=F32) + b_ref[...]


def _ada(cond, ada_w, ada_b):
    depth, d, n = ada_w.shape
    tn = _tile(n, 1024)
    return pl.pallas_call(
        _ada_kernel,
        grid=(depth, n // tn),
        in_specs=[
            pl.BlockSpec((COND_ROWS, d), lambda l, j: (0, 0)),
            pl.BlockSpec((None, d, tn), lambda l, j: (l, 0, j)),
            pl.BlockSpec((None, 1, tn), lambda l, j: (l, 0, j)),
        ],
        out_specs=pl.BlockSpec((None, COND_ROWS, tn), lambda l, j: (l, 0, j)),
        out_shape=jax.ShapeDtypeStruct((depth, COND_ROWS, n), F32),
        compiler_params=_params(2, 2 * d * tn * 4 + 4 * COND_ROWS * (d + tn) * 4 + d * tn * 2),
        name="ada",
    )(cond, ada_w, ada_b.reshape(depth, 1, n))


def _mod_spec(layer, piece, row_fn, d, n_grid):
    if n_grid == 1:
        return pl.BlockSpec((None, None, 1, d), lambda i: (layer, row_fn(i), 0, piece))
    return pl.BlockSpec((None, None, 1, d), lambda i, j: (layer, row_fn(i), 0, piece))


def _modulated(x_ref, g_ref, sh_ref, sc_ref):
    xf = x_ref[...]
    ms = jnp.mean(xf * xf, axis=-1, keepdims=True)
    y = xf * lax.rsqrt(ms + NORM_EPS) * g_ref[...]
    return (y * (1.0 + sc_ref[...]) + sh_ref[...]).astype(BF16)


def _swap_halves_32(a, first_half):
    fwd = pltpu.roll(a, 3 * LANES // 4, axis=1)
    bwd = pltpu.roll(a, LANES // 4, axis=1)
    return jnp.where(first_half, fwd, bwd)


def _proj_kernel(*refs, with_gates, n_rope_tiles):
    x_ref, sh_ref, sc_ref, g_ref, w_ref = refs[:5]
    k = 5
    if with_gates:
        wg_ref, bg_ref = refs[k:k + 2]
        k += 2
    if n_rope_tiles:
        cos_ref, sin_ref = refs[k:k + 2]
        k += 2
    o_ref = refs[k]
    k += 1
    if with_gates:
        og_ref = refs[k]
        k += 1
    h_scr = refs[k]
    j = pl.program_id(1)

    @pl.when(j == 0)
    def _():
        h_scr[...] = _modulated(x_ref, g_ref, sh_ref, sc_ref)
        if with_gates:
            og_ref[...] = jnp.dot(h_scr[...], wg_ref[...], preferred_element_type=F32) + bg_ref[...]

    def product():
        return jnp.dot(h_scr[...], w_ref[...], preferred_element_type=F32)

    if n_rope_tiles:
        @pl.when(j < n_rope_tiles)
        def _():
            cos = cos_ref[...]
            sin = sin_ref[...]
            lane = lax.broadcasted_iota(jnp.int32, cos.shape, 1)
            first_half = (lane % (LANES // 2)) < (LANES // 4)
            for c in range(w_ref.shape[1] // MXU_COLS):
                acc = jnp.dot(h_scr[...], w_ref[:, c * MXU_COLS:(c + 1) * MXU_COLS],
                              preferred_element_type=F32)
                for half in range(MXU_COLS // LANES):
                    a = acc[:, half * LANES:(half + 1) * LANES]
                    r = a * cos + _swap_halves_32(a, first_half) * sin
                    lo = c * MXU_COLS + half * LANES
                    o_ref[:, lo:lo + LANES] = r.astype(o_ref.dtype)

        @pl.when(j >= n_rope_tiles)
        def _():
            o_ref[...] = product().astype(o_ref.dtype)
    else:
        o_ref[...] = product().astype(o_ref.dtype)


def _proj(x, mods, layer, row_fn, g, w, *, tm, tn, gates=None, rope=None, rope_cols=0):
    m, d = x.shape
    n = w.shape[1]
    tm = _tile(m, tm)
    tn = _tile(n, tn)
    in_specs = [
        pl.BlockSpec((tm, d), lambda i, j: (i, 0)),
        _mod_spec(layer, 0, row_fn, d, 2),
        _mod_spec(layer, 1, row_fn, d, 2),
        pl.BlockSpec((1, d), lambda i, j: (0, 0)),
        pl.BlockSpec((d, tn), lambda i, j: (0, j)),
    ]
    args = [x, mods, mods, g, w]
    out_specs = [pl.BlockSpec((tm, tn), lambda i, j: (i, j))]
    out_shape = [jax.ShapeDtypeStruct((m, n), BF16)]
    if gates is not None:
        wg, bg = gates
        in_specs += [pl.BlockSpec((d, LANES), lambda i, j: (0, 0)),
                     pl.BlockSpec((1, LANES), lambda i, j: (0, 0))]
        args += [wg, bg]
        out_specs.append(pl.BlockSpec((tm, LANES), lambda i, j: (i, 0)))
        out_shape.append(jax.ShapeDtypeStruct((m, LANES), F32))
    n_rope_tiles = 0
    if rope is not None:
        cos, sin = rope
        assert rope_cols % tn == 0 and cos.shape[0] % tm == 0
        n_rope_tiles = rope_cols // tn
        n_pos_tiles = cos.shape[0] // tm
        in_specs += [pl.BlockSpec((tm, LANES), lambda i, j: (i % n_pos_tiles, 0)),
                     pl.BlockSpec((tm, LANES), lambda i, j: (i % n_pos_tiles, 0))]
        args += [cos, sin]
    vmem = 2 * tm * d * 4 + tm * d * 2 + 2 * d * tn * 2 + 2 * tm * tn * 2 + 3 * tm * tn * 4 \
        + 4 * tm * LANES * 4 + 2 * d * LANES * 2 + 3 * tm * d * 4
    out = pl.pallas_call(
        functools.partial(_proj_kernel, with_gates=gates is not None, n_rope_tiles=n_rope_tiles),
        grid=(m // tm, n // tn),
        in_specs=in_specs,
        out_specs=out_specs,
        out_shape=out_shape,
        scratch_shapes=[pltpu.VMEM((tm, d), BF16)],
        compiler_params=_params(2, vmem),
        name="proj",
    )(*args)
    return out if gates is not None else out[0]


def _outproj_kernel(y_ref, w_ref, x_ref, gate_ref, g_ref, o_ref):
    t = jnp.dot(y_ref[...], w_ref[...], preferred_element_type=F32)
    ms = jnp.mean(t * t, axis=-1, keepdims=True)
    o_ref[...] = x_ref[...] + gate_ref[...] * (t * lax.rsqrt(ms + NORM_EPS) * g_ref[...])


def _outproj(y, w, x, mods, layer, row_fn, g, *, tm):
    m, d = x.shape
    kdim = y.shape[1]
    tm = _tile(m, tm)
    vmem = 2 * tm * kdim * 2 + 2 * kdim * d * 2 + 4 * tm * d * 4 + 3 * tm * d * 4
    return pl.pallas_call(
        _outproj_kernel,
        grid=(m // tm,),
        in_specs=[
            pl.BlockSpec((tm, kdim), lambda i: (i, 0)),
            pl.BlockSpec((kdim, d), lambda i: (0, 0)),
            pl.BlockSpec((tm, d), lambda i: (i, 0)),
            _mod_spec(layer, 2, row_fn, d, 1),
            pl.BlockSpec((1, d), lambda i: (0, 0)),
        ],
        out_specs=pl.BlockSpec((tm, d), lambda i: (i, 0)),
        out_shape=jax.ShapeDtypeStruct((m, d), F32),
        compiler_params=_params(1, vmem),
        name="outproj",
    )(y, w, x, mods, g)


def _mlp_kernel(x_ref, sh_ref, sc_ref, gate_ref, g_in_ref, g_out_ref, w1_ref, w2_ref, o_ref, h_scr):
    j = pl.program_id(1)

    @pl.when(j == 0)
    def _():
        h_scr[...] = _modulated(x_ref, g_in_ref, sh_ref, sc_ref)
        o_ref[...] = jnp.zeros_like(o_ref)

    a = jnp.dot(h_scr[...], w1_ref[...], preferred_element_type=F32)
    a = jnp.maximum(a, 0.0)
    a = (a * a).astype(BF16)
    o_ref[...] += jnp.dot(a, w2_ref[...], preferred_element_type=F32)

    @pl.when(j == pl.num_programs(1) - 1)
    def _():
        f = o_ref[...]
        ms = jnp.mean(f * f, axis=-1, keepdims=True)
        o_ref[...] = x_ref[...] + gate_ref[...] * (f * lax.rsqrt(ms + NORM_EPS) * g_out_ref[...])


def _mlp(x, mods, layer, row_fn, g_in, g_out, w1, w2, *, tm, tf):
    m, d = x.shape
    dff = w1.shape[1]
    tm = _tile(m, tm)
    tf = _tile(dff, tf)
    vmem = 4 * tm * d * 4 + tm * d * 2 + 4 * d * tf * 2 + 3 * tm * tf * 4 + 4 * 1024 * 1024
    return pl.pallas_call(
        _mlp_kernel,
        grid=(m // tm, dff // tf),
        in_specs=[
            pl.BlockSpec((tm, d), lambda i, j: (i, 0)),
            _mod_spec(layer, 3, row_fn, d, 2),
            _mod_spec(layer, 4, row_fn, d, 2),
            _mod_spec(layer, 5, row_fn, d, 2),
            pl.BlockSpec((1, d), lambda i, j: (0, 0)),
            pl.BlockSpec((1, d), lambda i, j: (0, 0)),
            pl.BlockSpec((d, tf), lambda i, j: (0, j)),
            pl.BlockSpec((tf, d), lambda i, j: (j, 0)),
        ],
        out_specs=pl.BlockSpec((tm, d), lambda i, j: (i, 0)),
        out_shape=jax.ShapeDtypeStruct((m, d), F32),
        scratch_shapes=[pltpu.VMEM((tm, d), BF16)],
        compiler_params=_params(2, vmem),
        name="mlp",
    )(x, mods, mods, mods, g_in, g_out, w1, w2)


def _split3(x):
    x1 = x.astype(BF16)
    r1 = x - x1.astype(F32)
    x2 = r1.astype(BF16)
    x3 = (r1 - x2.astype(F32)).astype(BF16)
    return x1, x2, x3


def _dot_nt(a, b):
    return lax.dot_general(a, b, (((1,), (1,)), ((), ())), preferred_element_type=F32)


def _dot_tn(a, b):
    return lax.dot_general(a, b, (((0,), (0,)), ((), ())), preferred_element_type=F32)


def _log_sigmoid(x):
    return jnp.minimum(x, 0.0) - jnp.log(1.0 + jnp.exp(-jnp.abs(x)))


def _gate_scan_kernel(x_ref, o_ref):
    x = x_ref[...]
    L = x.shape[1]
    kind = lax.broadcasted_iota(jnp.int32, x.shape, 0) % 4
    ui = lax.broadcasted_iota(jnp.int32, (L, L), 0)
    si = lax.broadcasted_iota(jnp.int32, (L, L), 1)
    xs = _split3(_log_sigmoid(x))
    fwd = sum(jnp.dot(p, (ui <= si).astype(BF16), preferred_element_type=F32) for p in xs)
    bwd = sum(jnp.dot(p, (ui >= si).astype(BF16), preferred_element_type=F32) for p in xs)
    o_ref[...] = jnp.where(kind == 1, fwd, jnp.where(kind == 3, bwd, x))


def _gate_scan(gates, nb, t, L):
    nc = t // L
    g = gates[:, :4 * ML_HEADS].reshape(nb, nc, L, 4, ML_HEADS)
    g = g.transpose(0, 4, 1, 3, 2).reshape(nb * ML_HEADS * nc * 4, L)
    tr = _tile(g.shape[0], 1024)
    assert tr % 4 == 0
    s = pl.pallas_call(
        _gate_scan_kernel,
        grid=(g.shape[0] // tr,),
        in_specs=[pl.BlockSpec((tr, L), lambda i: (i, 0))],
        out_specs=pl.BlockSpec((tr, L), lambda i: (i, 0)),
        out_shape=jax.ShapeDtypeStruct(g.shape, F32),
        compiler_params=_params(1, 24 * tr * max(L, LANES) * 4),
        name="gate_scan",
    )(g)
    s = s.reshape(nb, ML_HEADS, nc, 4, L)
    rows = jnp.pad(s, ((0, 0), (0, 0), (0, 0), (0, GATE_ROWS - 4), (0, 0)))
    cols = jnp.pad(s.transpose(0, 1, 2, 4, 3), ((0, 0), (0, 0), (0, 0), (0, 0), (0, GATE_ROWS - 4)))
    return rows, cols


def _mlstm_kernel(qc_ref, kc_ref, vc_ref, ogc_ref, grc_ref, gcc_ref,
                  ql_ref, kl_ref, vl_ref, ogl_ref, grl_ref, gcl_ref, hg_ref,
                  oc_ref, ol_ref, hf_c, hb_c, hf_l, hb_l):
    L = grc_ref.shape[-1]
    dk = qc_ref.shape[1] // ML_HB
    dv = vc_ref.shape[1] // ML_HB
    ncc = qc_ref.shape[0] // L
    ncl = ql_ref.shape[0] // L

    ti = lax.broadcasted_iota(jnp.int32, (L, L), 0)
    si = lax.broadcasted_iota(jnp.int32, (L, L), 1)

    def chunk(c, carry, refs, hh, backward):
        q_ref, k_ref, v_ref, gr_ref, gc_ref, hs_ref = refs
        C, n, m = carry
        mask = (si >= ti) if backward else (si <= ti)
        rows = pl.ds(pl.multiple_of(c * L, L), L)
        qb = q_ref[rows, hh * dk:(hh + 1) * dk]
        kb = k_ref[rows, hh * dk:(hh + 1) * dk]
        vb = v_ref[rows, hh * dv:(hh + 1) * dv]

        g_rows = gr_ref[hh, c]
        g_cols = gc_ref[hh, c]
        r = 2 if backward else 0
        ig_row = g_rows[r:r + 1, :]
        b_row = g_rows[r + 1:r + 2, :]
        ig_col = g_cols[:, r:r + 1]
        b_col = g_cols[:, r + 1:r + 2]
        b_end = b_row[:, 0:1] if backward else b_row[:, L - 1:L]

        c_row = ig_row - b_row
        dmat = jnp.where(mask, b_col + c_row, -jnp.inf)
        m_intra = jnp.max(dmat, axis=-1, keepdims=True)
        inter = b_col + m
        m_t = jnp.maximum(inter, m_intra)
        dexp = jnp.exp(dmat - m_t)
        w_inter = jnp.exp(inter - m_t)

        s = _dot_nt(qb, kb) * dexp
        qn = jnp.sum(qb.astype(F32) * n, axis=-1, keepdims=True)
        s_fold = s if L <= LANES else sum(s[:, j:j + LANES] for j in range(0, L, LANES))
        den = w_inter * qn + jnp.sum(s_fold, axis=-1, keepdims=True)
        num = w_inter * jnp.dot(qb, C.astype(BF16), preferred_element_type=F32) \
            + jnp.dot(s.astype(BF16), vb, preferred_element_type=F32)
        hs_ref[rows, hh * dv:(hh + 1) * dv] = num * (1.0 / jnp.maximum(jnp.abs(den), jnp.exp(-m_t)))

        g_col = b_end - b_col + ig_col
        m_new = jnp.maximum(b_end + m, jnp.max(g_col, axis=0, keepdims=True))
        a = jnp.exp(b_end + m - m_new)
        w_col = jnp.exp(g_col - m_new)
        C_new = a * C + _dot_tn(kb, (w_col * vb.astype(F32)).astype(BF16))
        n_new = a * n + jnp.sum(w_col * kb.astype(F32), axis=0, keepdims=True)
        return C_new, n_new, m_new

    def scan_segment(nc, q_ref, k_ref, v_ref, gr_ref, gc_ref, hf_ref, hb_ref, states):
        fwd_refs = (q_ref, k_ref, v_ref, gr_ref, gc_ref, hf_ref)
        bwd_refs = (q_ref, k_ref, v_ref, gr_ref, gc_ref, hb_ref)

        def body(c, st):
            out = []
            for hh in range(ML_HB):
                out.append(chunk(c, st[2 * hh], fwd_refs, hh, False))
                out.append(chunk(nc - 1 - c, st[2 * hh + 1], bwd_refs, hh, True))
            return tuple(out)

        return lax.fori_loop(0, nc, body, states)

    zero = (jnp.zeros((dk, dv), F32), jnp.zeros((1, dk), F32), jnp.zeros((1, 1), F32))
    st = scan_segment(ncc, qc_ref, kc_ref, vc_ref, grc_ref, gcc_ref, hf_c, hb_c, (zero,) * (2 * ML_HB))
    scan_segment(ncl, ql_ref, kl_ref, vl_ref, grl_ref, gcl_ref, hf_l, hb_l, st)

    def finish_segment(nc, hf_ref, hb_ref, og_ref, o_ref):
        def body(c, _):
            rows = pl.ds(pl.multiple_of(c * L, L), L)
            for hh in range(ML_HB):
                cols = slice(hh * dv, (hh + 1) * dv)
                hs = hf_ref[rows, cols] + hb_ref[rows, cols]
                hn = hs * lax.rsqrt(jnp.mean(hs * hs, axis=-1, keepdims=True) + NORM_EPS) * hg_ref[:, cols]
                o_ref[rows, cols] = (hn * jax.nn.sigmoid(og_ref[rows, cols].astype(F32))).astype(o_ref.dtype)
            return 0

        lax.fori_loop(0, nc, body, 0)

    finish_segment(ncc, hf_c, hb_c, ogc_ref, oc_ref)
    finish_segment(ncl, hf_l, hb_l, ogl_ref, ol_ref)


def _mlstm(p_ctx, g_ctx, p_lat, g_lat, head_g, nb, d):
    ctx_len = p_ctx.shape[0] // nb
    seq = p_lat.shape[0] // nb
    dv = ML_HB * (d // ML_HEADS)
    dk = dv // 2
    nh = ML_HEADS // ML_HB
    L = math.gcd(math.gcd(ctx_len, seq), ML_BLOCK)
    grc, gcc = _gate_scan(g_ctx, nb, ctx_len, L)
    grl, gcl = _gate_scan(g_lat, nb, seq, L)

    def stream_specs(t):
        return [
            pl.BlockSpec((t, dk), lambda b, i: (b, i)),
            pl.BlockSpec((t, dk), lambda b, i: (b, nh + i)),
            pl.BlockSpec((t, dv), lambda b, i: (b, nh + i)),
            pl.BlockSpec((t, dv), lambda b, i: (b, 2 * nh + i)),
            pl.BlockSpec((None, ML_HB, t // L, GATE_ROWS, L), lambda b, i: (b, i, 0, 0, 0)),
            pl.BlockSpec((None, ML_HB, t // L, L, GATE_ROWS), lambda b, i: (b, i, 0, 0, 0)),
        ]

    t_all = ctx_len + seq
    gate_tiles = 2 * ML_HB * (t_all // L) * (GATE_ROWS + L) * max(L, LANES) * 4
    vmem = 2 * t_all * (2 * dk * 2 + 3 * dv * 2) + 2 * t_all * dv * 4 + gate_tiles + 8 * 1024 * 1024
    return pl.pallas_call(
        _mlstm_kernel,
        grid=(nb, nh),
        in_specs=stream_specs(ctx_len) + stream_specs(seq) + [pl.BlockSpec((1, dv), lambda b, i: (0, i))],
        out_specs=[pl.BlockSpec((ctx_len, dv), lambda b, i: (b, i)),
                   pl.BlockSpec((seq, dv), lambda b, i: (b, i))],
        out_shape=[jax.ShapeDtypeStruct((nb * ctx_len, d), BF16),
                   jax.ShapeDtypeStruct((nb * seq, d), BF16)],
        scratch_shapes=[pltpu.VMEM((ctx_len, dv), F32), pltpu.VMEM((ctx_len, dv), F32),
                        pltpu.VMEM((seq, dv), F32), pltpu.VMEM((seq, dv), F32)],
        compiler_params=_params(2, vmem),
        name="mlstm",
    )(p_ctx, p_ctx, p_ctx, p_ctx, grc, gcc, p_lat, p_lat, p_lat, p_lat, grl, gcl, head_g)


def _fold_lanes(x, op):
    slabs = [x[:, j:j + LANES] for j in range(0, x.shape[1], LANES)]
    return functools.reduce(op, slabs)


def _attn_kernel(q_ref, kl_ref, vl_ref, kc_ref, vc_ref, lam_ref, gs_ref, o_ref, *, lam_init):
    dh = q_ref.shape[1] // 2
    lq = lam_ref[...]
    lam = jnp.exp(jnp.sum(lq[0:1] * lq[1:2], axis=-1, keepdims=True)) \
        - jnp.exp(jnp.sum(lq[2:3] * lq[3:4], axis=-1, keepdims=True)) + lam_init

    seq = q_ref.shape[0]
    ctx_len = kc_ref.shape[0]
    tq = _tile(seq, ATTN_TQ)
    tk = _tile(seq, ATTN_TK)
    blocks = [(kl_ref, vl_ref, j * tk, tk) for j in range(seq // tk)] + [(kc_ref, vc_ref, 0, ctx_len)]

    def q_tile(t, _):
        rows = pl.ds(pl.multiple_of(t * tq, tq), tq)
        q0 = q_ref[rows, 0:dh]
        q1 = q_ref[rows, dh:2 * dh]
        m = l = acc = None
        for k_ref, v_ref, start, size in blocks:
            s = jnp.concatenate([_dot_nt(q0, k_ref[start:start + size, 0:dh]),
                                 _dot_nt(q1, k_ref[start:start + size, dh:2 * dh])], axis=0)
            mx = jnp.max(_fold_lanes(s, jnp.maximum), axis=-1, keepdims=True)
            m_new = mx if m is None else jnp.maximum(m, mx)
            p = jnp.exp2(s - m_new)
            ps = jnp.sum(_fold_lanes(p, jnp.add), axis=-1, keepdims=True)
            pv = jnp.dot(p.astype(BF16), v_ref[start:start + size, :], preferred_element_type=F32)
            if m is None:
                l, acc = ps, pv
            else:
                alpha = jnp.exp2(m - m_new)
                l = alpha * l + ps
                acc = alpha * acc + pv
            m = m_new
        a = acc * (1.0 / l)
        o = a[0:tq] - lam * a[tq:2 * tq]
        on = o * lax.rsqrt(jnp.mean(o * o, axis=-1, keepdims=True) + NORM_EPS) * gs_ref[...] * (1.0 - lam_init)
        o_ref[rows, :] = on.astype(o_ref.dtype)
        return 0

    lax.fori_loop(0, seq // tq, q_tile, 0)


def _attn(p_lat, p_ctx, lam_qk, g_sub, nb, d, lam_init):
    seq = p_lat.shape[0] // nb
    ctx_len = p_ctx.shape[0] // nb
    h = DA_HEADS
    hd = d // h
    tq = _tile(seq, ATTN_TQ)
    tk = _tile(seq, ATTN_TK)
    vmem = 2 * (4 * seq + 2 * ctx_len) * hd * 2 + 12 * tq * tk * 4 + 8 * tq * hd * 4 + 4 * 1024 * 1024
    return pl.pallas_call(
        functools.partial(_attn_kernel, lam_init=lam_init),
        grid=(nb, h),
        in_specs=[
            pl.BlockSpec((seq, hd), lambda b, i: (b, i)),
            pl.BlockSpec((seq, hd), lambda b, i: (b, h + i)),
            pl.BlockSpec((seq, hd), lambda b, i: (b, 2 * h + i)),
            pl.BlockSpec((ctx_len, hd), lambda b, i: (b, i)),
            pl.BlockSpec((ctx_len, hd), lambda b, i: (b, h + i)),
            pl.BlockSpec(lam_qk.shape, lambda b, i: (0, 0)),
            pl.BlockSpec((1, hd), lambda b, i: (0, 0)),
        ],
        out_specs=pl.BlockSpec((seq, hd), lambda b, i: (b, i)),
        out_shape=jax.ShapeDtypeStruct((nb * seq, d), BF16),
        compiler_params=_params(2, vmem),
        name="attn",
    )(p_lat, p_lat, p_lat, p_ctx, p_ctx, lam_qk, g_sub)


def _rope_tables(seq, dh):
    nf = dh // 4
    n = jnp.arange(seq)
    row = (n // GRID_W).astype(F32)
    col = (n % GRID_W).astype(F32)
    freq = ROPE_BASE ** (-jnp.arange(nf, dtype=F32) / nf)
    ar = row[:, None] * freq
    ac = col[:, None] * freq
    cos = jnp.concatenate([jnp.cos(ar), jnp.cos(ar), jnp.cos(ac), jnp.cos(ac)], axis=1)
    sin = jnp.concatenate([-jnp.sin(ar), jnp.sin(ar), -jnp.sin(ac), jnp.sin(ac)], axis=1)
    return cos, sin


def kernel(x, c, ctx, c_ctx, ada_w, ada_b, norm_g, mlp_w1, mlp_w2, ml_w_in, ml_b_gates, ml_head_g,
           ml_w_out, da_w_in, da_lambda, da_sub_g, da_w_out):
    nb, seq, d = x.shape
    ctx_len = ctx.shape[1]
    depth = ada_w.shape[0]
    assert depth == 2 and nb < COND_ROWS and d // ML_HEADS == 2 * LANES
    assert seq % 64 == 0 and ctx_len % 64 == 0 and seq % GRID_W == 0

    xl = x.reshape(nb * seq, d)
    xc = ctx.reshape(nb * ctx_len, d)

    cond = jnp.zeros((COND_ROWS, d), F32).at[:nb].set(c).at[nb].set(c_ctx)
    mods = _ada(cond, ada_w, ada_b).reshape(depth, COND_ROWS, 1, N_MOD * d)

    tm = _tile(seq, 512)
    tm_mlp = _tile(seq, 1024)
    tmc = _tile(nb * ctx_len, 512)

    def lat_row_of(tile_rows):
        return lambda i: i // (seq // tile_rows)

    lat_row = lat_row_of(tm)
    lat_row_mlp = lat_row_of(tm_mlp)

    def ctx_row(i):
        return nb

    g = norm_g.reshape(depth, 4, 1, d)

    ml_qk = ML_HEADS * (d // ML_HEADS // 2)
    n_main = 2 * ml_qk + 2 * d
    w_in = ml_w_in[0]
    q_scale = (d // ML_HEADS // 2) ** -0.5
    w_main = jnp.concatenate([w_in[:, :ml_qk] * q_scale, w_in[:, ml_qk:n_main]], axis=1).astype(BF16)
    n_gate = 4 * ML_HEADS
    w_gate = jnp.pad(w_in[:, n_main:], ((0, 0), (0, LANES - n_gate))).astype(BF16)
    b_gate = jnp.pad(ml_b_gates[0], (0, LANES - n_gate)).reshape(1, LANES)
    w_out0 = ml_w_out[0].astype(BF16)
    w1_0 = mlp_w1[0].astype(BF16)
    w2_0 = mlp_w2[0].astype(BF16)

    p_lat, g_lat = _proj(xl, mods, 0, lat_row, g[0, 0], w_main, tm=tm, tn=1024, gates=(w_gate, b_gate))
    p_ctx, g_ctx = _proj(xc, mods, 0, ctx_row, g[0, 0], w_main, tm=tmc, tn=1024, gates=(w_gate, b_gate))
    y_ctx, y_lat = _mlstm(p_ctx, g_ctx, p_lat, g_lat, ml_head_g[0].reshape(1, d), nb, d)
    xl = _outproj(y_lat, w_out0, xl, mods, 0, lat_row, g[0, 1], tm=tm)
    xc = _outproj(y_ctx, w_out0, xc, mods, 0, ctx_row, g[0, 1], tm=tmc)
    xl = _mlp(xl, mods, 0, lat_row_mlp, g[0, 2], g[0, 3], w1_0, w2_0, tm=tm_mlp, tf=512)
    xc = _mlp(xc, mods, 0, ctx_row, g[0, 2], g[0, 3], w1_0, w2_0, tm=tmc, tf=512)

    dh = d // (2 * DA_HEADS)
    w_in = da_w_in[0]
    w_qkv = jnp.concatenate([w_in[:, :d] * (dh ** -0.5 * math.log2(math.e)), w_in[:, d:]], axis=1).astype(BF16)
    w_kv = w_in[:, d:].astype(BF16)
    w_out1 = da_w_out[0].astype(BF16)
    w1_1 = mlp_w1[1].astype(BF16)
    w2_1 = mlp_w2[1].astype(BF16)
    layer_idx = 1
    lam_init = 0.8 - 0.6 * math.exp(-0.3 * layer_idx)

    q_lat = _proj(xl, mods, 1, lat_row, g[1, 0], w_qkv, tm=tm, tn=1024,
                  rope=_rope_tables(seq, dh), rope_cols=2 * d)
    kv_ctx = _proj(xc, mods, 1, ctx_row, g[1, 0], w_kv, tm=tmc, tn=1024)
    y_lat = _attn(q_lat, kv_ctx, da_lambda[0], da_sub_g[0].reshape(1, 2 * dh), nb, d, lam_init)
    xl = _outproj(y_lat, w_out1, xl, mods, 1, lat_row, g[1, 1], tm=tm)
    xl = _mlp(xl, mods, 1, lat_row_mlp, g[1, 2], g[1, 3], w1_1, w2_1, tm=tm_mlp, tf=512)
    return xl.reshape(nb, seq, d)
```

```python
import functools
import math

import jax
import jax.numpy as jnp
from jax import lax
from jax.experimental import pallas as pl
from jax.experimental.pallas import tpu as pltpu

F32 = jnp.float32
BF16 = jnp.bfloat16

NORM_EPS = 1e-6
GRID_W = 64
ROPE_BASE = 10000.0
LOG2E = math.log2(math.e)
ML_HEADS = 8
ML_BLOCK = 256
ML_HB = 2
ML_AUG = 16
DA_HEADS = 8
ATTN_TQ = 256
ATTN_TK = 512
N_MOD = 6
COND_ROWS = 32
LANES = 128
MXU_COLS = 256
GATE_ROWS = 8

VMEM_CAP_BYTES = 56 * 1024 * 1024


def _vmem_limit(nbytes):
    return int(min(VMEM_CAP_BYTES, max(16 * 1024 * 1024, nbytes)))


def _params(n_grid, vmem_bytes):
    return pltpu.CompilerParams(
        dimension_semantics=("arbitrary",) * n_grid,
        vmem_limit_bytes=_vmem_limit(vmem_bytes))


def _tile(n, want):
    if n <= want:
        return n
    t = want
    while t >= 8:
        if n % t == 0 and t % 8 == 0:
            return t
        t -= 8
    return n


def _dot_nt(a, b):
    return lax.dot_general(a, b, (((1,), (1,)), ((), ())), preferred_element_type=F32)


def _ada_kernel(c_ref, w_ref, b_ref, o_ref):
    cf = c_ref[...]
    s = (cf * jax.nn.sigmoid(cf)).astype(BF16)
    o_ref[...] = jnp.dot(s, w_ref[...].astype(BF16), preferred_element_type=F32) + b_ref[...]


def _ada(cond, ada_w, ada_b):
    depth, d, n = ada_w.shape
    tn = _tile(n, 1024)
    return pl.pallas_call(
        _ada_kernel,
        grid=(depth, n // tn),
        in_specs=[
            pl.BlockSpec((COND_ROWS, d), lambda l, j: (0, 0)),
            pl.BlockSpec((None, d, tn), lambda l, j: (l, 0, j)),
            pl.BlockSpec((None, 1, tn), lambda l, j: (l, 0, j)),
        ],
        out_specs=pl.BlockSpec((None, COND_ROWS, tn), lambda l, j: (l, 0, j)),
        out_shape=jax.ShapeDtypeStruct((depth, COND_ROWS, n), F32),
        compiler_params=_params(2, 2 * d * tn * 4 + 4 * COND_ROWS * (d + tn) * 4 + d * tn * 2),
        name="ada",
    )(cond, ada_w, ada_b.reshape(depth, 1, n))


def _mod_spec(layer, piece, row_fn, d, n_grid):
    if n_grid == 1:
        return pl.BlockSpec((None, None, 1, d), lambda i: (layer, row_fn(i), 0, piece))
    return pl.BlockSpec((None, None, 1, d), lambda i, j: (layer, row_fn(i), 0, piece))


def _modulated(x_ref, g_ref, sh_ref, sc_ref):
    xf = x_ref[...]
    ms = jnp.mean(xf * xf, axis=-1, keepdims=True)
    y = xf * lax.rsqrt(ms + NORM_EPS) * g_ref[...]
    return (y * (1.0 + sc_ref[...]) + sh_ref[...]).astype(BF16)


def _swap_halves_32(a, first_half):
    fwd = pltpu.roll(a, 3 * LANES // 4, axis=1)
    bwd = pltpu.roll(a, LANES // 4, axis=1)
    return jnp.where(first_half, fwd, bwd)


def _proj_kernel(*refs, n_rope_tiles):
    x_ref, sh_ref, sc_ref, g_ref, w_ref = refs[:5]
    if n_rope_tiles:
        cos_ref, sin_ref = refs[5:7]
    o_ref, h_scr = refs[-2:]
    j = pl.program_id(1)

    @pl.when(j == 0)
    def _():
        h_scr[...] = _modulated(x_ref, g_ref, sh_ref, sc_ref)

    def product():
        return jnp.dot(h_scr[...], w_ref[...], preferred_element_type=F32)

    if n_rope_tiles:
        @pl.when(j < n_rope_tiles)
        def _():
            cos = cos_ref[...]
            sin = sin_ref[...]
            lane = lax.broadcasted_iota(jnp.int32, cos.shape, 1)
            first_half = (lane % (LANES // 2)) < (LANES // 4)
            for c in range(w_ref.shape[1] // MXU_COLS):
                acc = jnp.dot(h_scr[...], w_ref[:, c * MXU_COLS:(c + 1) * MXU_COLS],
                              preferred_element_type=F32)
                for half in range(MXU_COLS // LANES):
                    a = acc[:, half * LANES:(half + 1) * LANES]
                    r = a * cos + _swap_halves_32(a, first_half) * sin
                    lo = c * MXU_COLS + half * LANES
                    o_ref[:, lo:lo + LANES] = r.astype(o_ref.dtype)

        @pl.when(j >= n_rope_tiles)
        def _():
            o_ref[...] = product().astype(o_ref.dtype)
    else:
        o_ref[...] = product().astype(o_ref.dtype)


def _proj(x, mods, layer, row_fn, g, w, *, tm, tn, rope=None, rope_cols=0):
    m, d = x.shape
    n = w.shape[1]
    tm = _tile(m, tm)
    tn = _tile(n, tn)
    in_specs = [
        pl.BlockSpec((tm, d), lambda i, j: (i, 0)),
        _mod_spec(layer, 0, row_fn, d, 2),
        _mod_spec(layer, 1, row_fn, d, 2),
        pl.BlockSpec((1, d), lambda i, j: (0, 0)),
        pl.BlockSpec((d, tn), lambda i, j: (0, j)),
    ]
    args = [x, mods, mods, g, w]
    n_rope_tiles = 0
    if rope is not None:
        cos, sin = rope
        assert rope_cols % tn == 0 and cos.shape[0] % tm == 0
        n_rope_tiles = rope_cols // tn
        n_pos_tiles = cos.shape[0] // tm
        in_specs += [pl.BlockSpec((tm, LANES), lambda i, j: (i % n_pos_tiles, 0)),
                     pl.BlockSpec((tm, LANES), lambda i, j: (i % n_pos_tiles, 0))]
        args += [cos, sin]
    vmem = 2 * tm * d * 4 + tm * d * 2 + 2 * d * tn * 2 + 2 * tm * tn * 2 + 3 * tm * tn * 4 \
        + 4 * tm * LANES * 4 + 3 * tm * d * 4
    return pl.pallas_call(
        functools.partial(_proj_kernel, n_rope_tiles=n_rope_tiles),
        grid=(m // tm, n // tn),
        in_specs=in_specs,
        out_specs=pl.BlockSpec((tm, tn), lambda i, j: (i, j)),
        out_shape=jax.ShapeDtypeStruct((m, n), BF16),
        scratch_shapes=[pltpu.VMEM((tm, d), BF16)],
        compiler_params=_params(2, vmem),
        name="proj",
    )(*args)


def _proj_ml_kernel(x_ref, sh_ref, sc_ref, g_ref, wk_ref, wt_ref, wg_ref, bg_ref,
                    k_ref, ft_ref, gate_ref, h_scr):
    j = pl.program_id(1)

    @pl.when(j == 0)
    def _():
        h_scr[...] = _modulated(x_ref, g_ref, sh_ref, sc_ref)
        h = h_scr[...]
        gate_ref[...] = jnp.dot(h, wg_ref[...], preferred_element_type=F32) + bg_ref[...]
        k_ref[...] = jnp.dot(h, wk_ref[...], preferred_element_type=F32).astype(k_ref.dtype)

    @pl.when(j > 0)
    def _():
        ft_ref[...] = _dot_nt(wt_ref[...], h_scr[...]).astype(ft_ref.dtype)


def _proj_ml(x, mods, layer, row_fn, g, wk, wt, wg, bg, *, tm, tn):
    m, d = x.shape
    nk = wk.shape[1]
    nf = wt.shape[0]
    tm = _tile(m, tm)
    tn = _tile(nf, tn)

    def feat_tile(j):
        return jnp.maximum(j - 1, 0)

    vmem = 2 * tm * d * 4 + tm * d * 2 + 2 * d * nk * 2 + 2 * tn * d * 2 + 2 * tm * nk * 2 + 2 * tn * tm * 2 \
        + 2 * tm * max(nk, tn) * 4 + 4 * tm * LANES * 4 + 2 * d * LANES * 2 + 3 * tm * d * 4
    return pl.pallas_call(
        _proj_ml_kernel,
        grid=(m // tm, 1 + nf // tn),
        in_specs=[
            pl.BlockSpec((tm, d), lambda i, j: (i, 0)),
            _mod_spec(layer, 0, row_fn, d, 2),
            _mod_spec(layer, 1, row_fn, d, 2),
            pl.BlockSpec((1, d), lambda i, j: (0, 0)),
            pl.BlockSpec((d, nk), lambda i, j: (0, 0)),
            pl.BlockSpec((tn, d), lambda i, j: (feat_tile(j), 0)),
            pl.BlockSpec((d, LANES), lambda i, j: (0, 0)),
            pl.BlockSpec((1, LANES), lambda i, j: (0, 0)),
        ],
        out_specs=[pl.BlockSpec((tm, nk), lambda i, j: (i, 0)),
                   pl.BlockSpec((tn, tm), lambda i, j: (feat_tile(j), i)),
                   pl.BlockSpec((tm, LANES), lambda i, j: (i, 0))],
        out_shape=[jax.ShapeDtypeStruct((m, nk), BF16),
                   jax.ShapeDtypeStruct((nf, m), BF16),
                   jax.ShapeDtypeStruct((m, LANES), F32)],
        scratch_shapes=[pltpu.VMEM((tm, d), BF16)],
        compiler_params=_params(2, vmem),
        name="proj_ml",
    )(x, mods, mods, g, wk, wt, wg, bg)


def _outproj_kernel(y_ref, w_ref, x_ref, gate_ref, g_ref, o_ref):
    t = jnp.dot(y_ref[...], w_ref[...], preferred_element_type=F32)
    ms = jnp.mean(t * t, axis=-1, keepdims=True)
    o_ref[...] = x_ref[...] + gate_ref[...] * (t * lax.rsqrt(ms + NORM_EPS) * g_ref[...])


def _outproj(y, w, x, mods, layer, row_fn, g, *, tm):
    m, d = x.shape
    kdim = y.shape[1]
    tm = _tile(m, tm)
    vmem = 2 * tm * kdim * 2 + 2 * kdim * d * 2 + 4 * tm * d * 4 + 3 * tm * d * 4
    return pl.pallas_call(
        _outproj_kernel,
        grid=(m // tm,),
        in_specs=[
            pl.BlockSpec((tm, kdim), lambda i: (i, 0)),
            pl.BlockSpec((kdim, d), lambda i: (0, 0)),
            pl.BlockSpec((tm, d), lambda i: (i, 0)),
            _mod_spec(layer, 2, row_fn, d, 1),
            pl.BlockSpec((1, d), lambda i: (0, 0)),
        ],
        out_specs=pl.BlockSpec((tm, d), lambda i: (i, 0)),
        out_shape=jax.ShapeDtypeStruct((m, d), F32),
        compiler_params=_params(1, vmem),
        name="outproj",
    )(y, w, x, mods, g)


def _mlp_kernel(x_ref, sh_ref, sc_ref, gate_ref, g_in_ref, g_out_ref, w1_ref, w2_ref, o_ref, h_scr):
    j = pl.program_id(1)

    @pl.when(j == 0)
    def _():
        h_scr[...] = _modulated(x_ref, g_in_ref, sh_ref, sc_ref)
        o_ref[...] = jnp.zeros_like(o_ref)

    a = jnp.dot(h_scr[...], w1_ref[...], preferred_element_type=F32)
    a = jnp.maximum(a, 0.0)
    a = (a * a).astype(BF16)
    o_ref[...] += jnp.dot(a, w2_ref[...], preferred_element_type=F32)

    @pl.when(j == pl.num_programs(1) - 1)
    def _():
        f = o_ref[...]
        ms = jnp.mean(f * f, axis=-1, keepdims=True)
        o_ref[...] = x_ref[...] + gate_ref[...] * (f * lax.rsqrt(ms + NORM_EPS) * g_out_ref[...])


def _mlp(x, mods, layer, row_fn, g_in, g_out, w1, w2, *, tm, tf):
    m, d = x.shape
    dff = w1.shape[1]
    tm = _tile(m, tm)
    tf = _tile(dff, tf)
    vmem = 4 * tm * d * 4 + tm * d * 2 + 4 * d * tf * 2 + 3 * tm * tf * 4 + 4 * 1024 * 1024
    return pl.pallas_call(
        _mlp_kernel,
        grid=(m // tm, dff // tf),
        in_specs=[
            pl.BlockSpec((tm, d), lambda i, j: (i, 0)),
            _mod_spec(layer, 3, row_fn, d, 2),
            _mod_spec(layer, 4, row_fn, d, 2),
            _mod_spec(layer, 5, row_fn, d, 2),
            pl.BlockSpec((1, d), lambda i, j: (0, 0)),
            pl.BlockSpec((1, d), lambda i, j: (0, 0)),
            pl.BlockSpec((d, tf), lambda i, j: (0, j)),
            pl.BlockSpec((tf, d), lambda i, j: (j, 0)),
        ],
        out_specs=pl.BlockSpec((tm, d), lambda i, j: (i, 0)),
        out_shape=jax.ShapeDtypeStruct((m, d), F32),
        scratch_shapes=[pltpu.VMEM((tm, d), BF16)],
        compiler_params=_params(2, vmem),
        name="mlp",
    )(x, mods, mods, mods, g_in, g_out, w1, w2)


def _split3(x):
    x1 = x.astype(BF16)
    r1 = x - x1.astype(F32)
    x2 = r1.astype(BF16)
    x3 = (r1 - x2.astype(F32)).astype(BF16)
    return x1, x2, x3


def _log_sigmoid(x):
    return jnp.minimum(x, 0.0) - jnp.log(1.0 + jnp.exp(-jnp.abs(x)))


def _running_max_lanes(x, reverse):
    lane = lax.broadcasted_iota(jnp.int32, (x.shape[0], LANES), 1)
    slabs = [x[:, j:j + LANES] for j in range(0, x.shape[1], LANES)]
    order = range(len(slabs) - 1, -1, -1) if reverse else range(len(slabs))
    carry = None
    for j in order:
        y = slabs[j]
        k = 1
        while k < LANES:
            if reverse:
                y = jnp.maximum(y, jnp.where(lane < LANES - k, pltpu.roll(y, LANES - k, axis=1), -jnp.inf))
            else:
                y = jnp.maximum(y, jnp.where(lane >= k, pltpu.roll(y, k, axis=1), -jnp.inf))
            k *= 2
        if carry is not None:
            y = jnp.maximum(y, carry)
        carry = jnp.max(y, axis=-1, keepdims=True)
        slabs[j] = y
    return jnp.concatenate(slabs, axis=1)


def _gate_scan_kernel(x_ref, o_ref):
    L = x_ref.shape[-1]
    ui = lax.broadcasted_iota(jnp.int32, (L, L), 0)
    si = lax.broadcasted_iota(jnp.int32, (L, L), 1)

    def cumulative(z, tri):
        return sum(jnp.dot(p, tri.astype(BF16), preferred_element_type=F32) for p in _split3(z))

    b_f = cumulative(_log_sigmoid(x_ref[1]), ui <= si) * LOG2E
    b_b = cumulative(_log_sigmoid(x_ref[3]), ui >= si) * LOG2E
    c_f = x_ref[0] * LOG2E - b_f
    c_b = x_ref[2] * LOG2E - b_b
    o_ref[0] = b_f
    o_ref[1] = c_f
    o_ref[2] = _running_max_lanes(c_f, False)
    o_ref[3] = b_b
    o_ref[4] = c_b
    o_ref[5] = _running_max_lanes(c_b, True)


def _gate_scan(gates, nb, t, L):
    nc = t // L
    g = gates[:, :4 * ML_HEADS].reshape(nb, nc, L, 4, ML_HEADS)
    g = g.transpose(3, 0, 4, 1, 2).reshape(4, nb * ML_HEADS * nc, L)
    nr = g.shape[1]
    tr = _tile(nr, 256)
    n_out = 6
    s = pl.pallas_call(
        _gate_scan_kernel,
        grid=(nr // tr,),
        in_specs=[pl.BlockSpec((4, tr, L), lambda i: (0, i, 0))],
        out_specs=pl.BlockSpec((n_out, tr, L), lambda i: (0, i, 0)),
        out_shape=jax.ShapeDtypeStruct((n_out, nr, L), F32),
        compiler_params=_params(1, 48 * tr * max(L, LANES) * 4),
        name="gate_scan",
    )(g)
    s = s.reshape(n_out, nb, ML_HEADS, nc, L)
    rows = jnp.pad(s.transpose(1, 2, 3, 0, 4), ((0, 0), (0, 0), (0, 0), (0, GATE_ROWS - n_out), (0, 0)))
    cols = jnp.pad(s[1::3].transpose(1, 2, 3, 4, 0), ((0, 0), (0, 0), (0, 0), (0, 0), (0, GATE_ROWS - 2)))
    return rows, cols


def _mlstm_kernel(kc_ref, qtc_ref, vtc_ref, ogtc_ref, grc_ref, gcc_ref,
                  kl_ref, qtl_ref, vtl_ref, ogtl_ref, grl_ref, gcl_ref, hg_ref,
                  oc_ref, ol_ref, hf_c, hb_c, hf_l, hb_l):
    L = grc_ref.shape[-1]
    dk = kc_ref.shape[1] // ML_HB
    dv = vtc_ref.shape[0] // ML_HB
    ncc = kc_ref.shape[0] // L
    ncl = kl_ref.shape[0] // L

    si = lax.broadcasted_iota(jnp.int32, (L, L), 0)
    ti = lax.broadcasted_iota(jnp.int32, (L, L), 1)
    ones_rows = (lax.broadcasted_iota(jnp.int32, (ML_AUG, L), 0) == 0).astype(BF16)

    def chunk(refs, c, hh, backward, state):
        k_ref, qt_ref, vt_ref, gr_ref, gc_ref, hs_ref = refs
        caug, m = state
        tok = slice(c * L, (c + 1) * L)
        kb = k_ref[tok, hh * dk:(hh + 1) * dk]
        qt = qt_ref[hh * dk:(hh + 1) * dk, tok]
        vt_aug = jnp.concatenate([vt_ref[hh * dv:(hh + 1) * dv, tok], ones_rows], axis=0)
        rows = gr_ref[hh, c]
        cols = gc_ref[hh, c]
        r = 3 if backward else 0
        b_row = rows[r:r + 1, :]
        c_row = rows[r + 1:r + 2, :]
        c_max = rows[r + 2:r + 3, :]
        c_col = cols[:, 1:2] if backward else cols[:, 0:1]
        b_end = b_row[:, 0:1] if backward else b_row[:, L - 1:L]

        inter = b_row + m
        m_row = jnp.maximum(inter, b_row + c_max)
        mask = (si >= ti) if backward else (si <= ti)
        decay = jnp.where(mask, jnp.exp2(c_col + (b_row - m_row)), 0.0)
        w_row = jnp.exp2(inter - m_row)

        p_t = (jnp.dot(kb, qt, preferred_element_type=F32) * decay).astype(BF16)
        lhs = jnp.concatenate([caug.astype(BF16), vt_aug], axis=1)
        rhs = jnp.concatenate([qt * w_row.astype(BF16), p_t], axis=0)
        out = jnp.dot(lhs, rhs, preferred_element_type=F32)
        den = out[dv:dv + 1, :]
        hs_ref[hh * dv:(hh + 1) * dv, tok] = out[0:dv, :] * (1.0 / jnp.maximum(jnp.abs(den), jnp.exp2(-m_row)))

        g_row = b_end + c_row
        m_new = jnp.maximum(b_end + m, jnp.max(g_row, axis=-1, keepdims=True))
        a = jnp.exp2(b_end + m - m_new)
        ws = jnp.exp2(g_row - m_new)
        caug_new = a * caug + jnp.dot(vt_aug * ws.astype(BF16), kb, preferred_element_type=F32)
        return caug_new, m_new

    def scan_segment(nc, k_ref, qt_ref, vt_ref, gr_ref, gc_ref, hf_ref, hb_ref, states):
        fwd_refs = (k_ref, qt_ref, vt_ref, gr_ref, gc_ref, hf_ref)
        bwd_refs = (k_ref, qt_ref, vt_ref, gr_ref, gc_ref, hb_ref)
        for step in range(nc):
            nxt = []
            for hh in range(ML_HB):
                nxt.append(chunk(fwd_refs, step, hh, False, states[2 * hh]))
                nxt.append(chunk(bwd_refs, nc - 1 - step, hh, True, states[2 * hh + 1]))
            states = nxt
        return states

    zero = (jnp.zeros((dv + ML_AUG, dk), F32), jnp.zeros((1, 1), F32))
    st = scan_segment(ncc, kc_ref, qtc_ref, vtc_ref, grc_ref, gcc_ref, hf_c, hb_c, [zero] * (2 * ML_HB))
    scan_segment(ncl, kl_ref, qtl_ref, vtl_ref, grl_ref, gcl_ref, hf_l, hb_l, st)

    def finish_segment(nc, hf_ref, hb_ref, ogt_ref, o_ref):
        for c in range(nc):
            tok = slice(c * L, (c + 1) * L)
            for hh in range(ML_HB):
                feat = slice(hh * dv, (hh + 1) * dv)
                hs = hf_ref[feat, tok] + hb_ref[feat, tok]
                scale = lax.rsqrt(jnp.mean(hs * hs, axis=0, keepdims=True) + NORM_EPS)
                head_g = jnp.concatenate([hg_ref[feat, :]] * (L // LANES), axis=1)
                y_t = hs * scale * head_g * jax.nn.sigmoid(ogt_ref[feat, tok].astype(F32))
                o_ref[tok, feat] = y_t.T.astype(o_ref.dtype)

    finish_segment(ncc, hf_c, hb_c, ogtc_ref, oc_ref)
    finish_segment(ncl, hf_l, hb_l, ogtl_ref, ol_ref)


def _mlstm(k_ctx, ft_ctx, g_ctx, k_lat, ft_lat, g_lat, head_g, nb, d):
    ctx_len = k_ctx.shape[0] // nb
    seq = k_lat.shape[0] // nb
    dv = ML_HB * (d // ML_HEADS)
    dk = dv // 2
    nh = ML_HEADS // ML_HB
    q_rows = ML_HEADS * (d // ML_HEADS // 2)
    v_blk = q_rows // dv
    og_blk = (q_rows + d) // dv
    L = math.gcd(math.gcd(ctx_len, seq), ML_BLOCK)
    assert L % LANES == 0
    grc, gcc = _gate_scan(g_ctx, nb, ctx_len, L)
    grl, gcl = _gate_scan(g_lat, nb, seq, L)
    hg = jnp.broadcast_to(head_g.reshape(d, 1), (d, LANES))

    def stream_specs(t):
        return [
            pl.BlockSpec((t, dk), lambda b, i: (b, i)),
            pl.BlockSpec((dk, t), lambda b, i: (i, b)),
            pl.BlockSpec((dv, t), lambda b, i: (v_blk + i, b)),
            pl.BlockSpec((dv, t), lambda b, i: (og_blk + i, b)),
            pl.BlockSpec((None, ML_HB, t // L, GATE_ROWS, L), lambda b, i: (b, i, 0, 0, 0)),
            pl.BlockSpec((None, ML_HB, t // L, L, GATE_ROWS), lambda b, i: (b, i, 0, 0, 0)),
        ]

    t_all = ctx_len + seq
    gate_tiles = 2 * ML_HB * (t_all // L) * (GATE_ROWS + L) * max(L, LANES) * 4
    vmem = 2 * t_all * (2 * dk * 2 + 3 * dv * 2) + 2 * t_all * dv * 4 + gate_tiles + 12 * 1024 * 1024
    return pl.pallas_call(
        _mlstm_kernel,
        grid=(nb, nh),
        in_specs=stream_specs(ctx_len) + stream_specs(seq) + [pl.BlockSpec((dv, LANES), lambda b, i: (i, 0))],
        out_specs=[pl.BlockSpec((ctx_len, dv), lambda b, i: (b, i)),
                   pl.BlockSpec((seq, dv), lambda b, i: (b, i))],
        out_shape=[jax.ShapeDtypeStruct((nb * ctx_len, d), BF16),
                   jax.ShapeDtypeStruct((nb * seq, d), BF16)],
        scratch_shapes=[pltpu.VMEM((dv, ctx_len), F32), pltpu.VMEM((dv, ctx_len), F32),
                        pltpu.VMEM((dv, seq), F32), pltpu.VMEM((dv, seq), F32)],
        compiler_params=_params(2, vmem),
        name="mlstm",
    )(k_ctx, ft_ctx, ft_ctx, ft_ctx, grc, gcc, k_lat, ft_lat, ft_lat, ft_lat, grl, gcl, hg)


def _fold_lanes(x, op):
    slabs = [x[:, j:j + LANES] for j in range(0, x.shape[1], LANES)]
    return functools.reduce(op, slabs)


def _attn_kernel(q_ref, kl_ref, vl_ref, kc_ref, vc_ref, lam_ref, gs_ref, o_ref, *, lam_init):
    dh = q_ref.shape[1] // 2
    lq = lam_ref[...]
    lam = jnp.exp(jnp.sum(lq[0:1] * lq[1:2], axis=-1, keepdims=True)) \
        - jnp.exp(jnp.sum(lq[2:3] * lq[3:4], axis=-1, keepdims=True)) + lam_init

    seq = q_ref.shape[0]
    ctx_len = kc_ref.shape[0]
    tq = _tile(seq, ATTN_TQ)
    tk = _tile(seq, ATTN_TK)
    blocks = [(kl_ref, vl_ref, j * tk, tk) for j in range(seq // tk)] + [(kc_ref, vc_ref, 0, ctx_len)]

    def q_tile(t, _):
        rows = pl.ds(pl.multiple_of(t * tq, tq), tq)
        q0 = q_ref[rows, 0:dh]
        q1 = q_ref[rows, dh:2 * dh]
        m = l = acc = None
        for k_ref, v_ref, start, size in blocks:
            s = jnp.concatenate([_dot_nt(q0, k_ref[start:start + size, 0:dh]),
                                 _dot_nt(q1, k_ref[start:start + size, dh:2 * dh])], axis=0)
            mx = jnp.max(_fold_lanes(s, jnp.maximum), axis=-1, keepdims=True)
            m_new = mx if m is None else jnp.maximum(m, mx)
            p = jnp.exp2(s - m_new)
            ps = jnp.sum(_fold_lanes(p, jnp.add), axis=-1, keepdims=True)
            pv = jnp.dot(p.astype(BF16), v_ref[start:start + size, :], preferred_element_type=F32)
            if m is None:
                l, acc = ps, pv
            else:
                alpha = jnp.exp2(m - m_new)
                l = alpha * l + ps
                acc = alpha * acc + pv
            m = m_new
        a = acc * (1.0 / l)
        o = a[0:tq] - lam * a[tq:2 * tq]
        on = o * lax.rsqrt(jnp.mean(o * o, axis=-1, keepdims=True) + NORM_EPS) * gs_ref[...] * (1.0 - lam_init)
        o_ref[rows, :] = on.astype(o_ref.dtype)
        return 0

    lax.fori_loop(0, seq // tq, q_tile, 0)


def _attn(p_lat, p_ctx, lam_qk, g_sub, nb, d, lam_init):
    seq = p_lat.shape[0] // nb
    ctx_len = p_ctx.shape[0] // nb
    h = DA_HEADS
    hd = d // h
    tq = _tile(seq, ATTN_TQ)
    tk = _tile(seq, ATTN_TK)
    vmem = 2 * (4 * seq + 2 * ctx_len) * hd * 2 + 12 * tq * tk * 4 + 8 * tq * hd * 4 + 4 * 1024 * 1024
    return pl.pallas_call(
        functools.partial(_attn_kernel, lam_init=lam_init),
        grid=(nb, h),
        in_specs=[
            pl.BlockSpec((seq, hd), lambda b, i: (b, i)),
            pl.BlockSpec((seq, hd), lambda b, i: (b, h + i)),
            pl.BlockSpec((seq, hd), lambda b, i: (b, 2 * h + i)),
            pl.BlockSpec((ctx_len, hd), lambda b, i: (b, i)),
            pl.BlockSpec((ctx_len, hd), lambda b, i: (b, h + i)),
            pl.BlockSpec(lam_qk.shape, lambda b, i: (0, 0)),
            pl.BlockSpec((1, hd), lambda b, i: (0, 0)),
        ],
        out_specs=pl.BlockSpec((seq, hd), lambda b, i: (b, i)),
        out_shape=jax.ShapeDtypeStruct((nb * seq, d), BF16),
        compiler_params=_params(2, vmem),
        name="attn",
    )(p_lat, p_lat, p_lat, p_ctx, p_ctx, lam_qk, g_sub)


def _rope_tables(seq, dh):
    nf = dh // 4
    n = jnp.arange(seq)
    row = (n // GRID_W).astype(F32)
    col = (n % GRID_W).astype(F32)
    freq = ROPE_BASE ** (-jnp.arange(nf, dtype=F32) / nf)
    ar = row[:, None] * freq
    ac = col[:, None] * freq
    cos = jnp.concatenate([jnp.cos(ar), jnp.cos(ar), jnp.cos(ac), jnp.cos(ac)], axis=1)
    sin = jnp.concatenate([-jnp.sin(ar), jnp.sin(ar), -jnp.sin(ac), jnp.sin(ac)], axis=1)
    return cos, sin


def kernel(x, c, ctx, c_ctx, ada_w, ada_b, norm_g, mlp_w1, mlp_w2, ml_w_in, ml_b_gates, ml_head_g,
           ml_w_out, da_w_in, da_lambda, da_sub_g, da_w_out):
    nb, seq, d = x.shape
    ctx_len = ctx.shape[1]
    depth = ada_w.shape[0]
    assert depth == 2 and nb < COND_ROWS and d // ML_HEADS == 2 * LANES
    assert seq % LANES == 0 and ctx_len % LANES == 0 and seq % GRID_W == 0

    xl = x.reshape(nb * seq, d)
    xc = ctx.reshape(nb * ctx_len, d)

    cond = jnp.zeros((COND_ROWS, d), F32).at[:nb].set(c).at[nb].set(c_ctx)
    mods = _ada(cond, ada_w, ada_b).reshape(depth, COND_ROWS, 1, N_MOD * d)

    tm = _tile(seq, 512)
    tm_mlp = _tile(seq, 1024)
    tmc = _tile(nb * ctx_len, 512)

    def lat_row_of(tile_rows):
        return lambda i: i // (seq // tile_rows)

    lat_row = lat_row_of(tm)
    lat_row_mlp = lat_row_of(tm_mlp)

    def ctx_row(i):
        return nb

    g = norm_g.reshape(depth, 4, 1, d)

    ml_qk = ML_HEADS * (d // ML_HEADS // 2)
    n_main = 2 * ml_qk + 2 * d
    w_in = ml_w_in[0]
    q_scale = (d // ML_HEADS // 2) ** -0.5
    w_k = w_in[:, ml_qk:2 * ml_qk].astype(BF16)
    w_feat_t = jnp.concatenate([w_in[:, :ml_qk] * q_scale, w_in[:, 2 * ml_qk:n_main]], axis=1).T.astype(BF16)
    n_gate = 4 * ML_HEADS
    w_gate = jnp.pad(w_in[:, n_main:], ((0, 0), (0, LANES - n_gate))).astype(BF16)
    b_gate = jnp.pad(ml_b_gates[0], (0, LANES - n_gate)).reshape(1, LANES)
    w_out0 = ml_w_out[0].astype(BF16)
    w1_0 = mlp_w1[0].astype(BF16)
    w2_0 = mlp_w2[0].astype(BF16)

    k_lat, ft_lat, g_lat = _proj_ml(xl, mods, 0, lat_row, g[0, 0], w_k, w_feat_t, w_gate, b_gate, tm=tm, tn=1024)
    k_ctx, ft_ctx, g_ctx = _proj_ml(xc, mods, 0, ctx_row, g[0, 0], w_k, w_feat_t, w_gate, b_gate, tm=tmc, tn=1024)
    y_ctx, y_lat = _mlstm(k_ctx, ft_ctx, g_ctx, k_lat, ft_lat, g_lat, ml_head_g[0], nb, d)
    xl = _outproj(y_lat, w_out0, xl, mods, 0, lat_row, g[0, 1], tm=tm)
    xc = _outproj(y_ctx, w_out0, xc, mods, 0, ctx_row, g[0, 1], tm=tmc)
    xl = _mlp(xl, mods, 0, lat_row_mlp, g[0, 2], g[0, 3], w1_0, w2_0, tm=tm_mlp, tf=512)
    xc = _mlp(xc, mods, 0, ctx_row, g[0, 2], g[0, 3], w1_0, w2_0, tm=tmc, tf=512)

    dh = d // (2 * DA_HEADS)
    w_in = da_w_in[0]
    w_qkv = jnp.concatenate([w_in[:, :d] * (dh ** -0.5 * LOG2E), w_in[:, d:]], axis=1).astype(BF16)
    w_kv = w_in[:, d:].astype(BF16)
    w_out1 = da_w_out[0].astype(BF16)
    w1_1 = mlp_w1[1].astype(BF16)
    w2_1 = mlp_w2[1].astype(BF16)
    layer_idx = 1
    lam_init = 0.8 - 0.6 * math.exp(-0.3 * layer_idx)

    q_lat = _proj(xl, mods, 1, lat_row, g[1, 0], w_qkv, tm=tm, tn=1024,
                  rope=_rope_tables(seq, dh), rope_cols=2 * d)
    kv_ctx = _proj(xc, mods, 1, ctx_row, g[1, 0], w_kv, tm=tmc, tn=1024)
    y_lat = _attn(q_lat, kv_ctx, da_lambda[0], da_sub_g[0].reshape(1, 2 * dh), nb, d, lam_init)
    xl = _outproj(y_lat, w_out1, xl, mods, 1, lat_row, g[1, 1], tm=tm)
    xl = _mlp(xl, mods, 1, lat_row_mlp, g[1, 2], g[1, 3], w1_1, w2_1, tm=tm_mlp, tf=512)
    return xl.reshape(nb, seq, d)
```

```python
import functools
import math

import jax
import jax.numpy as jnp
from jax import lax
from jax.experimental import pallas as pl
from jax.experimental.pallas import tpu as pltpu

F32 = jnp.float32
BF16 = jnp.bfloat16

NORM_EPS = 1e-6
GRID_W = 64
ROPE_BASE = 10000.0
LOG2E = math.log2(math.e)
ML_HEADS = 8
ML_BLOCK = 256
ML_HB = 2
ML_AUG = 16
DA_HEADS = 8
ATTN_TQ = 256
ATTN_TK = 512
N_MOD = 6
COND_ROWS = 32
LANES = 128
MXU_COLS = 256
GATE_ROWS = 8

VMEM_CAP_BYTES = 56 * 1024 * 1024


def _vmem_limit(nbytes):
    return int(min(VMEM_CAP_BYTES, max(16 * 1024 * 1024, nbytes)))


def _params(n_grid, vmem_bytes):
    return pltpu.CompilerParams(
        dimension_semantics=("arbitrary",) * n_grid,
        vmem_limit_bytes=_vmem_limit(vmem_bytes))


def _tile(n, want):
    if n <= want:
        return n
    t = want
    while t >= 8:
        if n % t == 0 and t % 8 == 0:
            return t
        t -= 8
    return n


def _dot_nt(a, b):
    return lax.dot_general(a, b, (((1,), (1,)), ((), ())), preferred_element_type=F32)


def _ada_kernel(c_ref, w_ref, b_ref, o_ref):
    cf = c_ref[...]
    s = (cf * jax.nn.sigmoid(cf)).astype(BF16)
    o_ref[...] = jnp.dot(s, w_ref[...].astype(BF16), preferred_element_type=F32) + b_ref[...]


def _ada(cond, ada_w, ada_b):
    depth, d, n = ada_w.shape
    tn = _tile(n, 1024)
    return pl.pallas_call(
        _ada_kernel,
        grid=(depth, n // tn),
        in_specs=[
            pl.BlockSpec((COND_ROWS, d), lambda l, j: (0, 0)),
            pl.BlockSpec((None, d, tn), lambda l, j: (l, 0, j)),
            pl.BlockSpec((None, 1, tn), lambda l, j: (l, 0, j)),
        ],
        out_specs=pl.BlockSpec((None, COND_ROWS, tn), lambda l, j: (l, 0, j)),
        out_shape=jax.ShapeDtypeStruct((depth, COND_ROWS, n), F32),
        compiler_params=_params(2, 2 * d * tn * 4 + 4 * COND_ROWS * (d + tn) * 4 + d * tn * 2),
        name="ada",
    )(cond, ada_w, ada_b.reshape(depth, 1, n))


def _mod_spec(layer, piece, row_fn, d, n_grid):
    if n_grid == 1:
        return pl.BlockSpec((None, None, 1, d), lambda i: (layer, row_fn(i), 0, piece))
    return pl.BlockSpec((None, None, 1, d), lambda i, j: (layer, row_fn(i), 0, piece))


def _modulated(x_ref, g_ref, sh_ref, sc_ref):
    xf = x_ref[...]
    ms = jnp.mean(xf * xf, axis=-1, keepdims=True)
    y = xf * lax.rsqrt(ms + NORM_EPS) * g_ref[...]
    return (y * (1.0 + sc_ref[...]) + sh_ref[...]).astype(BF16)


def _swap_halves_32(a, first_half):
    fwd = pltpu.roll(a, 3 * LANES // 4, axis=1)
    bwd = pltpu.roll(a, LANES // 4, axis=1)
    return jnp.where(first_half, fwd, bwd)


def _proj_kernel(*refs, n_rope_tiles):
    x_ref, sh_ref, sc_ref, g_ref, w_ref = refs[:5]
    if n_rope_tiles:
        cos_ref, sin_ref = refs[5:7]
    o_ref, h_scr = refs[-2:]
    j = pl.program_id(1)

    @pl.when(j == 0)
    def _():
        h_scr[...] = _modulated(x_ref, g_ref, sh_ref, sc_ref)

    def product():
        return jnp.dot(h_scr[...], w_ref[...], preferred_element_type=F32)

    if n_rope_tiles:
        @pl.when(j < n_rope_tiles)
        def _():
            cos = cos_ref[...]
            sin = sin_ref[...]
            lane = lax.broadcasted_iota(jnp.int32, cos.shape, 1)
            first_half = (lane % (LANES // 2)) < (LANES // 4)
            for c in range(w_ref.shape[1] // MXU_COLS):
                acc = jnp.dot(h_scr[...], w_ref[:, c * MXU_COLS:(c + 1) * MXU_COLS],
                              preferred_element_type=F32)
                for half in range(MXU_COLS // LANES):
                    a = acc[:, half * LANES:(half + 1) * LANES]
                    r = a * cos + _swap_halves_32(a, first_half) * sin
                    lo = c * MXU_COLS + half * LANES
                    o_ref[:, lo:lo + LANES] = r.astype(o_ref.dtype)

        @pl.when(j >= n_rope_tiles)
        def _():
            o_ref[...] = product().astype(o_ref.dtype)
    else:
        o_ref[...] = product().astype(o_ref.dtype)


def _proj(x, mods, layer, row_fn, g, w, *, tm, tn, rope=None, rope_cols=0):
    m, d = x.shape
    n = w.shape[1]
    tm = _tile(m, tm)
    tn = _tile(n, tn)
    in_specs = [
        pl.BlockSpec((tm, d), lambda i, j: (i, 0)),
        _mod_spec(layer, 0, row_fn, d, 2),
        _mod_spec(layer, 1, row_fn, d, 2),
        pl.BlockSpec((1, d), lambda i, j: (0, 0)),
        pl.BlockSpec((d, tn), lambda i, j: (0, j)),
    ]
    args = [x, mods, mods, g, w]
    n_rope_tiles = 0
    if rope is not None:
        cos, sin = rope
        assert rope_cols % tn == 0 and cos.shape[0] % tm == 0
        n_rope_tiles = rope_cols // tn
        n_pos_tiles = cos.shape[0] // tm
        in_specs += [pl.BlockSpec((tm, LANES), lambda i, j: (i % n_pos_tiles, 0)),
                     pl.BlockSpec((tm, LANES), lambda i, j: (i % n_pos_tiles, 0))]
        args += [cos, sin]
    vmem = 2 * tm * d * 4 + tm * d * 2 + 2 * d * tn * 2 + 2 * tm * tn * 2 + 3 * tm * tn * 4 \
        + 4 * tm * LANES * 4 + 3 * tm * d * 4
    return pl.pallas_call(
        functools.partial(_proj_kernel, n_rope_tiles=n_rope_tiles),
        grid=(m // tm, n // tn),
        in_specs=in_specs,
        out_specs=pl.BlockSpec((tm, tn), lambda i, j: (i, j)),
        out_shape=jax.ShapeDtypeStruct((m, n), BF16),
        scratch_shapes=[pltpu.VMEM((tm, d), BF16)],
        compiler_params=_params(2, vmem),
        name="proj",
    )(*args)


def _proj_ml_kernel(x_ref, sh_ref, sc_ref, g_ref, wk_ref, wt_ref, wg_ref, bg_ref,
                    k_ref, ft_ref, gate_ref, h_scr):
    j = pl.program_id(1)

    @pl.when(j == 0)
    def _():
        h_scr[...] = _modulated(x_ref, g_ref, sh_ref, sc_ref)
        h = h_scr[...]
        gate_ref[...] = jnp.dot(h, wg_ref[...], preferred_element_type=F32) + bg_ref[...]
        k_ref[...] = jnp.dot(h, wk_ref[...], preferred_element_type=F32).astype(k_ref.dtype)

    @pl.when(j > 0)
    def _():
        ft_ref[...] = _dot_nt(wt_ref[...], h_scr[...]).astype(ft_ref.dtype)


def _proj_ml(x, mods, layer, row_fn, g, wk, wt, wg, bg, *, tm, tn):
    m, d = x.shape
    nk = wk.shape[1]
    nf = wt.shape[0]
    tm = _tile(m, tm)
    tn = _tile(nf, tn)

    def feat_tile(j):
        return jnp.maximum(j - 1, 0)

    vmem = 2 * tm * d * 4 + tm * d * 2 + 2 * d * nk * 2 + 2 * tn * d * 2 + 2 * tm * nk * 2 + 2 * tn * tm * 2 \
        + 2 * tm * max(nk, tn) * 4 + 4 * tm * LANES * 4 + 2 * d * LANES * 2 + 3 * tm * d * 4
    return pl.pallas_call(
        _proj_ml_kernel,
        grid=(m // tm, 1 + nf // tn),
        in_specs=[
            pl.BlockSpec((tm, d), lambda i, j: (i, 0)),
            _mod_spec(layer, 0, row_fn, d, 2),
            _mod_spec(layer, 1, row_fn, d, 2),
            pl.BlockSpec((1, d), lambda i, j: (0, 0)),
            pl.BlockSpec((d, nk), lambda i, j: (0, 0)),
            pl.BlockSpec((tn, d), lambda i, j: (feat_tile(j), 0)),
            pl.BlockSpec((d, LANES), lambda i, j: (0, 0)),
            pl.BlockSpec((1, LANES), lambda i, j: (0, 0)),
        ],
        out_specs=[pl.BlockSpec((tm, nk), lambda i, j: (i, 0)),
                   pl.BlockSpec((tn, tm), lambda i, j: (feat_tile(j), i)),
                   pl.BlockSpec((tm, LANES), lambda i, j: (i, 0))],
        out_shape=[jax.ShapeDtypeStruct((m, nk), BF16),
                   jax.ShapeDtypeStruct((nf, m), BF16),
                   jax.ShapeDtypeStruct((m, LANES), F32)],
        scratch_shapes=[pltpu.VMEM((tm, d), BF16)],
        compiler_params=_params(2, vmem),
        name="proj_ml",
    )(x, mods, mods, g, wk, wt, wg, bg)


def _outproj_kernel(y_ref, w_ref, x_ref, gate_ref, g_ref, o_ref):
    t = jnp.dot(y_ref[...], w_ref[...], preferred_element_type=F32)
    ms = jnp.mean(t * t, axis=-1, keepdims=True)
    o_ref[...] = x_ref[...] + gate_ref[...] * (t * lax.rsqrt(ms + NORM_EPS) * g_ref[...])


def _outproj(y, w, x, mods, layer, row_fn, g, *, tm):
    m, d = x.shape
    kdim = y.shape[1]
    tm = _tile(m, tm)
    vmem = 2 * tm * kdim * 2 + 2 * kdim * d * 2 + 4 * tm * d * 4 + 3 * tm * d * 4
    return pl.pallas_call(
        _outproj_kernel,
        grid=(m // tm,),
        in_specs=[
            pl.BlockSpec((tm, kdim), lambda i: (i, 0)),
            pl.BlockSpec((kdim, d), lambda i: (0, 0)),
            pl.BlockSpec((tm, d), lambda i: (i, 0)),
            _mod_spec(layer, 2, row_fn, d, 1),
            pl.BlockSpec((1, d), lambda i: (0, 0)),
        ],
        out_specs=pl.BlockSpec((tm, d), lambda i: (i, 0)),
        out_shape=jax.ShapeDtypeStruct((m, d), F32),
        compiler_params=_params(1, vmem),
        name="outproj",
    )(y, w, x, mods, g)


def _mlp_kernel(x_ref, sh_ref, sc_ref, gate_ref, g_in_ref, g_out_ref, w1_ref, w2_ref, o_ref, h_scr):
    j = pl.program_id(1)

    @pl.when(j == 0)
    def _():
        h_scr[...] = _modulated(x_ref, g_in_ref, sh_ref, sc_ref)
        o_ref[...] = jnp.zeros_like(o_ref)

    a = jnp.dot(h_scr[...], w1_ref[...], preferred_element_type=F32)
    a = jnp.maximum(a, 0.0)
    a = (a * a).astype(BF16)
    o_ref[...] += jnp.dot(a, w2_ref[...], preferred_element_type=F32)

    @pl.when(j == pl.num_programs(1) - 1)
    def _():
        f = o_ref[...]
        ms = jnp.mean(f * f, axis=-1, keepdims=True)
        o_ref[...] = x_ref[...] + gate_ref[...] * (f * lax.rsqrt(ms + NORM_EPS) * g_out_ref[...])


def _mlp(x, mods, layer, row_fn, g_in, g_out, w1, w2, *, tm, tf):
    m, d = x.shape
    dff = w1.shape[1]
    tm = _tile(m, tm)
    tf = _tile(dff, tf)
    vmem = 4 * tm * d * 4 + tm * d * 2 + 4 * d * tf * 2 + 3 * tm * tf * 4 + 4 * 1024 * 1024
    return pl.pallas_call(
        _mlp_kernel,
        grid=(m // tm, dff // tf),
        in_specs=[
            pl.BlockSpec((tm, d), lambda i, j: (i, 0)),
            _mod_spec(layer, 3, row_fn, d, 2),
            _mod_spec(layer, 4, row_fn, d, 2),
            _mod_spec(layer, 5, row_fn, d, 2),
            pl.BlockSpec((1, d), lambda i, j: (0, 0)),
            pl.BlockSpec((1, d), lambda i, j: (0, 0)),
            pl.BlockSpec((d, tf), lambda i, j: (0, j)),
            pl.BlockSpec((tf, d), lambda i, j: (j, 0)),
        ],
        out_specs=pl.BlockSpec((tm, d), lambda i, j: (i, 0)),
        out_shape=jax.ShapeDtypeStruct((m, d), F32),
        scratch_shapes=[pltpu.VMEM((tm, d), BF16)],
        compiler_params=_params(2, vmem),
        name="mlp",
    )(x, mods, mods, mods, g_in, g_out, w1, w2)


def _split3(x):
    x1 = x.astype(BF16)
    r1 = x - x1.astype(F32)
    x2 = r1.astype(BF16)
    x3 = (r1 - x2.astype(F32)).astype(BF16)
    return x1, x2, x3


def _log_sigmoid(x):
    return jnp.minimum(x, 0.0) - jnp.log(1.0 + jnp.exp(-jnp.abs(x)))


def _running_max_lanes(x, reverse):
    lane = lax.broadcasted_iota(jnp.int32, (x.shape[0], LANES), 1)
    slabs = [x[:, j:j + LANES] for j in range(0, x.shape[1], LANES)]
    order = range(len(slabs) - 1, -1, -1) if reverse else range(len(slabs))
    carry = None
    for j in order:
        y = slabs[j]
        k = 1
        while k < LANES:
            if reverse:
                y = jnp.maximum(y, jnp.where(lane < LANES - k, pltpu.roll(y, LANES - k, axis=1), -jnp.inf))
            else:
                y = jnp.maximum(y, jnp.where(lane >= k, pltpu.roll(y, k, axis=1), -jnp.inf))
            k *= 2
        if carry is not None:
            y = jnp.maximum(y, carry)
        carry = jnp.max(y, axis=-1, keepdims=True)
        slabs[j] = y
    return jnp.concatenate(slabs, axis=1)


def _gate_scan_kernel(x_ref, o_ref):
    L = x_ref.shape[-1]
    ui = lax.broadcasted_iota(jnp.int32, (L, L), 0)
    si = lax.broadcasted_iota(jnp.int32, (L, L), 1)

    def cumulative(z, tri):
        return sum(jnp.dot(p, tri.astype(BF16), preferred_element_type=F32) for p in _split3(z))

    b_f = cumulative(_log_sigmoid(x_ref[1]), ui <= si) * LOG2E
    b_b = cumulative(_log_sigmoid(x_ref[3]), ui >= si) * LOG2E
    c_f = x_ref[0] * LOG2E - b_f
    c_b = x_ref[2] * LOG2E - b_b
    o_ref[0] = b_f
    o_ref[1] = c_f
    o_ref[2] = _running_max_lanes(c_f, False)
    o_ref[3] = b_b
    o_ref[4] = c_b
    o_ref[5] = _running_max_lanes(c_b, True)


def _gate_scan(gates, nb, t, L):
    nc = t // L
    g = gates[:, :4 * ML_HEADS].reshape(nb, nc, L, 4, ML_HEADS)
    g = g.transpose(3, 0, 4, 1, 2).reshape(4, nb * ML_HEADS * nc, L)
    nr = g.shape[1]
    tr = _tile(nr, 256)
    n_out = 6
    s = pl.pallas_call(
        _gate_scan_kernel,
        grid=(nr // tr,),
        in_specs=[pl.BlockSpec((4, tr, L), lambda i: (0, i, 0))],
        out_specs=pl.BlockSpec((n_out, tr, L), lambda i: (0, i, 0)),
        out_shape=jax.ShapeDtypeStruct((n_out, nr, L), F32),
        compiler_params=_params(1, 48 * tr * max(L, LANES) * 4),
        name="gate_scan",
    )(g)
    s = s.reshape(n_out, nb, ML_HEADS, nc, L)
    rows = jnp.pad(s.transpose(1, 2, 3, 0, 4), ((0, 0), (0, 0), (0, 0), (0, GATE_ROWS - n_out), (0, 0)))
    return rows


def _mlstm_kernel(kc_ref, qtc_ref, vtc_ref, ogtc_ref, grc_ref,
                  kl_ref, qtl_ref, vtl_ref, ogtl_ref, grl_ref, hg_ref,
                  oc_ref, ol_ref, hf_c, hb_c, hf_l, hb_l):
    L = grc_ref.shape[-1]
    dk = kc_ref.shape[1] // ML_HB
    dv = vtc_ref.shape[0] // ML_HB
    ncc = kc_ref.shape[0] // L
    ncl = kl_ref.shape[0] // L

    si = lax.broadcasted_iota(jnp.int32, (L, L), 0)
    ti = lax.broadcasted_iota(jnp.int32, (L, L), 1)
    ones_rows = (lax.broadcasted_iota(jnp.int32, (ML_AUG, L), 0) == 0).astype(BF16)

    gate_tiles = {}
    for gr_ref, nc in ((grc_ref, ncc), (grl_ref, ncl)):
        for hh in range(ML_HB):
            for c in range(nc):
                tile = gr_ref[hh, c]
                gate_tiles[(id(gr_ref), hh, c)] = (tile, tile.T)

    def chunk(refs, c, hh, backward, state):
        k_ref, qt_ref, vt_ref, gr_ref, hs_ref = refs
        caug, m = state
        tok = slice(c * L, (c + 1) * L)
        kb = k_ref[tok, hh * dk:(hh + 1) * dk]
        qt = qt_ref[hh * dk:(hh + 1) * dk, tok]
        vt_aug = jnp.concatenate([vt_ref[hh * dv:(hh + 1) * dv, tok], ones_rows], axis=0)
        rows, cols = gate_tiles[(id(gr_ref), hh, c)]
        r = 3 if backward else 0
        b_row = rows[r:r + 1, :]
        c_row = rows[r + 1:r + 2, :]
        c_max = rows[r + 2:r + 3, :]
        c_col = cols[:, r + 1:r + 2]
        b_end = b_row[:, 0:1] if backward else b_row[:, L - 1:L]

        inter = b_row + m
        m_row = jnp.maximum(inter, b_row + c_max)
        mask = (si >= ti) if backward else (si <= ti)
        decay = jnp.where(mask, jnp.exp2(c_col + (b_row - m_row)), 0.0)
        w_row = jnp.exp2(inter - m_row)

        p_t = (jnp.dot(kb, qt, preferred_element_type=F32) * decay).astype(BF16)
        lhs = jnp.concatenate([caug.astype(BF16), vt_aug], axis=1)
        rhs = jnp.concatenate([qt * w_row.astype(BF16), p_t], axis=0)
        out = jnp.dot(lhs, rhs, preferred_element_type=F32)
        den = out[dv:dv + 1, :]
        hs_ref[hh * dv:(hh + 1) * dv, tok] = out[0:dv, :] * (1.0 / jnp.maximum(jnp.abs(den), jnp.exp2(-m_row)))

        g_row = b_end + c_row
        m_new = jnp.maximum(b_end + m, jnp.max(g_row, axis=-1, keepdims=True))
        a = jnp.exp2(b_end + m - m_new)
        ws = jnp.exp2(g_row - m_new)
        caug_new = a * caug + jnp.dot(vt_aug * ws.astype(BF16), kb, preferred_element_type=F32)
        return caug_new, m_new

    def scan_segment(nc, k_ref, qt_ref, vt_ref, gr_ref, hf_ref, hb_ref, states):
        fwd_refs = (k_ref, qt_ref, vt_ref, gr_ref, hf_ref)
        bwd_refs = (k_ref, qt_ref, vt_ref, gr_ref, hb_ref)
        for step in range(nc):
            nxt = []
            for hh in range(ML_HB):
                nxt.append(chunk(fwd_refs, step, hh, False, states[2 * hh]))
                nxt.append(chunk(bwd_refs, nc - 1 - step, hh, True, states[2 * hh + 1]))
            states = nxt
        return states

    zero = (jnp.zeros((dv + ML_AUG, dk), F32), jnp.zeros((1, 1), F32))
    st = scan_segment(ncc, kc_ref, qtc_ref, vtc_ref, grc_ref, hf_c, hb_c, [zero] * (2 * ML_HB))
    scan_segment(ncl, kl_ref, qtl_ref, vtl_ref, grl_ref, hf_l, hb_l, st)

    def finish_segment(nc, hf_ref, hb_ref, ogt_ref, o_ref):
        for c in range(nc):
            tok = slice(c * L, (c + 1) * L)
            for hh in range(ML_HB):
                feat = slice(hh * dv, (hh + 1) * dv)
                hs = hf_ref[feat, tok] + hb_ref[feat, tok]
                scale = lax.rsqrt(jnp.mean(hs * hs, axis=0, keepdims=True) + NORM_EPS)
                head_g = jnp.concatenate([hg_ref[feat, :]] * (L // LANES), axis=1)
                y_t = hs * scale * head_g * jax.nn.sigmoid(ogt_ref[feat, tok].astype(F32))
                o_ref[tok, feat] = y_t.T.astype(o_ref.dtype)

    finish_segment(ncc, hf_c, hb_c, ogtc_ref, oc_ref)
    finish_segment(ncl, hf_l, hb_l, ogtl_ref, ol_ref)


def _mlstm(k_ctx, ft_ctx, g_ctx, k_lat, ft_lat, g_lat, head_g, nb, d):
    ctx_len = k_ctx.shape[0] // nb
    seq = k_lat.shape[0] // nb
    dv = ML_HB * (d // ML_HEADS)
    dk = dv // 2
    nh = ML_HEADS // ML_HB
    q_rows = ML_HEADS * (d // ML_HEADS // 2)
    v_blk = q_rows // dv
    og_blk = (q_rows + d) // dv
    L = math.gcd(math.gcd(ctx_len, seq), ML_BLOCK)
    assert L % LANES == 0
    grc = _gate_scan(g_ctx, nb, ctx_len, L)
    grl = _gate_scan(g_lat, nb, seq, L)
    hg = jnp.broadcast_to(head_g.reshape(d, 1), (d, LANES))

    def stream_specs(t):
        return [
            pl.BlockSpec((t, dk), lambda b, i: (b, i)),
            pl.BlockSpec((dk, t), lambda b, i: (i, b)),
            pl.BlockSpec((dv, t), lambda b, i: (v_blk + i, b)),
            pl.BlockSpec((dv, t), lambda b, i: (og_blk + i, b)),
            pl.BlockSpec((None, ML_HB, t // L, GATE_ROWS, L), lambda b, i: (b, i, 0, 0, 0)),
        ]

    t_all = ctx_len + seq
    gate_tiles = 2 * ML_HB * (t_all // L) * GATE_ROWS * L * 4
    vmem = 2 * t_all * (2 * dk * 2 + 3 * dv * 2) + 2 * t_all * dv * 4 + gate_tiles + 12 * 1024 * 1024
    return pl.pallas_call(
        _mlstm_kernel,
        grid=(nb, nh),
        in_specs=stream_specs(ctx_len) + stream_specs(seq) + [pl.BlockSpec((dv, LANES), lambda b, i: (i, 0))],
        out_specs=[pl.BlockSpec((ctx_len, dv), lambda b, i: (b, i)),
                   pl.BlockSpec((seq, dv), lambda b, i: (b, i))],
        out_shape=[jax.ShapeDtypeStruct((nb * ctx_len, d), BF16),
                   jax.ShapeDtypeStruct((nb * seq, d), BF16)],
        scratch_shapes=[pltpu.VMEM((dv, ctx_len), F32), pltpu.VMEM((dv, ctx_len), F32),
                        pltpu.VMEM((dv, seq), F32), pltpu.VMEM((dv, seq), F32)],
        compiler_params=_params(2, vmem),
        name="mlstm",
    )(k_ctx, ft_ctx, ft_ctx, ft_ctx, grc, k_lat, ft_lat, ft_lat, ft_lat, grl, hg)


def _fold_lanes(x, op):
    slabs = [x[:, j:j + LANES] for j in range(0, x.shape[1], LANES)]
    return functools.reduce(op, slabs)


def _attn_kernel(q_ref, kl_ref, vl_ref, kc_ref, vc_ref, lam_ref, gs_ref, o_ref, *, lam_init):
    dh = q_ref.shape[1] // 2
    lq = lam_ref[...]
    lam = jnp.exp(jnp.sum(lq[0:1] * lq[1:2], axis=-1, keepdims=True)) \
        - jnp.exp(jnp.sum(lq[2:3] * lq[3:4], axis=-1, keepdims=True)) + lam_init

    seq = q_ref.shape[0]
    ctx_len = kc_ref.shape[0]
    tq = _tile(seq, ATTN_TQ)
    tk = _tile(seq, ATTN_TK)
    blocks = [(kl_ref, vl_ref, j * tk, tk) for j in range(seq // tk)] + [(kc_ref, vc_ref, 0, ctx_len)]

    def q_tile(t, _):
        rows = pl.ds(pl.multiple_of(t * tq, tq), tq)
        q0 = q_ref[rows, 0:dh]
        q1 = q_ref[rows, dh:2 * dh]
        m = l = acc = None
        for k_ref, v_ref, start, size in blocks:
            s = jnp.concatenate([_dot_nt(q0, k_ref[start:start + size, 0:dh]),
                                 _dot_nt(q1, k_ref[start:start + size, dh:2 * dh])], axis=0)
            mx = jnp.max(_fold_lanes(s, jnp.maximum), axis=-1, keepdims=True)
            m_new = mx if m is None else jnp.maximum(m, mx)
            p = jnp.exp2(s - m_new)
            ps = jnp.sum(_fold_lanes(p, jnp.add), axis=-1, keepdims=True)
            pv = jnp.dot(p.astype(BF16), v_ref[start:start + size, :], preferred_element_type=F32)
            if m is None:
                l, acc = ps, pv
            else:
                alpha = jnp.exp2(m - m_new)
                l = alpha * l + ps
                acc = alpha * acc + pv
            m = m_new
        a = acc * (1.0 / l)
        o = a[0:tq] - lam * a[tq:2 * tq]
        on = o * lax.rsqrt(jnp.mean(o * o, axis=-1, keepdims=True) + NORM_EPS) * gs_ref[...] * (1.0 - lam_init)
        o_ref[rows, :] = on.astype(o_ref.dtype)
        return 0

    lax.fori_loop(0, seq // tq, q_tile, 0)


def _attn(p_lat, p_ctx, lam_qk, g_sub, nb, d, lam_init):
    seq = p_lat.shape[0] // nb
    ctx_len = p_ctx.shape[0] // nb
    h = DA_HEADS
    hd = d // h
    tq = _tile(seq, ATTN_TQ)
    tk = _tile(seq, ATTN_TK)
    vmem = 2 * (4 * seq + 2 * ctx_len) * hd * 2 + 12 * tq * tk * 4 + 8 * tq * hd * 4 + 4 * 1024 * 1024
    return pl.pallas_call(
        functools.partial(_attn_kernel, lam_init=lam_init),
        grid=(nb, h),
        in_specs=[
            pl.BlockSpec((seq, hd), lambda b, i: (b, i)),
            pl.BlockSpec((seq, hd), lambda b, i: (b, h + i)),
            pl.BlockSpec((seq, hd), lambda b, i: (b, 2 * h + i)),
            pl.BlockSpec((ctx_len, hd), lambda b, i: (b, i)),
            pl.BlockSpec((ctx_len, hd), lambda b, i: (b, h + i)),
            pl.BlockSpec(lam_qk.shape, lambda b, i: (0, 0)),
            pl.BlockSpec((1, hd), lambda b, i: (0, 0)),
        ],
        out_specs=pl.BlockSpec((seq, hd), lambda b, i: (b, i)),
        out_shape=jax.ShapeDtypeStruct((nb * seq, d), BF16),
        compiler_params=_params(2, vmem),
        name="attn",
    )(p_lat, p_lat, p_lat, p_ctx, p_ctx, lam_qk, g_sub)


def _rope_tables(seq, dh):
    nf = dh // 4
    n = jnp.arange(seq)
    row = (n // GRID_W).astype(F32)
    col = (n % GRID_W).astype(F32)
    freq = ROPE_BASE ** (-jnp.arange(nf, dtype=F32) / nf)
    ar = row[:, None] * freq
    ac = col[:, None] * freq
    cos = jnp.concatenate([jnp.cos(ar), jnp.cos(ar), jnp.cos(ac), jnp.cos(ac)], axis=1)
    sin = jnp.concatenate([-jnp.sin(ar), jnp.sin(ar), -jnp.sin(ac), jnp.sin(ac)], axis=1)
    return cos, sin


def kernel(x, c, ctx, c_ctx, ada_w, ada_b, norm_g, mlp_w1, mlp_w2, ml_w_in, ml_b_gates, ml_head_g,
           ml_w_out, da_w_in, da_lambda, da_sub_g, da_w_out):
    nb, seq, d = x.shape
    ctx_len = ctx.shape[1]
    depth = ada_w.shape[0]
    assert depth == 2 and nb < COND_ROWS and d // ML_HEADS == 2 * LANES
    assert seq % LANES == 0 and ctx_len % LANES == 0 and seq % GRID_W == 0

    xl = x.reshape(nb * seq, d)
    xc = ctx.reshape(nb * ctx_len, d)

    cond = jnp.zeros((COND_ROWS, d), F32).at[:nb].set(c).at[nb].set(c_ctx)
    mods = _ada(cond, ada_w, ada_b).reshape(depth, COND_ROWS, 1, N_MOD * d)

    tm = _tile(seq, 512)
    tm_mlp = _tile(seq, 1024)
    tmc = _tile(nb * ctx_len, 512)

    def lat_row_of(tile_rows):
        return lambda i: i // (seq // tile_rows)

    lat_row = lat_row_of(tm)
    lat_row_mlp = lat_row_of(tm_mlp)

    def ctx_row(i):
        return nb

    g = norm_g.reshape(depth, 4, 1, d)

    ml_qk = ML_HEADS * (d // ML_HEADS // 2)
    n_main = 2 * ml_qk + 2 * d
    w_in = ml_w_in[0]
    q_scale = (d // ML_HEADS // 2) ** -0.5
    w_k = w_in[:, ml_qk:2 * ml_qk].astype(BF16)
    w_feat_t = jnp.concatenate([w_in[:, :ml_qk] * q_scale, w_in[:, 2 * ml_qk:n_main]], axis=1).T.astype(BF16)
    n_gate = 4 * ML_HEADS
    w_gate = jnp.pad(w_in[:, n_main:], ((0, 0), (0, LANES - n_gate))).astype(BF16)
    b_gate = jnp.pad(ml_b_gates[0], (0, LANES - n_gate)).reshape(1, LANES)
    w_out0 = ml_w_out[0].astype(BF16)
    w1_0 = mlp_w1[0].astype(BF16)
    w2_0 = mlp_w2[0].astype(BF16)

    k_lat, ft_lat, g_lat = _proj_ml(xl, mods, 0, lat_row, g[0, 0], w_k, w_feat_t, w_gate, b_gate, tm=tm, tn=1024)
    k_ctx, ft_ctx, g_ctx = _proj_ml(xc, mods, 0, ctx_row, g[0, 0], w_k, w_feat_t, w_gate, b_gate, tm=tmc, tn=1024)
    y_ctx, y_lat = _mlstm(k_ctx, ft_ctx, g_ctx, k_lat, ft_lat, g_lat, ml_head_g[0], nb, d)
    xl = _outproj(y_lat, w_out0, xl, mods, 0, lat_row, g[0, 1], tm=tm)
    xc = _outproj(y_ctx, w_out0, xc, mods, 0, ctx_row, g[0, 1], tm=tmc)
    xl = _mlp(xl, mods, 0, lat_row_mlp, g[0, 2], g[0, 3], w1_0, w2_0, tm=tm_mlp, tf=512)
    xc = _mlp(xc, mods, 0, ctx_row, g[0, 2], g[0, 3], w1_0, w2_0, tm=tmc, tf=512)

    dh = d // (2 * DA_HEADS)
    w_in = da_w_in[0]
    w_qkv = jnp.concatenate([w_in[:, :d] * (dh ** -0.5 * LOG2E), w_in[:, d:]], axis=1).astype(BF16)
    w_kv = w_in[:, d:].astype(BF16)
    w_out1 = da_w_out[0].astype(BF16)
    w1_1 = mlp_w1[1].astype(BF16)
    w2_1 = mlp_w2[1].astype(BF16)
    layer_idx = 1
    lam_init = 0.8 - 0.6 * math.exp(-0.3 * layer_idx)

    q_lat = _proj(xl, mods, 1, lat_row, g[1, 0], w_qkv, tm=tm, tn=1024,
                  rope=_rope_tables(seq, dh), rope_cols=2 * d)
    kv_ctx = _proj(xc, mods, 1, ctx_row, g[1, 0], w_kv, tm=tmc, tn=1024)
    y_lat = _attn(q_lat, kv_ctx, da_lambda[0], da_sub_g[0].reshape(1, 2 * dh), nb, d, lam_init)
    xl = _outproj(y_lat, w_out1, xl, mods, 1, lat_row, g[1, 1], tm=tm)
    xl = _mlp(xl, mods, 1, lat_row_mlp, g[1, 2], g[1, 3], w1_1, w2_1, tm=tm_mlp, tf=512)
    return xl.reshape(nb, seq, d)
```

```python
import functools
import math

import jax
import jax.numpy as jnp
from jax import lax
from jax.experimental import pallas as pl
from jax.experimental.pallas import tpu as pltpu

F32 = jnp.float32
BF16 = jnp.bfloat16

NORM_EPS = 1e-6
GRID_W = 64
ROPE_BASE = 10000.0
LOG2E = math.log2(math.e)
ML_HEADS = 8
ML_BLOCK = 256
ML_HB = 2
ML_AUG = 16
DA_HEADS = 8
ATTN_TQ = 256
ATTN_TK = 512
N_MOD = 6
COND_ROWS = 32
LANES = 128
MXU_COLS = 256
GATE_ROWS = 8

VMEM_CAP_BYTES = 56 * 1024 * 1024


def _vmem_limit(nbytes):
    return int(min(VMEM_CAP_BYTES, max(16 * 1024 * 1024, nbytes)))


def _params(n_grid, vmem_bytes):
    return pltpu.CompilerParams(
        dimension_semantics=("arbitrary",) * n_grid,
        vmem_limit_bytes=_vmem_limit(vmem_bytes))


def _tile(n, want):
    if n <= want:
        return n
    t = want
    while t >= 8:
        if n % t == 0 and t % 8 == 0:
            return t
        t -= 8
    return n


def _dot_nt(a, b):
    return lax.dot_general(a, b, (((1,), (1,)), ((), ())), preferred_element_type=F32)


def _ada_kernel(c_ref, w_ref, b_ref, o_ref):
    cf = c_ref[...]
    s = (cf * jax.nn.sigmoid(cf)).astype(BF16)
    o_ref[...] = jnp.dot(s, w_ref[...].astype(BF16), preferred_element_type=F32) + b_ref[...]


def _ada(cond, ada_w, ada_b):
    depth, d, n = ada_w.shape
    tn = _tile(n, 1024)
    return pl.pallas_call(
        _ada_kernel,
        grid=(depth, n // tn),
        in_specs=[
            pl.BlockSpec((COND_ROWS, d), lambda l, j: (0, 0)),
            pl.BlockSpec((None, d, tn), lambda l, j: (l, 0, j)),
            pl.BlockSpec((None, 1, tn), lambda l, j: (l, 0, j)),
        ],
        out_specs=pl.BlockSpec((None, COND_ROWS, tn), lambda l, j: (l, 0, j)),
        out_shape=jax.ShapeDtypeStruct((depth, COND_ROWS, n), F32),
        compiler_params=_params(2, 2 * d * tn * 4 + 4 * COND_ROWS * (d + tn) * 4 + d * tn * 2),
        name="ada",
    )(cond, ada_w, ada_b.reshape(depth, 1, n))


def _mod_spec(layer, piece, row_fn, d, n_grid):
    if n_grid == 1:
        return pl.BlockSpec((None, None, 1, d), lambda i: (layer, row_fn(i), 0, piece))
    return pl.BlockSpec((None, None, 1, d), lambda i, j: (layer, row_fn(i), 0, piece))


def _modulated(x_ref, g_ref, sh_ref, sc_ref):
    xf = x_ref[...]
    ms = jnp.mean(xf * xf, axis=-1, keepdims=True)
    gain = g_ref[...] * (1.0 + sc_ref[...])
    return (xf * lax.rsqrt(ms + NORM_EPS) * gain + sh_ref[...]).astype(BF16)


def _swap_halves_32(a, first_half):
    fwd = pltpu.roll(a, 3 * LANES // 4, axis=1)
    bwd = pltpu.roll(a, LANES // 4, axis=1)
    return jnp.where(first_half, fwd, bwd)


def _swap_quarter_rows(a):
    q = LANES // 4
    return jnp.concatenate([a[q:2 * q], a[0:q], a[3 * q:4 * q], a[2 * q:3 * q]], axis=0)


def _proj_kernel(*refs, with_gates, with_rope, n_rope_feat_tiles):
    x_ref, sh_ref, sc_ref, g_ref, wk_ref, wt_ref = refs[:6]
    n = 6
    if with_gates:
        wg_ref, bg_ref = refs[n:n + 2]
        n += 2
    if with_rope:
        cos_ref, sin_ref, cost_ref, sint_ref = refs[n:n + 4]
        n += 4
    k_ref, ft_ref = refs[n:n + 2]
    n += 2
    if with_gates:
        gate_ref = refs[n]
    h_scr = refs[-1]
    j = pl.program_id(1)

    @pl.when(j == 0)
    def _():
        h_scr[...] = _modulated(x_ref, g_ref, sh_ref, sc_ref)
        h = h_scr[...]
        if with_gates:
            gate_ref[...] = jnp.dot(h, wg_ref[...], preferred_element_type=F32) + bg_ref[...]
        if with_rope:
            cos = cos_ref[...]
            sin = sin_ref[...]
            lane = lax.broadcasted_iota(jnp.int32, cos.shape, 1)
            first_half = (lane % (LANES // 2)) < (LANES // 4)
            for c in range(wk_ref.shape[1] // MXU_COLS):
                acc = jnp.dot(h, wk_ref[:, c * MXU_COLS:(c + 1) * MXU_COLS], preferred_element_type=F32)
                for half in range(MXU_COLS // LANES):
                    a = acc[:, half * LANES:(half + 1) * LANES]
                    r = a * cos + _swap_halves_32(a, first_half) * sin
                    lo = c * MXU_COLS + half * LANES
                    k_ref[:, lo:lo + LANES] = r.astype(k_ref.dtype)
        else:
            k_ref[...] = jnp.dot(h, wk_ref[...], preferred_element_type=F32).astype(k_ref.dtype)

    def feature_tile():
        return _dot_nt(wt_ref[...], h_scr[...])

    if with_rope:
        @pl.when(jnp.logical_and(j > 0, j <= n_rope_feat_tiles))
        def _():
            cos_t = cost_ref[...]
            sin_t = sint_ref[...]
            for c in range(wt_ref.shape[0] // LANES):
                a = _dot_nt(wt_ref[c * LANES:(c + 1) * LANES, :], h_scr[...])
                r = a * cos_t + _swap_quarter_rows(a) * sin_t
                ft_ref[c * LANES:(c + 1) * LANES, :] = r.astype(ft_ref.dtype)

        @pl.when(j > n_rope_feat_tiles)
        def _():
            ft_ref[...] = feature_tile().astype(ft_ref.dtype)
    else:
        @pl.when(j > 0)
        def _():
            ft_ref[...] = feature_tile().astype(ft_ref.dtype)


def _proj(x, mods, layer, row_fn, g, wk, wt, *, tm, tn, gates=None, rope=None, n_rope_feat=0):
    m, d = x.shape
    nk = wk.shape[1]
    nf = wt.shape[0]
    tm = _tile(m, tm)
    tn = _tile(nf, tn)

    def feat_tile(j):
        return jnp.maximum(j - 1, 0)

    in_specs = [
        pl.BlockSpec((tm, d), lambda i, j: (i, 0)),
        _mod_spec(layer, 0, row_fn, d, 2),
        _mod_spec(layer, 1, row_fn, d, 2),
        pl.BlockSpec((1, d), lambda i, j: (0, 0)),
        pl.BlockSpec((d, nk), lambda i, j: (0, 0)),
        pl.BlockSpec((tn, d), lambda i, j: (feat_tile(j), 0)),
    ]
    args = [x, mods, mods, g, wk, wt]
    out_specs = [pl.BlockSpec((tm, nk), lambda i, j: (i, 0)),
                 pl.BlockSpec((tn, tm), lambda i, j: (feat_tile(j), i))]
    out_shape = [jax.ShapeDtypeStruct((m, nk), BF16), jax.ShapeDtypeStruct((nf, m), BF16)]
    if gates is not None:
        in_specs += [pl.BlockSpec((d, LANES), lambda i, j: (0, 0)),
                     pl.BlockSpec((1, LANES), lambda i, j: (0, 0))]
        args += list(gates)
        out_specs.append(pl.BlockSpec((tm, LANES), lambda i, j: (i, 0)))
        out_shape.append(jax.ShapeDtypeStruct((m, LANES), F32))
    if rope is not None:
        n_pos_tiles = rope[0].shape[0] // tm
        assert rope[0].shape[0] % tm == 0 and (n_rope_feat * LANES) % tn == 0
        in_specs += [pl.BlockSpec((tm, LANES), lambda i, j: (i % n_pos_tiles, 0)),
                     pl.BlockSpec((tm, LANES), lambda i, j: (i % n_pos_tiles, 0)),
                     pl.BlockSpec((LANES, tm), lambda i, j: (0, i % n_pos_tiles)),
                     pl.BlockSpec((LANES, tm), lambda i, j: (0, i % n_pos_tiles))]
        args += list(rope)
    vmem = 2 * tm * d * 4 + tm * d * 2 + 2 * d * nk * 2 + 2 * tn * d * 2 + 2 * tm * nk * 2 + 2 * tn * tm * 2 \
        + 2 * tm * max(nk, tn) * 4 + 12 * tm * LANES * 4 + 2 * d * LANES * 2 + 3 * tm * d * 4
    return pl.pallas_call(
        functools.partial(_proj_kernel, with_gates=gates is not None, with_rope=rope is not None,
                          n_rope_feat_tiles=n_rope_feat * LANES // tn),
        grid=(m // tm, 1 + nf // tn),
        in_specs=in_specs,
        out_specs=out_specs,
        out_shape=out_shape,
        scratch_shapes=[pltpu.VMEM((tm, d), BF16)],
        compiler_params=_params(2, vmem),
        name="proj",
    )(*args)


def _outproj_kernel(y_ref, w_ref, x_ref, gate_ref, g_ref, o_ref):
    t = jnp.dot(y_ref[...], w_ref[...], preferred_element_type=F32)
    ms = jnp.mean(t * t, axis=-1, keepdims=True)
    o_ref[...] = x_ref[...] + gate_ref[...] * (t * lax.rsqrt(ms + NORM_EPS) * g_ref[...])


def _outproj(y, w, x, mods, layer, row_fn, g, *, tm):
    m, d = x.shape
    kdim = y.shape[1]
    tm = _tile(m, tm)
    vmem = 2 * tm * kdim * 2 + 2 * kdim * d * 2 + 4 * tm * d * 4 + 3 * tm * d * 4
    return pl.pallas_call(
        _outproj_kernel,
        grid=(m // tm,),
        in_specs=[
            pl.BlockSpec((tm, kdim), lambda i: (i, 0)),
            pl.BlockSpec((kdim, d), lambda i: (0, 0)),
            pl.BlockSpec((tm, d), lambda i: (i, 0)),
            _mod_spec(layer, 2, row_fn, d, 1),
            pl.BlockSpec((1, d), lambda i: (0, 0)),
        ],
        out_specs=pl.BlockSpec((tm, d), lambda i: (i, 0)),
        out_shape=jax.ShapeDtypeStruct((m, d), F32),
        compiler_params=_params(1, vmem),
        name="outproj",
    )(y, w, x, mods, g)


def _mlp_kernel(x_ref, sh_ref, sc_ref, gate_ref, g_in_ref, g_out_ref, w1_ref, w2_ref, o_ref, h_scr):
    j = pl.program_id(1)

    @pl.when(j == 0)
    def _():
        h_scr[...] = _modulated(x_ref, g_in_ref, sh_ref, sc_ref)
        o_ref[...] = jnp.zeros_like(o_ref)

    a = jnp.dot(h_scr[...], w1_ref[...], preferred_element_type=F32)
    a = jnp.maximum(a, 0.0)
    a = (a * a).astype(BF16)
    o_ref[...] += jnp.dot(a, w2_ref[...], preferred_element_type=F32)

    @pl.when(j == pl.num_programs(1) - 1)
    def _():
        f = o_ref[...]
        ms = jnp.mean(f * f, axis=-1, keepdims=True)
        o_ref[...] = x_ref[...] + gate_ref[...] * (f * lax.rsqrt(ms + NORM_EPS) * g_out_ref[...])


def _mlp(x, mods, layer, row_fn, g_in, g_out, w1, w2, *, tm, tf):
    m, d = x.shape
    dff = w1.shape[1]
    tm = _tile(m, tm)
    tf = _tile(dff, tf)
    vmem = 4 * tm * d * 4 + tm * d * 2 + 4 * d * tf * 2 + 3 * tm * tf * 4 + 4 * 1024 * 1024
    return pl.pallas_call(
        _mlp_kernel,
        grid=(m // tm, dff // tf),
        in_specs=[
            pl.BlockSpec((tm, d), lambda i, j: (i, 0)),
            _mod_spec(layer, 3, row_fn, d, 2),
            _mod_spec(layer, 4, row_fn, d, 2),
            _mod_spec(layer, 5, row_fn, d, 2),
            pl.BlockSpec((1, d), lambda i, j: (0, 0)),
            pl.BlockSpec((1, d), lambda i, j: (0, 0)),
            pl.BlockSpec((d, tf), lambda i, j: (0, j)),
            pl.BlockSpec((tf, d), lambda i, j: (j, 0)),
        ],
        out_specs=pl.BlockSpec((tm, d), lambda i, j: (i, 0)),
        out_shape=jax.ShapeDtypeStruct((m, d), F32),
        scratch_shapes=[pltpu.VMEM((tm, d), BF16)],
        compiler_params=_params(2, vmem),
        name="mlp",
    )(x, mods, mods, mods, g_in, g_out, w1, w2)


def _split3(x):
    x1 = x.astype(BF16)
    r1 = x - x1.astype(F32)
    x2 = r1.astype(BF16)
    x3 = (r1 - x2.astype(F32)).astype(BF16)
    return x1, x2, x3


def _log_sigmoid(x):
    return jnp.minimum(x, 0.0) - jnp.log(1.0 + jnp.exp(-jnp.abs(x)))


def _running_max_lanes(x, reverse):
    lane = lax.broadcasted_iota(jnp.int32, (x.shape[0], LANES), 1)
    slabs = [x[:, j:j + LANES] for j in range(0, x.shape[1], LANES)]
    order = range(len(slabs) - 1, -1, -1) if reverse else range(len(slabs))
    carry = None
    for j in order:
        y = slabs[j]
        k = 1
        while k < LANES:
            if reverse:
                y = jnp.maximum(y, jnp.where(lane < LANES - k, pltpu.roll(y, LANES - k, axis=1), -jnp.inf))
            else:
                y = jnp.maximum(y, jnp.where(lane >= k, pltpu.roll(y, k, axis=1), -jnp.inf))
            k *= 2
        if carry is not None:
            y = jnp.maximum(y, carry)
        carry = jnp.max(y, axis=-1, keepdims=True)
        slabs[j] = y
    return jnp.concatenate(slabs, axis=1)


def _gate_scan_kernel(x_ref, o_ref):
    L = x_ref.shape[-1]
    ui = lax.broadcasted_iota(jnp.int32, (L, L), 0)
    si = lax.broadcasted_iota(jnp.int32, (L, L), 1)

    def cumulative(z, tri):
        return sum(jnp.dot(p, tri.astype(BF16), preferred_element_type=F32) for p in _split3(z))

    b_f = cumulative(_log_sigmoid(x_ref[1]), ui <= si) * LOG2E
    b_b = cumulative(_log_sigmoid(x_ref[3]), ui >= si) * LOG2E
    c_f = x_ref[0] * LOG2E - b_f
    c_b = x_ref[2] * LOG2E - b_b
    o_ref[0] = b_f
    o_ref[1] = c_f
    o_ref[2] = _running_max_lanes(c_f, False)
    o_ref[3] = b_b
    o_ref[4] = c_b
    o_ref[5] = _running_max_lanes(c_b, True)


def _gate_scan(gates, nb, t, L):
    nc = t // L
    g = gates[:, :4 * ML_HEADS].reshape(nb, nc, L, 4, ML_HEADS)
    g = g.transpose(3, 0, 4, 1, 2).reshape(4, nb * ML_HEADS * nc, L)
    nr = g.shape[1]
    tr = _tile(nr, 256)
    n_out = 6
    s = pl.pallas_call(
        _gate_scan_kernel,
        grid=(nr // tr,),
        in_specs=[pl.BlockSpec((4, tr, L), lambda i: (0, i, 0))],
        out_specs=pl.BlockSpec((n_out, tr, L), lambda i: (0, i, 0)),
        out_shape=jax.ShapeDtypeStruct((n_out, nr, L), F32),
        compiler_params=_params(1, 48 * tr * max(L, LANES) * 4),
        name="gate_scan",
    )(g)
    s = s.reshape(n_out, nb, ML_HEADS, nc, L)
    rows = jnp.pad(s.transpose(1, 2, 3, 0, 4), ((0, 0), (0, 0), (0, 0), (0, GATE_ROWS - n_out), (0, 0)))
    return rows


def _mlstm_kernel(kc_ref, qtc_ref, vtc_ref, ogtc_ref, grc_ref,
                  kl_ref, qtl_ref, vtl_ref, ogtl_ref, grl_ref, hg_ref,
                  oc_ref, ol_ref, hf_c, hb_c, hf_l, hb_l):
    L = grc_ref.shape[-1]
    dk = kc_ref.shape[1] // ML_HB
    dv = vtc_ref.shape[0] // ML_HB
    ncc = kc_ref.shape[0] // L
    ncl = kl_ref.shape[0] // L

    si = lax.broadcasted_iota(jnp.int32, (L, L), 0)
    ti = lax.broadcasted_iota(jnp.int32, (L, L), 1)
    ones_rows = (lax.broadcasted_iota(jnp.int32, (ML_AUG, L), 0) == 0).astype(BF16)

    gate_tiles = {}
    for gr_ref, nc in ((grc_ref, ncc), (grl_ref, ncl)):
        for hh in range(ML_HB):
            for c in range(nc):
                tile = gr_ref[hh, c]
                gate_tiles[(id(gr_ref), hh, c)] = (tile, tile.T)

    def chunk(refs, c, hh, backward, state):
        k_ref, qt_ref, vt_ref, gr_ref, hs_ref = refs
        caug, m = state
        tok = slice(c * L, (c + 1) * L)
        kb = k_ref[tok, hh * dk:(hh + 1) * dk]
        qt = qt_ref[hh * dk:(hh + 1) * dk, tok]
        vt_aug = jnp.concatenate([vt_ref[hh * dv:(hh + 1) * dv, tok], ones_rows], axis=0)
        rows, cols = gate_tiles[(id(gr_ref), hh, c)]
        r = 3 if backward else 0
        b_row = rows[r:r + 1, :]
        c_row = rows[r + 1:r + 2, :]
        c_max = rows[r + 2:r + 3, :]
        c_col = cols[:, r + 1:r + 2]
        b_end = b_row[:, 0:1] if backward else b_row[:, L - 1:L]

        inter = b_row + m
        m_row = jnp.maximum(inter, b_row + c_max)
        mask = (si >= ti) if backward else (si <= ti)
        decay = jnp.where(mask, jnp.exp2(c_col + (b_row - m_row)), 0.0)
        w_row = jnp.exp2(inter - m_row)

        p_t = (jnp.dot(kb, qt, preferred_element_type=F32) * decay).astype(BF16)
        lhs = jnp.concatenate([caug.astype(BF16), vt_aug], axis=1)
        rhs = jnp.concatenate([qt * w_row.astype(BF16), p_t], axis=0)
        out = jnp.dot(lhs, rhs, preferred_element_type=F32)
        den = out[dv:dv + 1, :]
        hs_ref[hh * dv:(hh + 1) * dv, tok] = out[0:dv, :] * (1.0 / jnp.maximum(jnp.abs(den), jnp.exp2(-m_row)))

        g_row = b_end + c_row
        m_new = jnp.maximum(b_end + m, jnp.max(g_row, axis=-1, keepdims=True))
        a = jnp.exp2(b_end + m - m_new)
        ws = jnp.exp2(g_row - m_new)
        caug_new = a * caug + jnp.dot(vt_aug * ws.astype(BF16), kb, preferred_element_type=F32)
        return caug_new, m_new

    def scan_segment(nc, k_ref, qt_ref, vt_ref, gr_ref, hf_ref, hb_ref, states):
        fwd_refs = (k_ref, qt_ref, vt_ref, gr_ref, hf_ref)
        bwd_refs = (k_ref, qt_ref, vt_ref, gr_ref, hb_ref)
        for step in range(nc):
            nxt = []
            for hh in range(ML_HB):
                nxt.append(chunk(fwd_refs, step, hh, False, states[2 * hh]))
                nxt.append(chunk(bwd_refs, nc - 1 - step, hh, True, states[2 * hh + 1]))
            states = nxt
        return states

    zero = (jnp.zeros((dv + ML_AUG, dk), F32), jnp.zeros((1, 1), F32))
    st = scan_segment(ncc, kc_ref, qtc_ref, vtc_ref, grc_ref, hf_c, hb_c, [zero] * (2 * ML_HB))
    scan_segment(ncl, kl_ref, qtl_ref, vtl_ref, grl_ref, hf_l, hb_l, st)

    def finish_segment(nc, hf_ref, hb_ref, ogt_ref, o_ref):
        for c in range(nc):
            tok = slice(c * L, (c + 1) * L)
            for hh in range(ML_HB):
                feat = slice(hh * dv, (hh + 1) * dv)
                hs = hf_ref[feat, tok] + hb_ref[feat, tok]
                scale = lax.rsqrt(jnp.mean(hs * hs, axis=0, keepdims=True) + NORM_EPS)
                head_g = jnp.concatenate([hg_ref[feat, :]] * (L // LANES), axis=1)
                y_t = hs * scale * head_g * jax.nn.sigmoid(ogt_ref[feat, tok].astype(F32))
                o_ref[tok, feat] = y_t.T.astype(o_ref.dtype)

    finish_segment(ncc, hf_c, hb_c, ogtc_ref, oc_ref)
    finish_segment(ncl, hf_l, hb_l, ogtl_ref, ol_ref)


def _mlstm(k_ctx, ft_ctx, g_ctx, k_lat, ft_lat, g_lat, head_g, nb, d):
    ctx_len = k_ctx.shape[0] // nb
    seq = k_lat.shape[0] // nb
    dv = ML_HB * (d // ML_HEADS)
    dk = dv // 2
    nh = ML_HEADS // ML_HB
    q_rows = ML_HEADS * (d // ML_HEADS // 2)
    v_blk = q_rows // dv
    og_blk = (q_rows + d) // dv
    L = math.gcd(math.gcd(ctx_len, seq), ML_BLOCK)
    assert L % LANES == 0
    grc = _gate_scan(g_ctx, nb, ctx_len, L)
    grl = _gate_scan(g_lat, nb, seq, L)
    hg = jnp.broadcast_to(head_g.reshape(d, 1), (d, LANES))

    def stream_specs(t):
        return [
            pl.BlockSpec((t, dk), lambda b, i: (b, i)),
            pl.BlockSpec((dk, t), lambda b, i: (i, b)),
            pl.BlockSpec((dv, t), lambda b, i: (v_blk + i, b)),
            pl.BlockSpec((dv, t), lambda b, i: (og_blk + i, b)),
            pl.BlockSpec((None, ML_HB, t // L, GATE_ROWS, L), lambda b, i: (b, i, 0, 0, 0)),
        ]

    t_all = ctx_len + seq
    gate_tiles = 2 * ML_HB * (t_all // L) * GATE_ROWS * L * 4
    vmem = 2 * t_all * (2 * dk * 2 + 3 * dv * 2) + 2 * t_all * dv * 4 + gate_tiles + 12 * 1024 * 1024
    return pl.pallas_call(
        _mlstm_kernel,
        grid=(nb, nh),
        in_specs=stream_specs(ctx_len) + stream_specs(seq) + [pl.BlockSpec((dv, LANES), lambda b, i: (i, 0))],
        out_specs=[pl.BlockSpec((ctx_len, dv), lambda b, i: (b, i)),
                   pl.BlockSpec((seq, dv), lambda b, i: (b, i))],
        out_shape=[jax.ShapeDtypeStruct((nb * ctx_len, d), BF16),
                   jax.ShapeDtypeStruct((nb * seq, d), BF16)],
        scratch_shapes=[pltpu.VMEM((dv, ctx_len), F32), pltpu.VMEM((dv, ctx_len), F32),
                        pltpu.VMEM((dv, seq), F32), pltpu.VMEM((dv, seq), F32)],
        compiler_params=_params(2, vmem),
        name="mlstm",
    )(k_ctx, ft_ctx, ft_ctx, ft_ctx, grc, k_lat, ft_lat, ft_lat, ft_lat, grl, hg)


def _attn_kernel(kl_ref, kc_ref, qt_ref, vlt_ref, vct_ref, lam_ref, gs_ref, o_ref, *, lam_init):
    dh = kl_ref.shape[1] // 2
    dvh = vlt_ref.shape[0]
    seq = kl_ref.shape[0]
    ctx_len = kc_ref.shape[0]
    lq = lam_ref[...]
    lam = jnp.exp(jnp.sum(lq[0:1] * lq[1:2], axis=-1, keepdims=True)) \
        - jnp.exp(jnp.sum(lq[2:3] * lq[3:4], axis=-1, keepdims=True)) + lam_init

    tq = _tile(seq, ATTN_TQ)
    tk = _tile(seq, ATTN_TK)
    blocks = [(kl_ref, vlt_ref, j * tk, tk) for j in range(seq // tk)] + [(kc_ref, vct_ref, 0, ctx_len)]
    g_sub = jnp.concatenate([gs_ref[...]] * (tq // LANES), axis=1) * (1.0 - lam_init)

    def ones_rows(size):
        return (lax.broadcasted_iota(jnp.int32, (ML_AUG, size), 0) == 0).astype(BF16)

    def scores(t, blk):
        k_ref, _, start, size = blocks[blk]
        tok = slice(t * tq, (t + 1) * tq)
        keys = slice(start, start + size)
        return jnp.concatenate(
            [jnp.dot(k_ref[keys, 0:dh], qt_ref[0:dh, tok], preferred_element_type=F32),
             jnp.dot(k_ref[keys, dh:2 * dh], qt_ref[dh:2 * dh, tok], preferred_element_type=F32)],
            axis=1)

    items = [(t, blk) for t in range(seq // tq) for blk in range(len(blocks))]
    s_t = scores(*items[0])
    m = acc = None
    for n, (t, blk) in enumerate(items):
        s_next = scores(*items[n + 1]) if n + 1 < len(items) else None
        _, vt_ref, start, size = blocks[blk]
        mx = jnp.max(s_t, axis=0, keepdims=True)
        m_new = mx if blk == 0 else jnp.maximum(m, mx)
        p_t = jnp.exp2(s_t - m_new).astype(BF16)
        vt_aug = jnp.concatenate([vt_ref[:, start:start + size], ones_rows(size)], axis=0)
        pv = jnp.dot(vt_aug, p_t, preferred_element_type=F32)
        acc = pv if blk == 0 else jnp.exp2(m - m_new) * acc + pv
        m = m_new
        if blk == len(blocks) - 1:
            a = acc[0:dvh, :] * (1.0 / acc[dvh:dvh + 1, :])
            o_t = a[:, 0:tq] - lam * a[:, tq:2 * tq]
            scale = lax.rsqrt(jnp.mean(o_t * o_t, axis=0, keepdims=True) + NORM_EPS)
            o_ref[t * tq:(t + 1) * tq, :] = (o_t * scale * g_sub).T.astype(o_ref.dtype)
        s_t = s_next


def _attn(k_lat, ft_lat, k_ctx, ft_ctx, lam_qk, g_sub, nb, d, lam_init):
    seq = k_lat.shape[0] // nb
    ctx_len = k_ctx.shape[0] // nb
    h = DA_HEADS
    hd = d // h
    tq = _tile(seq, ATTN_TQ)
    tk = _tile(seq, ATTN_TK)
    gs = jnp.broadcast_to(g_sub.reshape(hd, 1), (hd, LANES))
    vmem = 2 * (4 * seq + 2 * ctx_len) * hd * 2 + 16 * tq * tk * 4 + 16 * tq * hd * 4 + 8 * 1024 * 1024
    return pl.pallas_call(
        functools.partial(_attn_kernel, lam_init=lam_init),
        grid=(nb, h),
        in_specs=[
            pl.BlockSpec((seq, hd), lambda b, i: (b, i)),
            pl.BlockSpec((ctx_len, hd), lambda b, i: (b, i)),
            pl.BlockSpec((hd, seq), lambda b, i: (i, b)),
            pl.BlockSpec((hd, seq), lambda b, i: (h + i, b)),
            pl.BlockSpec((hd, ctx_len), lambda b, i: (i, b)),
            pl.BlockSpec(lam_qk.shape, lambda b, i: (0, 0)),
            pl.BlockSpec((hd, LANES), lambda b, i: (0, 0)),
        ],
        out_specs=pl.BlockSpec((seq, hd), lambda b, i: (b, i)),
        out_shape=jax.ShapeDtypeStruct((nb * seq, d), BF16),
        compiler_params=_params(2, vmem),
        name="attn",
    )(k_lat, k_ctx, ft_lat, ft_lat, ft_ctx, lam_qk, gs)


def _rope_tables(seq, dh):
    nf = dh // 4
    n = jnp.arange(seq)
    row = (n // GRID_W).astype(F32)
    col = (n % GRID_W).astype(F32)
    freq = ROPE_BASE ** (-jnp.arange(nf, dtype=F32) / nf)
    ar = row[:, None] * freq
    ac = col[:, None] * freq
    cos = jnp.concatenate([jnp.cos(ar), jnp.cos(ar), jnp.cos(ac), jnp.cos(ac)], axis=1)
    sin = jnp.concatenate([-jnp.sin(ar), jnp.sin(ar), -jnp.sin(ac), jnp.sin(ac)], axis=1)
    return cos, sin


def kernel(x, c, ctx, c_ctx, ada_w, ada_b, norm_g, mlp_w1, mlp_w2, ml_w_in, ml_b_gates, ml_head_g,
           ml_w_out, da_w_in, da_lambda, da_sub_g, da_w_out):
    nb, seq, d = x.shape
    ctx_len = ctx.shape[1]
    depth = ada_w.shape[0]
    assert depth == 2 and nb < COND_ROWS and d // ML_HEADS == 2 * LANES
    assert seq % LANES == 0 and ctx_len % LANES == 0 and seq % GRID_W == 0

    xl = x.reshape(nb * seq, d)
    xc = ctx.reshape(nb * ctx_len, d)

    cond = jnp.zeros((COND_ROWS, d), F32).at[:nb].set(c).at[nb].set(c_ctx)
    mods = _ada(cond, ada_w, ada_b).reshape(depth, COND_ROWS, 1, N_MOD * d)

    tm = _tile(seq, 512)
    tm_mlp = _tile(seq, 1024)
    tmc = _tile(nb * ctx_len, 512)

    def lat_row_of(tile_rows):
        return lambda i: i // (seq // tile_rows)

    lat_row = lat_row_of(tm)
    lat_row_mlp = lat_row_of(tm_mlp)

    def ctx_row(i):
        return nb

    g = norm_g.reshape(depth, 4, 1, d)

    ml_qk = ML_HEADS * (d // ML_HEADS // 2)
    n_main = 2 * ml_qk + 2 * d
    w_in = ml_w_in[0]
    q_scale = (d // ML_HEADS // 2) ** -0.5
    w_k = w_in[:, ml_qk:2 * ml_qk].astype(BF16)
    w_feat_t = jnp.concatenate([w_in[:, :ml_qk] * q_scale, w_in[:, 2 * ml_qk:n_main]], axis=1).T.astype(BF16)
    n_gate = 4 * ML_HEADS
    w_gate = jnp.pad(w_in[:, n_main:], ((0, 0), (0, LANES - n_gate))).astype(BF16)
    b_gate = jnp.pad(ml_b_gates[0], (0, LANES - n_gate)).reshape(1, LANES)
    w_out0 = ml_w_out[0].astype(BF16)
    w1_0 = mlp_w1[0].astype(BF16)
    w2_0 = mlp_w2[0].astype(BF16)

    k_lat, ft_lat, g_lat = _proj(xl, mods, 0, lat_row_mlp, g[0, 0], w_k, w_feat_t, tm=tm_mlp, tn=1024,
                                 gates=(w_gate, b_gate))
    k_ctx, ft_ctx, g_ctx = _proj(xc, mods, 0, ctx_row, g[0, 0], w_k, w_feat_t, tm=tmc, tn=1024,
                                 gates=(w_gate, b_gate))
    y_ctx, y_lat = _mlstm(k_ctx, ft_ctx, g_ctx, k_lat, ft_lat, g_lat, ml_head_g[0], nb, d)
    xl = _outproj(y_lat, w_out0, xl, mods, 0, lat_row, g[0, 1], tm=tm)
    xc = _outproj(y_ctx, w_out0, xc, mods, 0, ctx_row, g[0, 1], tm=tmc)
    xl = _mlp(xl, mods, 0, lat_row_mlp, g[0, 2], g[0, 3], w1_0, w2_0, tm=tm_mlp, tf=512)
    xc = _mlp(xc, mods, 0, ctx_row, g[0, 2], g[0, 3], w1_0, w2_0, tm=tmc, tf=512)

    dh = d // (2 * DA_HEADS)
    w_in = da_w_in[0]
    w_k1 = w_in[:, d:2 * d].astype(BF16)
    w_v_t = w_in[:, 2 * d:].T.astype(BF16)
    w_qv_t = jnp.concatenate([(w_in[:, :d] * (dh ** -0.5 * LOG2E)).T.astype(BF16), w_v_t], axis=0)
    w_out1 = da_w_out[0].astype(BF16)
    w1_1 = mlp_w1[1].astype(BF16)
    w2_1 = mlp_w2[1].astype(BF16)
    layer_idx = 1
    lam_init = 0.8 - 0.6 * math.exp(-0.3 * layer_idx)

    cos, sin = _rope_tables(seq, dh)
    k_lat, ft_lat = _proj(xl, mods, 1, lat_row, g[1, 0], w_k1, w_qv_t, tm=tm, tn=1024,
                          rope=(cos, sin, cos.T, sin.T), n_rope_feat=d // LANES)
    k_ctx, ft_ctx = _proj(xc, mods, 1, ctx_row, g[1, 0], w_k1, w_v_t, tm=tmc, tn=1024)
    y_lat = _attn(k_lat, ft_lat, k_ctx, ft_ctx, da_lambda[0], da_sub_g[0], nb, d, lam_init)
    xl = _outproj(y_lat, w_out1, xl, mods, 1, lat_row, g[1, 1], tm=tm)
    xl = _mlp(xl, mods, 1, lat_row_mlp, g[1, 2], g[1, 3], w1_1, w2_1, tm=tm_mlp, tf=512)
    return xl.reshape(nb, seq, d)
```

```python
import functools
import math

import jax
import jax.numpy as jnp
from jax import lax
from jax.experimental import pallas as pl
from jax.experimental.pallas import tpu as pltpu

F32 = jnp.float32
BF16 = jnp.bfloat16

NORM_EPS = 1e-6
GRID_W = 64
ROPE_BASE = 10000.0
LOG2E = math.log2(math.e)
ML_HEADS = 8
ML_BLOCK = 256
ML_HB = 2
ML_AUG = 16
DA_HEADS = 8
ATTN_TQ = 512
ATTN_TK = 512
N_MOD = 6
COND_ROWS = 32
LANES = 128
MXU_COLS = 256
GATE_ROWS = 8

VMEM_CAP_BYTES = 56 * 1024 * 1024


def _vmem_limit(nbytes):
    return int(min(VMEM_CAP_BYTES, max(16 * 1024 * 1024, nbytes)))


def _params(n_grid, vmem_bytes):
    return pltpu.CompilerParams(
        dimension_semantics=("arbitrary",) * n_grid,
        vmem_limit_bytes=_vmem_limit(vmem_bytes))


def _tile(n, want):
    if n <= want:
        return n
    t = want
    while t >= 8:
        if n % t == 0 and t % 8 == 0:
            return t
        t -= 8
    return n


def _dot_nt(a, b):
    return lax.dot_general(a, b, (((1,), (1,)), ((), ())), preferred_element_type=F32)


def _ada_kernel(c_ref, w_ref, b_ref, o_ref):
    cf = c_ref[...]
    s = (cf * jax.nn.sigmoid(cf)).astype(BF16)
    o_ref[...] = jnp.dot(s, w_ref[...].astype(BF16), preferred_element_type=F32) + b_ref[...]


def _ada(cond, ada_w, ada_b):
    depth, d, n = ada_w.shape
    tn = _tile(n, 1024)
    return pl.pallas_call(
        _ada_kernel,
        grid=(depth, n // tn),
        in_specs=[
            pl.BlockSpec((COND_ROWS, d), lambda l, j: (0, 0)),
            pl.BlockSpec((None, d, tn), lambda l, j: (l, 0, j)),
            pl.BlockSpec((None, 1, tn), lambda l, j: (l, 0, j)),
        ],
        out_specs=pl.BlockSpec((None, COND_ROWS, tn), lambda l, j: (l, 0, j)),
        out_shape=jax.ShapeDtypeStruct((depth, COND_ROWS, n), F32),
        compiler_params=_params(2, 2 * d * tn * 4 + 4 * COND_ROWS * (d + tn) * 4 + d * tn * 2),
        name="ada",
    )(cond, ada_w, ada_b.reshape(depth, 1, n))


def _mod_spec(layer, piece, row_fn, d, n_grid):
    if n_grid == 1:
        return pl.BlockSpec((None, None, 1, d), lambda i: (layer, row_fn(i), 0, piece))
    return pl.BlockSpec((None, None, 1, d), lambda i, j: (layer, row_fn(i), 0, piece))


def _modulated(x_ref, g_ref, sh_ref, sc_ref):
    xf = x_ref[...]
    ms = jnp.mean(xf * xf, axis=-1, keepdims=True)
    gain = g_ref[...] * (1.0 + sc_ref[...])
    return (xf * lax.rsqrt(ms + NORM_EPS) * gain + sh_ref[...]).astype(BF16)


def _swap_halves_32(a, first_half):
    fwd = pltpu.roll(a, 3 * LANES // 4, axis=1)
    bwd = pltpu.roll(a, LANES // 4, axis=1)
    return jnp.where(first_half, fwd, bwd)


def _proj_kernel(*refs, n_tok, n_rope_tok, with_feat, with_gates):
    x_ref, sh_ref, sc_ref, g_ref, wk_ref = refs[:5]
    n = 5
    if with_feat:
        wt_ref = refs[n]
        n += 1
    if with_gates:
        wg_ref, bg_ref = refs[n:n + 2]
        n += 2
    if n_rope_tok:
        cos_ref, sin_ref = refs[n:n + 2]
        n += 2
    k_ref = refs[n]
    n += 1
    if with_feat:
        ft_ref = refs[n]
        n += 1
    if with_gates:
        gate_ref = refs[n]
    h_scr = refs[-1]
    j = pl.program_id(1)

    @pl.when(j == 0)
    def _():
        h_scr[...] = _modulated(x_ref, g_ref, sh_ref, sc_ref)
        if with_gates:
            gate_ref[...] = jnp.dot(h_scr[...], wg_ref[...], preferred_element_type=F32) + bg_ref[...]

    if n_rope_tok:
        @pl.when(j < n_rope_tok)
        def _():
            cos = cos_ref[...]
            sin = sin_ref[...]
            lane = lax.broadcasted_iota(jnp.int32, cos.shape, 1)
            first_half = (lane % (LANES // 2)) < (LANES // 4)
            for c in range(wk_ref.shape[1] // MXU_COLS):
                acc = jnp.dot(h_scr[...], wk_ref[:, c * MXU_COLS:(c + 1) * MXU_COLS],
                              preferred_element_type=F32)
                for half in range(MXU_COLS // LANES):
                    a = acc[:, half * LANES:(half + 1) * LANES]
                    r = a * cos + _swap_halves_32(a, first_half) * sin
                    lo = c * MXU_COLS + half * LANES
                    k_ref[:, lo:lo + LANES] = r.astype(k_ref.dtype)

    @pl.when(jnp.logical_and(j >= n_rope_tok, j < n_tok))
    def _():
        k_ref[...] = jnp.dot(h_scr[...], wk_ref[...], preferred_element_type=F32).astype(k_ref.dtype)

    if with_feat:
        @pl.when(j >= n_tok)
        def _():
            ft_ref[...] = _dot_nt(wt_ref[...], h_scr[...]).astype(ft_ref.dtype)


def _proj(x, mods, layer, row_fn, g, wk, wt=None, *, tm, tn, gates=None, rope=None, rope_cols=0):
    m, d = x.shape
    nk = wk.shape[1]
    tm = _tile(m, tm)
    tnk = _tile(nk, tn)
    n_tok = nk // tnk
    assert rope_cols % tnk == 0

    def tok_tile(j):
        return jnp.minimum(j, n_tok - 1)

    def feat_tile(j):
        return jnp.maximum(j - n_tok, 0)

    in_specs = [
        pl.BlockSpec((tm, d), lambda i, j: (i, 0)),
        _mod_spec(layer, 0, row_fn, d, 2),
        _mod_spec(layer, 1, row_fn, d, 2),
        pl.BlockSpec((1, d), lambda i, j: (0, 0)),
        pl.BlockSpec((d, tnk), lambda i, j: (0, tok_tile(j))),
    ]
    args = [x, mods, mods, g, wk]
    out_specs = [pl.BlockSpec((tm, tnk), lambda i, j: (i, tok_tile(j)))]
    out_shape = [jax.ShapeDtypeStruct((m, nk), BF16)]
    n_feat = tnf = 0
    if wt is not None:
        nf = wt.shape[0]
        tnf = _tile(nf, tn)
        n_feat = nf // tnf
        in_specs.append(pl.BlockSpec((tnf, d), lambda i, j: (feat_tile(j), 0)))
        args.append(wt)
        out_specs.append(pl.BlockSpec((tnf, tm), lambda i, j: (feat_tile(j), i)))
        out_shape.append(jax.ShapeDtypeStruct((nf, m), BF16))
    if gates is not None:
        in_specs += [pl.BlockSpec((d, LANES), lambda i, j: (0, 0)),
                     pl.BlockSpec((1, LANES), lambda i, j: (0, 0))]
        args += list(gates)
        out_specs.append(pl.BlockSpec((tm, LANES), lambda i, j: (i, 0)))
        out_shape.append(jax.ShapeDtypeStruct((m, LANES), F32))
    if rope is not None:
        n_pos_tiles = rope[0].shape[0] // tm
        assert rope[0].shape[0] % tm == 0
        in_specs += [pl.BlockSpec((tm, LANES), lambda i, j: (i % n_pos_tiles, 0)),
                     pl.BlockSpec((tm, LANES), lambda i, j: (i % n_pos_tiles, 0))]
        args += list(rope)
    vmem = 2 * tm * d * 4 + tm * d * 2 + 2 * d * tnk * 2 + 2 * tnf * d * 2 + 2 * tm * tnk * 2 + 2 * tnf * tm * 2 \
        + 2 * tm * max(tnk, tnf) * 4 + 8 * tm * LANES * 4 + 2 * d * LANES * 2 + 3 * tm * d * 4
    out = pl.pallas_call(
        functools.partial(_proj_kernel, n_tok=n_tok, n_rope_tok=rope_cols // tnk if rope is not None else 0,
                          with_feat=wt is not None, with_gates=gates is not None),
        grid=(m // tm, n_tok + n_feat),
        in_specs=in_specs,
        out_specs=out_specs,
        out_shape=out_shape,
        scratch_shapes=[pltpu.VMEM((tm, d), BF16)],
        compiler_params=_params(2, vmem),
        name="proj",
    )(*args)
    return out if len(out) > 1 else out[0]


def _outproj_kernel(y_ref, w_ref, x_ref, gate_ref, g_ref, o_ref):
    t = jnp.dot(y_ref[...], w_ref[...], preferred_element_type=F32)
    ms = jnp.mean(t * t, axis=-1, keepdims=True)
    o_ref[...] = x_ref[...] + gate_ref[...] * (t * lax.rsqrt(ms + NORM_EPS) * g_ref[...])


def _outproj(y, w, x, mods, layer, row_fn, g, *, tm):
    m, d = x.shape
    kdim = y.shape[1]
    tm = _tile(m, tm)
    vmem = 2 * tm * kdim * 2 + 2 * kdim * d * 2 + 4 * tm * d * 4 + 3 * tm * d * 4
    return pl.pallas_call(
        _outproj_kernel,
        grid=(m // tm,),
        in_specs=[
            pl.BlockSpec((tm, kdim), lambda i: (i, 0)),
            pl.BlockSpec((kdim, d), lambda i: (0, 0)),
            pl.BlockSpec((tm, d), lambda i: (i, 0)),
            _mod_spec(layer, 2, row_fn, d, 1),
            pl.BlockSpec((1, d), lambda i: (0, 0)),
        ],
        out_specs=pl.BlockSpec((tm, d), lambda i: (i, 0)),
        out_shape=jax.ShapeDtypeStruct((m, d), F32),
        compiler_params=_params(1, vmem),
        name="outproj",
    )(y, w, x, mods, g)


def _mlp_kernel(x_ref, sh_ref, sc_ref, gate_ref, g_in_ref, g_out_ref, w1_ref, w2_ref, o_ref, h_scr):
    j = pl.program_id(1)

    @pl.when(j == 0)
    def _():
        h_scr[...] = _modulated(x_ref, g_in_ref, sh_ref, sc_ref)
        o_ref[...] = jnp.zeros_like(o_ref)

    a = jnp.dot(h_scr[...], w1_ref[...], preferred_element_type=F32)
    a = jnp.maximum(a, 0.0)
    a = (a * a).astype(BF16)
    o_ref[...] += jnp.dot(a, w2_ref[...], preferred_element_type=F32)

    @pl.when(j == pl.num_programs(1) - 1)
    def _():
        f = o_ref[...]
        ms = jnp.mean(f * f, axis=-1, keepdims=True)
        o_ref[...] = x_ref[...] + gate_ref[...] * (f * lax.rsqrt(ms + NORM_EPS) * g_out_ref[...])


def _mlp(x, mods, layer, row_fn, g_in, g_out, w1, w2, *, tm, tf):
    m, d = x.shape
    dff = w1.shape[1]
    tm = _tile(m, tm)
    tf = _tile(dff, tf)
    vmem = 4 * tm * d * 4 + tm * d * 2 + 4 * d * tf * 2 + 3 * tm * tf * 4 + 4 * 1024 * 1024
    return pl.pallas_call(
        _mlp_kernel,
        grid=(m // tm, dff // tf),
        in_specs=[
            pl.BlockSpec((tm, d), lambda i, j: (i, 0)),
            _mod_spec(layer, 3, row_fn, d, 2),
            _mod_spec(layer, 4, row_fn, d, 2),
            _mod_spec(layer, 5, row_fn, d, 2),
            pl.BlockSpec((1, d), lambda i, j: (0, 0)),
            pl.BlockSpec((1, d), lambda i, j: (0, 0)),
            pl.BlockSpec((d, tf), lambda i, j: (0, j)),
            pl.BlockSpec((tf, d), lambda i, j: (j, 0)),
        ],
        out_specs=pl.BlockSpec((tm, d), lambda i, j: (i, 0)),
        out_shape=jax.ShapeDtypeStruct((m, d), F32),
        scratch_shapes=[pltpu.VMEM((tm, d), BF16)],
        compiler_params=_params(2, vmem),
        name="mlp",
    )(x, mods, mods, mods, g_in, g_out, w1, w2)


def _split3(x):
    x1 = x.astype(BF16)
    r1 = x - x1.astype(F32)
    x2 = r1.astype(BF16)
    x3 = (r1 - x2.astype(F32)).astype(BF16)
    return x1, x2, x3


def _log_sigmoid(x):
    return jnp.minimum(x, 0.0) - jnp.log(1.0 + jnp.exp(-jnp.abs(x)))


def _running_max_lanes(x, reverse):
    lane = lax.broadcasted_iota(jnp.int32, (x.shape[0], LANES), 1)
    slabs = [x[:, j:j + LANES] for j in range(0, x.shape[1], LANES)]
    order = range(len(slabs) - 1, -1, -1) if reverse else range(len(slabs))
    carry = None
    for j in order:
        y = slabs[j]
        k = 1
        while k < LANES:
            if reverse:
                y = jnp.maximum(y, jnp.where(lane < LANES - k, pltpu.roll(y, LANES - k, axis=1), -jnp.inf))
            else:
                y = jnp.maximum(y, jnp.where(lane >= k, pltpu.roll(y, k, axis=1), -jnp.inf))
            k *= 2
        if carry is not None:
            y = jnp.maximum(y, carry)
        carry = jnp.max(y, axis=-1, keepdims=True)
        slabs[j] = y
    return jnp.concatenate(slabs, axis=1)


def _gate_scan_kernel(x_ref, o_ref):
    L = x_ref.shape[-1]
    ui = lax.broadcasted_iota(jnp.int32, (L, L), 0)
    si = lax.broadcasted_iota(jnp.int32, (L, L), 1)

    def cumulative(z, tri):
        return sum(jnp.dot(p, tri.astype(BF16), preferred_element_type=F32) for p in _split3(z))

    b_f = cumulative(_log_sigmoid(x_ref[1]), ui <= si) * LOG2E
    b_b = cumulative(_log_sigmoid(x_ref[3]), ui >= si) * LOG2E
    c_f = x_ref[0] * LOG2E - b_f
    c_b = x_ref[2] * LOG2E - b_b
    o_ref[0] = b_f
    o_ref[1] = c_f
    o_ref[2] = _running_max_lanes(c_f, False)
    o_ref[3] = b_b
    o_ref[4] = c_b
    o_ref[5] = _running_max_lanes(c_b, True)


def _gate_scan(gates, nb, t, L):
    nc = t // L
    g = gates[:, :4 * ML_HEADS].reshape(nb, nc, L, 4, ML_HEADS)
    g = g.transpose(3, 0, 4, 1, 2).reshape(4, nb * ML_HEADS * nc, L)
    nr = g.shape[1]
    tr = _tile(nr, 256)
    n_out = 6
    s = pl.pallas_call(
        _gate_scan_kernel,
        grid=(nr // tr,),
        in_specs=[pl.BlockSpec((4, tr, L), lambda i: (0, i, 0))],
        out_specs=pl.BlockSpec((n_out, tr, L), lambda i: (0, i, 0)),
        out_shape=jax.ShapeDtypeStruct((n_out, nr, L), F32),
        compiler_params=_params(1, 48 * tr * max(L, LANES) * 4),
        name="gate_scan",
    )(g)
    s = s.reshape(n_out, nb, ML_HEADS, nc, L)
    rows = jnp.pad(s.transpose(1, 2, 3, 0, 4), ((0, 0), (0, 0), (0, 0), (0, GATE_ROWS - n_out), (0, 0)))
    return rows


def _mlstm_kernel(kc_ref, qtc_ref, vtc_ref, ogtc_ref, grc_ref,
                  kl_ref, qtl_ref, vtl_ref, ogtl_ref, grl_ref, hg_ref,
                  oc_ref, ol_ref, hf_c, hb_c, hf_l, hb_l):
    L = grc_ref.shape[-1]
    dk = kc_ref.shape[1] // ML_HB
    dv = vtc_ref.shape[0] // ML_HB
    ncc = kc_ref.shape[0] // L
    ncl = kl_ref.shape[0] // L

    si = lax.broadcasted_iota(jnp.int32, (L, L), 0)
    ti = lax.broadcasted_iota(jnp.int32, (L, L), 1)
    ones_rows = (lax.broadcasted_iota(jnp.int32, (ML_AUG, L), 0) == 0).astype(BF16)

    gate_tiles = {}
    for gr_ref, nc in ((grc_ref, ncc), (grl_ref, ncl)):
        for hh in range(ML_HB):
            for c in range(nc):
                tile = gr_ref[hh, c]
                gate_tiles[(id(gr_ref), hh, c)] = (tile, tile.T)

    def chunk(refs, c, hh, backward, state):
        k_ref, qt_ref, vt_ref, gr_ref, hs_ref = refs
        caug, m = state
        tok = slice(c * L, (c + 1) * L)
        kb = k_ref[tok, hh * dk:(hh + 1) * dk]
        qt = qt_ref[hh * dk:(hh + 1) * dk, tok]
        vt_aug = jnp.concatenate([vt_ref[hh * dv:(hh + 1) * dv, tok], ones_rows], axis=0)
        rows, cols = gate_tiles[(id(gr_ref), hh, c)]
        r = 3 if backward else 0
        b_row = rows[r:r + 1, :]
        c_row = rows[r + 1:r + 2, :]
        c_max = rows[r + 2:r + 3, :]
        c_col = cols[:, r + 1:r + 2]
        b_end = b_row[:, 0:1] if backward else b_row[:, L - 1:L]

        inter = b_row + m
        m_row = jnp.maximum(inter, b_row + c_max)
        mask = (si >= ti) if backward else (si <= ti)
        decay = jnp.where(mask, jnp.exp2(c_col + (b_row - m_row)), 0.0)
        w_row = jnp.exp2(inter - m_row)

        p_t = (jnp.dot(kb, qt, preferred_element_type=F32) * decay).astype(BF16)
        lhs = jnp.concatenate([caug.astype(BF16), vt_aug], axis=1)
        rhs = jnp.concatenate([qt * w_row.astype(BF16), p_t], axis=0)
        out = jnp.dot(lhs, rhs, preferred_element_type=F32)
        den = out[dv:dv + 1, :]
        hs_ref[hh * dv:(hh + 1) * dv, tok] = out[0:dv, :] * (1.0 / jnp.maximum(jnp.abs(den), jnp.exp2(-m_row)))

        g_row = b_end + c_row
        m_new = jnp.maximum(b_end + m, jnp.max(g_row, axis=-1, keepdims=True))
        a = jnp.exp2(b_end + m - m_new)
        ws = jnp.exp2(g_row - m_new)
        caug_new = a * caug + jnp.dot(vt_aug * ws.astype(BF16), kb, preferred_element_type=F32)
        return caug_new, m_new

    def scan_segment(nc, k_ref, qt_ref, vt_ref, gr_ref, hf_ref, hb_ref, states):
        fwd_refs = (k_ref, qt_ref, vt_ref, gr_ref, hf_ref)
        bwd_refs = (k_ref, qt_ref, vt_ref, gr_ref, hb_ref)
        for step in range(nc):
            nxt = []
            for hh in range(ML_HB):
                nxt.append(chunk(fwd_refs, step, hh, False, states[2 * hh]))
                nxt.append(chunk(bwd_refs, nc - 1 - step, hh, True, states[2 * hh + 1]))
            states = nxt
        return states

    zero = (jnp.zeros((dv + ML_AUG, dk), F32), jnp.zeros((1, 1), F32))
    st = scan_segment(ncc, kc_ref, qtc_ref, vtc_ref, grc_ref, hf_c, hb_c, [zero] * (2 * ML_HB))
    scan_segment(ncl, kl_ref, qtl_ref, vtl_ref, grl_ref, hf_l, hb_l, st)

    def finish_segment(nc, hf_ref, hb_ref, ogt_ref, o_ref):
        for c in range(nc):
            tok = slice(c * L, (c + 1) * L)
            for hh in range(ML_HB):
                feat = slice(hh * dv, (hh + 1) * dv)
                hs = hf_ref[feat, tok] + hb_ref[feat, tok]
                scale = lax.rsqrt(jnp.mean(hs * hs, axis=0, keepdims=True) + NORM_EPS)
                head_g = jnp.concatenate([hg_ref[feat, :]] * (L // LANES), axis=1)
                y_t = hs * scale * head_g * jax.nn.sigmoid(ogt_ref[feat, tok].astype(F32))
                o_ref[tok, feat] = y_t.T.astype(o_ref.dtype)

    finish_segment(ncc, hf_c, hb_c, ogtc_ref, oc_ref)
    finish_segment(ncl, hf_l, hb_l, ogtl_ref, ol_ref)


def _mlstm(k_ctx, ft_ctx, g_ctx, k_lat, ft_lat, g_lat, head_g, nb, d):
    ctx_len = k_ctx.shape[0] // nb
    seq = k_lat.shape[0] // nb
    dv = ML_HB * (d // ML_HEADS)
    dk = dv // 2
    nh = ML_HEADS // ML_HB
    q_rows = ML_HEADS * (d // ML_HEADS // 2)
    v_blk = q_rows // dv
    og_blk = (q_rows + d) // dv
    L = math.gcd(math.gcd(ctx_len, seq), ML_BLOCK)
    assert L % LANES == 0
    grc = _gate_scan(g_ctx, nb, ctx_len, L)
    grl = _gate_scan(g_lat, nb, seq, L)
    hg = jnp.broadcast_to(head_g.reshape(d, 1), (d, LANES))

    def stream_specs(t):
        return [
            pl.BlockSpec((t, dk), lambda b, i: (b, i)),
            pl.BlockSpec((dk, t), lambda b, i: (i, b)),
            pl.BlockSpec((dv, t), lambda b, i: (v_blk + i, b)),
            pl.BlockSpec((dv, t), lambda b, i: (og_blk + i, b)),
            pl.BlockSpec((None, ML_HB, t // L, GATE_ROWS, L), lambda b, i: (b, i, 0, 0, 0)),
        ]

    t_all = ctx_len + seq
    gate_tiles = 2 * ML_HB * (t_all // L) * GATE_ROWS * L * 4
    vmem = 2 * t_all * (2 * dk * 2 + 3 * dv * 2) + 2 * t_all * dv * 4 + gate_tiles + 12 * 1024 * 1024
    return pl.pallas_call(
        _mlstm_kernel,
        grid=(nb, nh),
        in_specs=stream_specs(ctx_len) + stream_specs(seq) + [pl.BlockSpec((dv, LANES), lambda b, i: (i, 0))],
        out_specs=[pl.BlockSpec((ctx_len, dv), lambda b, i: (b, i)),
                   pl.BlockSpec((seq, dv), lambda b, i: (b, i))],
        out_shape=[jax.ShapeDtypeStruct((nb * ctx_len, d), BF16),
                   jax.ShapeDtypeStruct((nb * seq, d), BF16)],
        scratch_shapes=[pltpu.VMEM((dv, ctx_len), F32), pltpu.VMEM((dv, ctx_len), F32),
                        pltpu.VMEM((dv, seq), F32), pltpu.VMEM((dv, seq), F32)],
        compiler_params=_params(2, vmem),
        name="mlstm",
    )(k_ctx, ft_ctx, ft_ctx, ft_ctx, grc, k_lat, ft_lat, ft_lat, ft_lat, grl, hg)


def _attn_kernel(q_ref, kl_ref, vl_ref, kc_ref, vc_ref, lam_ref, gs_ref, o_ref, *, lam_init):
    dh = q_ref.shape[1] // 2
    seq = q_ref.shape[0]
    ctx_len = kc_ref.shape[0]
    lq = lam_ref[...]
    lam = jnp.exp(jnp.sum(lq[0:1] * lq[1:2], axis=-1, keepdims=True)) \
        - jnp.exp(jnp.sum(lq[2:3] * lq[3:4], axis=-1, keepdims=True)) + lam_init

    tq = _tile(seq, ATTN_TQ)
    tk = _tile(seq, ATTN_TK)
    blocks = [(kl_ref, vl_ref, j * tk, tk) for j in range(seq // tk)] + [(kc_ref, vc_ref, 0, ctx_len)]

    def scores(t, blk):
        k_ref, _, start, size = blocks[blk]
        rows = slice(t * tq, (t + 1) * tq)
        keys = slice(start, start + size)
        return jnp.concatenate([_dot_nt(q_ref[rows, 0:dh], k_ref[keys, 0:dh]),
                                _dot_nt(q_ref[rows, dh:2 * dh], k_ref[keys, dh:2 * dh])], axis=0)

    items = [(t, blk) for t in range(seq // tq) for blk in range(len(blocks))]
    s = scores(*items[0])
    m = l = acc = None
    for n, (t, blk) in enumerate(items):
        s_next = scores(*items[n + 1]) if n + 1 < len(items) else None
        _, v_ref, start, size = blocks[blk]
        mx = jnp.max(s, axis=-1, keepdims=True)
        m_new = mx if blk == 0 else jnp.maximum(m, mx)
        p = jnp.exp2(s - m_new)
        ps = jnp.sum(p, axis=-1, keepdims=True)
        pv = jnp.dot(p.astype(BF16), v_ref[start:start + size, :], preferred_element_type=F32)
        if blk == 0:
            l, acc = ps, pv
        else:
            alpha = jnp.exp2(m - m_new)
            l = alpha * l + ps
            acc = alpha * acc + pv
        m = m_new
        if blk == len(blocks) - 1:
            a = acc * (1.0 / l)
            o = a[0:tq] - lam * a[tq:2 * tq]
            scale = lax.rsqrt(jnp.mean(o * o, axis=-1, keepdims=True) + NORM_EPS)
            o_ref[t * tq:(t + 1) * tq, :] = (o * scale * gs_ref[...] * (1.0 - lam_init)).astype(o_ref.dtype)
        s = s_next


def _attn(p_lat, p_ctx, lam_qk, g_sub, nb, d, lam_init):
    seq = p_lat.shape[0] // nb
    ctx_len = p_ctx.shape[0] // nb
    h = DA_HEADS
    hd = d // h
    tq = _tile(seq, ATTN_TQ)
    tk = _tile(seq, ATTN_TK)
    vmem = 2 * (4 * seq + 2 * ctx_len) * hd * 2 + 48 * tq * tk * 4 + 32 * tq * hd * 4 + 8 * 1024 * 1024
    return pl.pallas_call(
        functools.partial(_attn_kernel, lam_init=lam_init),
        grid=(nb, h),
        in_specs=[
            pl.BlockSpec((seq, hd), lambda b, i: (b, i)),
            pl.BlockSpec((seq, hd), lambda b, i: (b, h + i)),
            pl.BlockSpec((seq, hd), lambda b, i: (b, 2 * h + i)),
            pl.BlockSpec((ctx_len, hd), lambda b, i: (b, i)),
            pl.BlockSpec((ctx_len, hd), lambda b, i: (b, h + i)),
            pl.BlockSpec(lam_qk.shape, lambda b, i: (0, 0)),
            pl.BlockSpec((1, hd), lambda b, i: (0, 0)),
        ],
        out_specs=pl.BlockSpec((seq, hd), lambda b, i: (b, i)),
        out_shape=jax.ShapeDtypeStruct((nb * seq, d), BF16),
        compiler_params=_params(2, vmem),
        name="attn",
    )(p_lat, p_lat, p_lat, p_ctx, p_ctx, lam_qk, g_sub)


def _rope_tables(seq, dh):
    nf = dh // 4
    n = jnp.arange(seq)
    row = (n // GRID_W).astype(F32)
    col = (n % GRID_W).astype(F32)
    freq = ROPE_BASE ** (-jnp.arange(nf, dtype=F32) / nf)
    ar = row[:, None] * freq
    ac = col[:, None] * freq
    cos = jnp.concatenate([jnp.cos(ar), jnp.cos(ar), jnp.cos(ac), jnp.cos(ac)], axis=1)
    sin = jnp.concatenate([-jnp.sin(ar), jnp.sin(ar), -jnp.sin(ac), jnp.sin(ac)], axis=1)
    return cos, sin


def kernel(x, c, ctx, c_ctx, ada_w, ada_b, norm_g, mlp_w1, mlp_w2, ml_w_in, ml_b_gates, ml_head_g,
           ml_w_out, da_w_in, da_lambda, da_sub_g, da_w_out):
    nb, seq, d = x.shape
    ctx_len = ctx.shape[1]
    depth = ada_w.shape[0]
    assert depth == 2 and nb < COND_ROWS and d // ML_HEADS == 2 * LANES
    assert seq % LANES == 0 and ctx_len % LANES == 0 and seq % GRID_W == 0

    xl = x.reshape(nb * seq, d)
    xc = ctx.reshape(nb * ctx_len, d)

    cond = jnp.zeros((COND_ROWS, d), F32).at[:nb].set(c).at[nb].set(c_ctx)
    mods = _ada(cond, ada_w, ada_b).reshape(depth, COND_ROWS, 1, N_MOD * d)

    tm = _tile(seq, 512)
    tm_mlp = _tile(seq, 1024)
    tmc = _tile(nb * ctx_len, 512)

    def lat_row_of(tile_rows):
        return lambda i: i // (seq // tile_rows)

    lat_row = lat_row_of(tm)
    lat_row_mlp = lat_row_of(tm_mlp)

    def ctx_row(i):
        return nb

    g = norm_g.reshape(depth, 4, 1, d)

    ml_qk = ML_HEADS * (d // ML_HEADS // 2)
    n_main = 2 * ml_qk + 2 * d
    w_in = ml_w_in[0]
    q_scale = (d // ML_HEADS // 2) ** -0.5
    w_k = w_in[:, ml_qk:2 * ml_qk].astype(BF16)
    w_feat_t = jnp.concatenate([w_in[:, :ml_qk] * q_scale, w_in[:, 2 * ml_qk:n_main]], axis=1).T.astype(BF16)
    n_gate = 4 * ML_HEADS
    w_gate = jnp.pad(w_in[:, n_main:], ((0, 0), (0, LANES - n_gate))).astype(BF16)
    b_gate = jnp.pad(ml_b_gates[0], (0, LANES - n_gate)).reshape(1, LANES)
    w_out0 = ml_w_out[0].astype(BF16)
    w1_0 = mlp_w1[0].astype(BF16)
    w2_0 = mlp_w2[0].astype(BF16)

    k_lat, ft_lat, g_lat = _proj(xl, mods, 0, lat_row_mlp, g[0, 0], w_k, w_feat_t, tm=tm_mlp, tn=1024,
                                 gates=(w_gate, b_gate))
    k_ctx, ft_ctx, g_ctx = _proj(xc, mods, 0, ctx_row, g[0, 0], w_k, w_feat_t, tm=tmc, tn=1024,
                                 gates=(w_gate, b_gate))
    y_ctx, y_lat = _mlstm(k_ctx, ft_ctx, g_ctx, k_lat, ft_lat, g_lat, ml_head_g[0], nb, d)
    xl = _outproj(y_lat, w_out0, xl, mods, 0, lat_row, g[0, 1], tm=tm)
    xc = _outproj(y_ctx, w_out0, xc, mods, 0, ctx_row, g[0, 1], tm=tmc)
    xl = _mlp(xl, mods, 0, lat_row_mlp, g[0, 2], g[0, 3], w1_0, w2_0, tm=tm_mlp, tf=512)
    xc = _mlp(xc, mods, 0, ctx_row, g[0, 2], g[0, 3], w1_0, w2_0, tm=tmc, tf=512)

    dh = d // (2 * DA_HEADS)
    w_in = da_w_in[0]
    w_qkv = jnp.concatenate([w_in[:, :d] * (dh ** -0.5 * LOG2E), w_in[:, d:]], axis=1).astype(BF16)
    w_kv = w_in[:, d:].astype(BF16)
    w_out1 = da_w_out[0].astype(BF16)
    w1_1 = mlp_w1[1].astype(BF16)
    w2_1 = mlp_w2[1].astype(BF16)
    layer_idx = 1
    lam_init = 0.8 - 0.6 * math.exp(-0.3 * layer_idx)

    p_lat = _proj(xl, mods, 1, lat_row_mlp, g[1, 0], w_qkv, tm=tm_mlp, tn=1024,
                  rope=_rope_tables(seq, dh), rope_cols=2 * d)
    p_ctx = _proj(xc, mods, 1, ctx_row, g[1, 0], w_kv, tm=tmc, tn=1024)
    y_lat = _attn(p_lat, p_ctx, da_lambda[0], da_sub_g[0].reshape(1, 2 * dh), nb, d, lam_init)
    xl = _outproj(y_lat, w_out1, xl, mods, 1, lat_row, g[1, 1], tm=tm)
    xl = _mlp(xl, mods, 1, lat_row_mlp, g[1, 2], g[1, 3], w1_1, w2_1, tm=tm_mlp, tf=512)
    return xl.reshape(nb, seq, d)
```

```python
import functools
import math

import jax
import jax.numpy as jnp
from jax import lax
from jax.experimental import pallas as pl
from jax.experimental.pallas import tpu as pltpu

F32 = jnp.float32
BF16 = jnp.bfloat16

NORM_EPS = 1e-6
GRID_W = 64
ROPE_BASE = 10000.0
LOG2E = math.log2(math.e)
ML_HEADS = 8
ML_BLOCK = 256
ML_HB = 2
ML_AUG = 16
DA_HEADS = 8
EDGE_ROWS = 256
ATTN_TQ = 512
ATTN_TK = 512
N_MOD = 6
COND_ROWS = 32
LANES = 128
MXU_COLS = 256
GATE_ROWS = 8

VMEM_CAP_BYTES = 56 * 1024 * 1024


def _vmem_limit(nbytes):
    return int(min(VMEM_CAP_BYTES, max(16 * 1024 * 1024, nbytes)))


def _params(n_grid, vmem_bytes):
    return pltpu.CompilerParams(
        dimension_semantics=("arbitrary",) * n_grid,
        vmem_limit_bytes=_vmem_limit(vmem_bytes))


def _tile(n, want):
    if n <= want:
        return n
    t = want
    while t >= 8:
        if n % t == 0 and t % 8 == 0:
            return t
        t -= 8
    return n


def _dot_nt(a, b):
    return lax.dot_general(a, b, (((1,), (1,)), ((), ())), preferred_element_type=F32)


def _ada_kernel(c_ref, w_ref, b_ref, o_ref):
    cf = c_ref[...]
    s = (cf * jax.nn.sigmoid(cf)).astype(BF16)
    o_ref[...] = jnp.dot(s, w_ref[...].astype(BF16), preferred_element_type=F32) + b_ref[...]


def _ada(cond, ada_w, ada_b):
    depth, d, n = ada_w.shape
    tn = _tile(n, 1024)
    return pl.pallas_call(
        _ada_kernel,
        grid=(depth, n // tn),
        in_specs=[
            pl.BlockSpec((COND_ROWS, d), lambda l, j: (0, 0)),
            pl.BlockSpec((None, d, tn), lambda l, j: (l, 0, j)),
            pl.BlockSpec((None, 1, tn), lambda l, j: (l, 0, j)),
        ],
        out_specs=pl.BlockSpec((None, COND_ROWS, tn), lambda l, j: (l, 0, j)),
        out_shape=jax.ShapeDtypeStruct((depth, COND_ROWS, n), F32),
        compiler_params=_params(2, 2 * d * tn * 4 + 4 * COND_ROWS * (d + tn) * 4 + d * tn * 2),
        name="ada",
    )(cond, ada_w, ada_b.reshape(depth, 1, n))


def _mod_spec(layer, piece, row_fn, d, n_grid):
    if n_grid == 1:
        return pl.BlockSpec((None, None, 1, d), lambda i: (layer, row_fn(i), 0, piece))
    return pl.BlockSpec((None, None, 1, d), lambda i, j: (layer, row_fn(i), 0, piece))


def _modulated(x_ref, g_ref, sh_ref, sc_ref):
    xf = x_ref[...]
    ms = jnp.mean(xf * xf, axis=-1, keepdims=True)
    gain = g_ref[...] * (1.0 + sc_ref[...])
    return (xf * lax.rsqrt(ms + NORM_EPS) * gain + sh_ref[...]).astype(BF16)


def _swap_halves_32(a, first_half):
    fwd = pltpu.roll(a, 3 * LANES // 4, axis=1)
    bwd = pltpu.roll(a, LANES // 4, axis=1)
    return jnp.where(first_half, fwd, bwd)


def _proj_kernel(*refs, n_tok, n_rope_tok, with_feat, with_gates):
    x_ref, sh_ref, sc_ref, g_ref, wk_ref = refs[:5]
    n = 5
    if with_feat:
        wt_ref = refs[n]
        n += 1
    if with_gates:
        wg_ref, bg_ref = refs[n:n + 2]
        n += 2
    if n_rope_tok:
        cos_ref, sin_ref = refs[n:n + 2]
        n += 2
    k_ref = refs[n]
    n += 1
    if with_feat:
        ft_ref = refs[n]
        n += 1
    if with_gates:
        gate_ref = refs[n]
    h_scr = refs[-1]
    j = pl.program_id(1)
    tm = x_ref.shape[0]
    sub = _tile(tm, EDGE_ROWS)

    def token_tile(rows, rotate):
        if not rotate:
            k_ref[rows, :] = jnp.dot(h_scr[rows, :], wk_ref[...], preferred_element_type=F32).astype(k_ref.dtype)
            return
        cos = cos_ref[rows, :]
        sin = sin_ref[rows, :]
        lane = lax.broadcasted_iota(jnp.int32, cos.shape, 1)
        first_half = (lane % (LANES // 2)) < (LANES // 4)
        for c in range(wk_ref.shape[1] // MXU_COLS):
            acc = jnp.dot(h_scr[rows, :], wk_ref[:, c * MXU_COLS:(c + 1) * MXU_COLS],
                          preferred_element_type=F32)
            for half in range(MXU_COLS // LANES):
                a = acc[:, half * LANES:(half + 1) * LANES]
                r = a * cos + _swap_halves_32(a, first_half) * sin
                lo = c * MXU_COLS + half * LANES
                k_ref[rows, lo:lo + LANES] = r.astype(k_ref.dtype)

    @pl.when(j == 0)
    def _():
        for r0 in range(0, tm, sub):
            rows = slice(r0, r0 + sub)
            h_scr[rows, :] = _modulated(x_ref.at[rows, :], g_ref, sh_ref, sc_ref)
            if with_gates:
                gate_ref[rows, :] = jnp.dot(h_scr[rows, :], wg_ref[...], preferred_element_type=F32) + bg_ref[...]
            token_tile(rows, n_rope_tok > 0)

    if n_rope_tok > 1:
        @pl.when(jnp.logical_and(j > 0, j < n_rope_tok))
        def _():
            token_tile(slice(0, tm), True)

    if n_tok > max(n_rope_tok, 1):
        @pl.when(jnp.logical_and(j >= max(n_rope_tok, 1), j < n_tok))
        def _():
            token_tile(slice(0, tm), False)

    if with_feat:
        @pl.when(j >= n_tok)
        def _():
            ft_ref[...] = _dot_nt(wt_ref[...], h_scr[...]).astype(ft_ref.dtype)


def _proj(x, mods, layer, row_fn, g, wk, wt=None, *, tm, tn, gates=None, rope=None, rope_cols=0):
    m, d = x.shape
    nk = wk.shape[1]
    tm = _tile(m, tm)
    tnk = _tile(nk, tn)
    n_tok = nk // tnk
    assert rope_cols % tnk == 0

    def tok_tile(j):
        return jnp.minimum(j, n_tok - 1)

    def feat_tile(j):
        return jnp.maximum(j - n_tok, 0)

    in_specs = [
        pl.BlockSpec((tm, d), lambda i, j: (i, 0)),
        _mod_spec(layer, 0, row_fn, d, 2),
        _mod_spec(layer, 1, row_fn, d, 2),
        pl.BlockSpec((1, d), lambda i, j: (0, 0)),
        pl.BlockSpec((d, tnk), lambda i, j: (0, tok_tile(j))),
    ]
    args = [x, mods, mods, g, wk]
    out_specs = [pl.BlockSpec((tm, tnk), lambda i, j: (i, tok_tile(j)))]
    out_shape = [jax.ShapeDtypeStruct((m, nk), BF16)]
    n_feat = tnf = 0
    if wt is not None:
        nf = wt.shape[0]
        tnf = _tile(nf, tn)
        n_feat = nf // tnf
        in_specs.append(pl.BlockSpec((tnf, d), lambda i, j: (feat_tile(j), 0)))
        args.append(wt)
        out_specs.append(pl.BlockSpec((tnf, tm), lambda i, j: (feat_tile(j), i)))
        out_shape.append(jax.ShapeDtypeStruct((nf, m), BF16))
    if gates is not None:
        in_specs += [pl.BlockSpec((d, LANES), lambda i, j: (0, 0)),
                     pl.BlockSpec((1, LANES), lambda i, j: (0, 0))]
        args += list(gates)
        out_specs.append(pl.BlockSpec((tm, LANES), lambda i, j: (i, 0)))
        out_shape.append(jax.ShapeDtypeStruct((m, LANES), F32))
    if rope is not None:
        n_pos_tiles = rope[0].shape[0] // tm
        assert rope[0].shape[0] % tm == 0
        in_specs += [pl.BlockSpec((tm, LANES), lambda i, j: (i % n_pos_tiles, 0)),
                     pl.BlockSpec((tm, LANES), lambda i, j: (i % n_pos_tiles, 0))]
        args += list(rope)
    vmem = 2 * tm * d * 4 + tm * d * 2 + 2 * d * tnk * 2 + 2 * tnf * d * 2 + 2 * tm * tnk * 2 + 2 * tnf * tm * 2 \
        + 2 * tm * max(tnk, tnf) * 4 + 8 * tm * LANES * 4 + 2 * d * LANES * 2 + 3 * tm * d * 4
    out = pl.pallas_call(
        functools.partial(_proj_kernel, n_tok=n_tok, n_rope_tok=rope_cols // tnk if rope is not None else 0,
                          with_feat=wt is not None, with_gates=gates is not None),
        grid=(m // tm, n_tok + n_feat),
        in_specs=in_specs,
        out_specs=out_specs,
        out_shape=out_shape,
        scratch_shapes=[pltpu.VMEM((tm, d), BF16)],
        compiler_params=_params(2, vmem),
        name="proj",
    )(*args)
    return out if len(out) > 1 else out[0]


def _outproj_kernel(y_ref, w_ref, x_ref, gate_ref, g_ref, o_ref):
    t = jnp.dot(y_ref[...], w_ref[...], preferred_element_type=F32)
    ms = jnp.mean(t * t, axis=-1, keepdims=True)
    o_ref[...] = x_ref[...] + gate_ref[...] * (t * lax.rsqrt(ms + NORM_EPS) * g_ref[...])


def _outproj(y, w, x, mods, layer, row_fn, g, *, tm):
    m, d = x.shape
    kdim = y.shape[1]
    tm = _tile(m, tm)
    vmem = 2 * tm * kdim * 2 + 2 * kdim * d * 2 + 4 * tm * d * 4 + 3 * tm * d * 4
    return pl.pallas_call(
        _outproj_kernel,
        grid=(m // tm,),
        in_specs=[
            pl.BlockSpec((tm, kdim), lambda i: (i, 0)),
            pl.BlockSpec((kdim, d), lambda i: (0, 0)),
            pl.BlockSpec((tm, d), lambda i: (i, 0)),
            _mod_spec(layer, 2, row_fn, d, 1),
            pl.BlockSpec((1, d), lambda i: (0, 0)),
        ],
        out_specs=pl.BlockSpec((tm, d), lambda i: (i, 0)),
        out_shape=jax.ShapeDtypeStruct((m, d), F32),
        compiler_params=_params(1, vmem),
        name="outproj",
    )(y, w, x, mods, g)


def _mlp_kernel(x_ref, sh_ref, sc_ref, gate_ref, g_in_ref, g_out_ref, w1_ref, w2_ref, o_ref, h_scr):
    j = pl.program_id(1)
    last = pl.num_programs(1) - 1
    tm = o_ref.shape[0]
    sub = _tile(tm, EDGE_ROWS)

    def hidden(h):
        a = jnp.dot(h, w1_ref[...], preferred_element_type=F32)
        a = jnp.maximum(a, 0.0)
        return jnp.dot((a * a).astype(BF16), w2_ref[...], preferred_element_type=F32)

    @pl.when(j == 0)
    def _():
        for r0 in range(0, tm, sub):
            rows = slice(r0, r0 + sub)
            h_scr[rows, :] = _modulated(x_ref.at[rows, :], g_in_ref, sh_ref, sc_ref)
            o_ref[rows, :] = hidden(h_scr[rows, :])

    @pl.when(jnp.logical_and(j > 0, j < last))
    def _():
        o_ref[...] += hidden(h_scr[...])

    @pl.when(j == last)
    def _():
        for r0 in range(0, tm, sub):
            rows = slice(r0, r0 + sub)
            f = o_ref[rows, :] + hidden(h_scr[rows, :])
            ms = jnp.mean(f * f, axis=-1, keepdims=True)
            o_ref[rows, :] = x_ref[rows, :] + gate_ref[...] * (f * lax.rsqrt(ms + NORM_EPS) * g_out_ref[...])


def _mlp(x, mods, layer, row_fn, g_in, g_out, w1, w2, *, tm, tf):
    m, d = x.shape
    dff = w1.shape[1]
    tm = _tile(m, tm)
    tf = _tile(dff, tf)
    vmem = 4 * tm * d * 4 + tm * d * 2 + 4 * d * tf * 2 + 3 * tm * tf * 4 + 4 * 1024 * 1024
    return pl.pallas_call(
        _mlp_kernel,
        grid=(m // tm, dff // tf),
        in_specs=[
            pl.BlockSpec((tm, d), lambda i, j: (i, 0)),
            _mod_spec(layer, 3, row_fn, d, 2),
            _mod_spec(layer, 4, row_fn, d, 2),
            _mod_spec(layer, 5, row_fn, d, 2),
            pl.BlockSpec((1, d), lambda i, j: (0, 0)),
            pl.BlockSpec((1, d), lambda i, j: (0, 0)),
            pl.BlockSpec((d, tf), lambda i, j: (0, j)),
            pl.BlockSpec((tf, d), lambda i, j: (j, 0)),
        ],
        out_specs=pl.BlockSpec((tm, d), lambda i, j: (i, 0)),
        out_shape=jax.ShapeDtypeStruct((m, d), F32),
        scratch_shapes=[pltpu.VMEM((tm, d), BF16)],
        compiler_params=_params(2, vmem),
        name="mlp",
    )(x, mods, mods, mods, g_in, g_out, w1, w2)


def _split3(x):
    x1 = x.astype(BF16)
    r1 = x - x1.astype(F32)
    x2 = r1.astype(BF16)
    x3 = (r1 - x2.astype(F32)).astype(BF16)
    return x1, x2, x3


def _log_sigmoid(x):
    return jnp.minimum(x, 0.0) - jnp.log(1.0 + jnp.exp(-jnp.abs(x)))


def _running_max_lanes(x, reverse):
    lane = lax.broadcasted_iota(jnp.int32, (x.shape[0], LANES), 1)
    slabs = [x[:, j:j + LANES] for j in range(0, x.shape[1], LANES)]
    order = range(len(slabs) - 1, -1, -1) if reverse else range(len(slabs))
    carry = None
    for j in order:
        y = slabs[j]
        k = 1
        while k < LANES:
            if reverse:
                y = jnp.maximum(y, jnp.where(lane < LANES - k, pltpu.roll(y, LANES - k, axis=1), -jnp.inf))
            else:
                y = jnp.maximum(y, jnp.where(lane >= k, pltpu.roll(y, k, axis=1), -jnp.inf))
            k *= 2
        if carry is not None:
            y = jnp.maximum(y, carry)
        carry = jnp.max(y, axis=-1, keepdims=True)
        slabs[j] = y
    return jnp.concatenate(slabs, axis=1)


def _gate_scan_kernel(x_ref, o_ref):
    L = x_ref.shape[-1]
    ui = lax.broadcasted_iota(jnp.int32, (L, L), 0)
    si = lax.broadcasted_iota(jnp.int32, (L, L), 1)

    def cumulative(z, tri):
        return sum(jnp.dot(p, tri.astype(BF16), preferred_element_type=F32) for p in _split3(z))

    b_f = cumulative(_log_sigmoid(x_ref[1]), ui <= si) * LOG2E
    b_b = cumulative(_log_sigmoid(x_ref[3]), ui >= si) * LOG2E
    c_f = x_ref[0] * LOG2E - b_f
    c_b = x_ref[2] * LOG2E - b_b
    o_ref[0] = b_f
    o_ref[1] = c_f
    o_ref[2] = _running_max_lanes(c_f, False)
    o_ref[3] = b_b
    o_ref[4] = c_b
    o_ref[5] = _running_max_lanes(c_b, True)


def _gate_scan(gates, nb, t, L):
    nc = t // L
    g = gates[:, :4 * ML_HEADS].reshape(nb, nc, L, 4, ML_HEADS)
    g = g.transpose(3, 0, 4, 1, 2).reshape(4, nb * ML_HEADS * nc, L)
    nr = g.shape[1]
    tr = _tile(nr, 256)
    n_out = 6
    s = pl.pallas_call(
        _gate_scan_kernel,
        grid=(nr // tr,),
        in_specs=[pl.BlockSpec((4, tr, L), lambda i: (0, i, 0))],
        out_specs=pl.BlockSpec((n_out, tr, L), lambda i: (0, i, 0)),
        out_shape=jax.ShapeDtypeStruct((n_out, nr, L), F32),
        compiler_params=_params(1, 48 * tr * max(L, LANES) * 4),
        name="gate_scan",
    )(g)
    s = s.reshape(n_out, nb, ML_HEADS, nc, L)
    rows = jnp.pad(s.transpose(1, 2, 3, 0, 4), ((0, 0), (0, 0), (0, 0), (0, GATE_ROWS - n_out), (0, 0)))
    return rows


def _mlstm_kernel(kc_ref, qtc_ref, vtc_ref, ogtc_ref, grc_ref,
                  kl_ref, qtl_ref, vtl_ref, ogtl_ref, grl_ref, hg_ref,
                  oc_ref, ol_ref, hf_c, hb_c, hf_l, hb_l):
    L = grc_ref.shape[-1]
    dk = kc_ref.shape[1] // ML_HB
    dv = vtc_ref.shape[0] // ML_HB
    ncc = kc_ref.shape[0] // L
    ncl = kl_ref.shape[0] // L

    si = lax.broadcasted_iota(jnp.int32, (L, L), 0)
    ti = lax.broadcasted_iota(jnp.int32, (L, L), 1)
    ones_rows = (lax.broadcasted_iota(jnp.int32, (ML_AUG, L), 0) == 0).astype(BF16)

    gate_tiles = {}
    for gr_ref, nc in ((grc_ref, ncc), (grl_ref, ncl)):
        for hh in range(ML_HB):
            for c in range(nc):
                tile = gr_ref[hh, c]
                gate_tiles[(id(gr_ref), hh, c)] = (tile, tile.T)

    def chunk(refs, c, hh, backward, state):
        k_ref, qt_ref, vt_ref, gr_ref, hs_ref = refs
        caug, m = state
        tok = slice(c * L, (c + 1) * L)
        kb = k_ref[tok, hh * dk:(hh + 1) * dk]
        qt = qt_ref[hh * dk:(hh + 1) * dk, tok]
        vt_aug = jnp.concatenate([vt_ref[hh * dv:(hh + 1) * dv, tok], ones_rows], axis=0)
        rows, cols = gate_tiles[(id(gr_ref), hh, c)]
        r = 3 if backward else 0
        b_row = rows[r:r + 1, :]
        c_row = rows[r + 1:r + 2, :]
        c_max = rows[r + 2:r + 3, :]
        c_col = cols[:, r + 1:r + 2]
        b_end = b_row[:, 0:1] if backward else b_row[:, L - 1:L]

        inter = b_row + m
        m_row = jnp.maximum(inter, b_row + c_max)
        mask = (si >= ti) if backward else (si <= ti)
        decay = jnp.where(mask, jnp.exp2(c_col + (b_row - m_row)), 0.0)
        w_row = jnp.exp2(inter - m_row)

        p_t = (jnp.dot(kb, qt, preferred_element_type=F32) * decay).astype(BF16)
        lhs = jnp.concatenate([caug.astype(BF16), vt_aug], axis=1)
        rhs = jnp.concatenate([qt * w_row.astype(BF16), p_t], axis=0)
        out = jnp.dot(lhs, rhs, preferred_element_type=F32)
        den = out[dv:dv + 1, :]
        hs_ref[hh * dv:(hh + 1) * dv, tok] = out[0:dv, :] * (1.0 / jnp.maximum(jnp.abs(den), jnp.exp2(-m_row)))

        g_row = b_end + c_row
        m_new = jnp.maximum(b_end + m, jnp.max(g_row, axis=-1, keepdims=True))
        a = jnp.exp2(b_end + m - m_new)
        ws = jnp.exp2(g_row - m_new)
        caug_new = a * caug + jnp.dot(vt_aug * ws.astype(BF16), kb, preferred_element_type=F32)
        return caug_new, m_new

    def scan_segment(nc, k_ref, qt_ref, vt_ref, gr_ref, hf_ref, hb_ref, states):
        fwd_refs = (k_ref, qt_ref, vt_ref, gr_ref, hf_ref)
        bwd_refs = (k_ref, qt_ref, vt_ref, gr_ref, hb_ref)
        for step in range(nc):
            nxt = []
            for hh in range(ML_HB):
                nxt.append(chunk(fwd_refs, step, hh, False, states[2 * hh]))
                nxt.append(chunk(bwd_refs, nc - 1 - step, hh, True, states[2 * hh + 1]))
            states = nxt
        return states

    zero = (jnp.zeros((dv + ML_AUG, dk), F32), jnp.zeros((1, 1), F32))
    st = scan_segment(ncc, kc_ref, qtc_ref, vtc_ref, grc_ref, hf_c, hb_c, [zero] * (2 * ML_HB))
    scan_segment(ncl, kl_ref, qtl_ref, vtl_ref, grl_ref, hf_l, hb_l, st)

    def finish_segment(nc, hf_ref, hb_ref, ogt_ref, o_ref):
        for c in range(nc):
            tok = slice(c * L, (c + 1) * L)
            for hh in range(ML_HB):
                feat = slice(hh * dv, (hh + 1) * dv)
                hs = hf_ref[feat, tok] + hb_ref[feat, tok]
                scale = lax.rsqrt(jnp.mean(hs * hs, axis=0, keepdims=True) + NORM_EPS)
                head_g = jnp.concatenate([hg_ref[feat, :]] * (L // LANES), axis=1)
                y_t = hs * scale * head_g * jax.nn.sigmoid(ogt_ref[feat, tok].astype(F32))
                o_ref[tok, feat] = y_t.T.astype(o_ref.dtype)

    finish_segment(ncc, hf_c, hb_c, ogtc_ref, oc_ref)
    finish_segment(ncl, hf_l, hb_l, ogtl_ref, ol_ref)


def _mlstm(k_ctx, ft_ctx, g_ctx, k_lat, ft_lat, g_lat, head_g, nb, d):
    ctx_len = k_ctx.shape[0] // nb
    seq = k_lat.shape[0] // nb
    dv = ML_HB * (d // ML_HEADS)
    dk = dv // 2
    nh = ML_HEADS // ML_HB
    q_rows = ML_HEADS * (d // ML_HEADS // 2)
    v_blk = q_rows // dv
    og_blk = (q_rows + d) // dv
    L = math.gcd(math.gcd(ctx_len, seq), ML_BLOCK)
    assert L % LANES == 0
    grc = _gate_scan(g_ctx, nb, ctx_len, L)
    grl = _gate_scan(g_lat, nb, seq, L)
    hg = jnp.broadcast_to(head_g.reshape(d, 1), (d, LANES))

    def stream_specs(t):
        return [
            pl.BlockSpec((t, dk), lambda b, i: (b, i)),
            pl.BlockSpec((dk, t), lambda b, i: (i, b)),
            pl.BlockSpec((dv, t), lambda b, i: (v_blk + i, b)),
            pl.BlockSpec((dv, t), lambda b, i: (og_blk + i, b)),
            pl.BlockSpec((None, ML_HB, t // L, GATE_ROWS, L), lambda b, i: (b, i, 0, 0, 0)),
        ]

    t_all = ctx_len + seq
    gate_tiles = 2 * ML_HB * (t_all // L) * GATE_ROWS * L * 4
    vmem = 2 * t_all * (2 * dk * 2 + 3 * dv * 2) + 2 * t_all * dv * 4 + gate_tiles + 12 * 1024 * 1024
    return pl.pallas_call(
        _mlstm_kernel,
        grid=(nb, nh),
        in_specs=stream_specs(ctx_len) + stream_specs(seq) + [pl.BlockSpec((dv, LANES), lambda b, i: (i, 0))],
        out_specs=[pl.BlockSpec((ctx_len, dv), lambda b, i: (b, i)),
                   pl.BlockSpec((seq, dv), lambda b, i: (b, i))],
        out_shape=[jax.ShapeDtypeStruct((nb * ctx_len, d), BF16),
                   jax.ShapeDtypeStruct((nb * seq, d), BF16)],
        scratch_shapes=[pltpu.VMEM((dv, ctx_len), F32), pltpu.VMEM((dv, ctx_len), F32),
                        pltpu.VMEM((dv, seq), F32), pltpu.VMEM((dv, seq), F32)],
        compiler_params=_params(2, vmem),
        name="mlstm",
    )(k_ctx, ft_ctx, ft_ctx, ft_ctx, grc, k_lat, ft_lat, ft_lat, ft_lat, grl, hg)


def _attn_kernel(q_ref, kl_ref, vl_ref, kc_ref, vc_ref, lam_ref, gs_ref, o_ref, *, lam_init):
    dh = q_ref.shape[1] // 2
    seq = q_ref.shape[0]
    ctx_len = kc_ref.shape[0]
    lq = lam_ref[...]
    lam = jnp.exp(jnp.sum(lq[0:1] * lq[1:2], axis=-1, keepdims=True)) \
        - jnp.exp(jnp.sum(lq[2:3] * lq[3:4], axis=-1, keepdims=True)) + lam_init

    tq = _tile(seq, ATTN_TQ)
    tk = _tile(seq, ATTN_TK)
    blocks = [(kl_ref, vl_ref, j * tk, tk) for j in range(seq // tk)] + [(kc_ref, vc_ref, 0, ctx_len)]

    def scores(t, blk):
        k_ref, _, start, size = blocks[blk]
        rows = slice(t * tq, (t + 1) * tq)
        keys = slice(start, start + size)
        return jnp.concatenate([_dot_nt(q_ref[rows, 0:dh], k_ref[keys, 0:dh]),
                                _dot_nt(q_ref[rows, dh:2 * dh], k_ref[keys, dh:2 * dh])], axis=0)

    items = [(t, blk) for t in range(seq // tq) for blk in range(len(blocks))]
    s = scores(*items[0])
    m = l = acc = None
    for n, (t, blk) in enumerate(items):
        s_next = scores(*items[n + 1]) if n + 1 < len(items) else None
        _, v_ref, start, size = blocks[blk]
        mx = jnp.max(s, axis=-1, keepdims=True)
        m_new = mx if blk == 0 else jnp.maximum(m, mx)
        p = jnp.exp2(s - m_new)
        ps = jnp.sum(p, axis=-1, keepdims=True)
        pv = jnp.dot(p.astype(BF16), v_ref[start:start + size, :], preferred_element_type=F32)
        if blk == 0:
            l, acc = ps, pv
        else:
            alpha = jnp.exp2(m - m_new)
            l = alpha * l + ps
            acc = alpha * acc + pv
        m = m_new
        if blk == len(blocks) - 1:
            a = acc * (1.0 / l)
            o = a[0:tq] - lam * a[tq:2 * tq]
            scale = lax.rsqrt(jnp.mean(o * o, axis=-1, keepdims=True) + NORM_EPS)
            o_ref[t * tq:(t + 1) * tq, :] = (o * scale * gs_ref[...] * (1.0 - lam_init)).astype(o_ref.dtype)
        s = s_next


def _attn(p_lat, p_ctx, lam_qk, g_sub, nb, d, lam_init):
    seq = p_lat.shape[0] // nb
    ctx_len = p_ctx.shape[0] // nb
    h = DA_HEADS
    hd = d // h
    tq = _tile(seq, ATTN_TQ)
    tk = _tile(seq, ATTN_TK)
    vmem = 2 * (4 * seq + 2 * ctx_len) * hd * 2 + 48 * tq * tk * 4 + 32 * tq * hd * 4 + 8 * 1024 * 1024
    return pl.pallas_call(
        functools.partial(_attn_kernel, lam_init=lam_init),
        grid=(nb, h),
        in_specs=[
            pl.BlockSpec((seq, hd), lambda b, i: (b, i)),
            pl.BlockSpec((seq, hd), lambda b, i: (b, h + i)),
            pl.BlockSpec((seq, hd), lambda b, i: (b, 2 * h + i)),
            pl.BlockSpec((ctx_len, hd), lambda b, i: (b, i)),
            pl.BlockSpec((ctx_len, hd), lambda b, i: (b, h + i)),
            pl.BlockSpec(lam_qk.shape, lambda b, i: (0, 0)),
            pl.BlockSpec((1, hd), lambda b, i: (0, 0)),
        ],
        out_specs=pl.BlockSpec((seq, hd), lambda b, i: (b, i)),
        out_shape=jax.ShapeDtypeStruct((nb * seq, d), BF16),
        compiler_params=_params(2, vmem),
        name="attn",
    )(p_lat, p_lat, p_lat, p_ctx, p_ctx, lam_qk, g_sub)


def _rope_tables(seq, dh):
    nf = dh // 4
    n = jnp.arange(seq)
    row = (n // GRID_W).astype(F32)
    col = (n % GRID_W).astype(F32)
    freq = ROPE_BASE ** (-jnp.arange(nf, dtype=F32) / nf)
    ar = row[:, None] * freq
    ac = col[:, None] * freq
    cos = jnp.concatenate([jnp.cos(ar), jnp.cos(ar), jnp.cos(ac), jnp.cos(ac)], axis=1)
    sin = jnp.concatenate([-jnp.sin(ar), jnp.sin(ar), -jnp.sin(ac), jnp.sin(ac)], axis=1)
    return cos, sin


def kernel(x, c, ctx, c_ctx, ada_w, ada_b, norm_g, mlp_w1, mlp_w2, ml_w_in, ml_b_gates, ml_head_g,
           ml_w_out, da_w_in, da_lambda, da_sub_g, da_w_out):
    nb, seq, d = x.shape
    ctx_len = ctx.shape[1]
    depth = ada_w.shape[0]
    assert depth == 2 and nb < COND_ROWS and d // ML_HEADS == 2 * LANES
    assert seq % LANES == 0 and ctx_len % LANES == 0 and seq % GRID_W == 0

    xl = x.reshape(nb * seq, d)
    xc = ctx.reshape(nb * ctx_len, d)

    cond = jnp.zeros((COND_ROWS, d), F32).at[:nb].set(c).at[nb].set(c_ctx)
    mods = _ada(cond, ada_w, ada_b).reshape(depth, COND_ROWS, 1, N_MOD * d)

    tm = _tile(seq, 512)
    tm_mlp = _tile(seq, 1024)
    tmc = _tile(nb * ctx_len, 512)

    def lat_row_of(tile_rows):
        return lambda i: i // (seq // tile_rows)

    lat_row = lat_row_of(tm)
    lat_row_mlp = lat_row_of(tm_mlp)

    def ctx_row(i):
        return nb

    g = norm_g.reshape(depth, 4, 1, d)

    ml_qk = ML_HEADS * (d // ML_HEADS // 2)
    n_main = 2 * ml_qk + 2 * d
    w_in = ml_w_in[0]
    q_scale = (d // ML_HEADS // 2) ** -0.5
    w_k = w_in[:, ml_qk:2 * ml_qk].astype(BF16)
    w_feat_t = jnp.concatenate([w_in[:, :ml_qk] * q_scale, w_in[:, 2 * ml_qk:n_main]], axis=1).T.astype(BF16)
    n_gate = 4 * ML_HEADS
    w_gate = jnp.pad(w_in[:, n_main:], ((0, 0), (0, LANES - n_gate))).astype(BF16)
    b_gate = jnp.pad(ml_b_gates[0], (0, LANES - n_gate)).reshape(1, LANES)
    w_out0 = ml_w_out[0].astype(BF16)
    w1_0 = mlp_w1[0].astype(BF16)
    w2_0 = mlp_w2[0].astype(BF16)

    k_lat, ft_lat, g_lat = _proj(xl, mods, 0, lat_row_mlp, g[0, 0], w_k, w_feat_t, tm=tm_mlp, tn=1024,
                                 gates=(w_gate, b_gate))
    k_ctx, ft_ctx, g_ctx = _proj(xc, mods, 0, ctx_row, g[0, 0], w_k, w_feat_t, tm=tmc, tn=1024,
                                 gates=(w_gate, b_gate))
    y_ctx, y_lat = _mlstm(k_ctx, ft_ctx, g_ctx, k_lat, ft_lat, g_lat, ml_head_g[0], nb, d)
    xl = _outproj(y_lat, w_out0, xl, mods, 0, lat_row, g[0, 1], tm=tm)
    xc = _outproj(y_ctx, w_out0, xc, mods, 0, ctx_row, g[0, 1], tm=tmc)
    xl = _mlp(xl, mods, 0, lat_row_mlp, g[0, 2], g[0, 3], w1_0, w2_0, tm=tm_mlp, tf=512)
    xc = _mlp(xc, mods, 0, ctx_row, g[0, 2], g[0, 3], w1_0, w2_0, tm=tmc, tf=512)

    dh = d // (2 * DA_HEADS)
    w_in = da_w_in[0]
    w_qkv = jnp.concatenate([w_in[:, :d] * (dh ** -0.5 * LOG2E), w_in[:, d:]], axis=1).astype(BF16)
    w_kv = w_in[:, d:].astype(BF16)
    w_out1 = da_w_out[0].astype(BF16)
    w1_1 = mlp_w1[1].astype(BF16)
    w2_1 = mlp_w2[1].astype(BF16)
    layer_idx = 1
    lam_init = 0.8 - 0.6 * math.exp(-0.3 * layer_idx)

    p_lat = _proj(xl, mods, 1, lat_row_mlp, g[1, 0], w_qkv, tm=tm_mlp, tn=1024,
                  rope=_rope_tables(seq, dh), rope_cols=2 * d)
    p_ctx = _proj(xc, mods, 1, ctx_row, g[1, 0], w_kv, tm=tmc, tn=1024)
    y_lat = _attn(p_lat, p_ctx, da_lambda[0], da_sub_g[0].reshape(1, 2 * dh), nb, d, lam_init)
    xl = _outproj(y_lat, w_out1, xl, mods, 1, lat_row, g[1, 1], tm=tm)
    xl = _mlp(xl, mods, 1, lat_row_mlp, g[1, 2], g[1, 3], w1_1, w2_1, tm=tm_mlp, tf=512)
    return xl.reshape(nb, seq, d)
```

```python
import functools
import math

import jax
import jax.numpy as jnp
from jax import lax
from jax.experimental import pallas as pl
from jax.experimental.pallas import tpu as pltpu

F32 = jnp.float32
BF16 = jnp.bfloat16

NORM_EPS = 1e-6
GRID_W = 64
ROPE_BASE = 10000.0
LOG2E = math.log2(math.e)
ML_HEADS = 8
ML_BLOCK = 256
ML_HB = 2
ML_AUG = 16
DA_HEADS = 8
EDGE_ROWS = 256
ATTN_TQ = 256
ATTN_TK = 512
N_MOD = 6
COND_ROWS = 32
LANES = 128
MXU_COLS = 256
GATE_ROWS = 8

VMEM_CAP_BYTES = 56 * 1024 * 1024


def _vmem_limit(nbytes):
    return int(min(VMEM_CAP_BYTES, max(16 * 1024 * 1024, nbytes)))


def _params(n_grid, vmem_bytes):
    return pltpu.CompilerParams(
        dimension_semantics=("arbitrary",) * n_grid,
        vmem_limit_bytes=_vmem_limit(vmem_bytes))


def _tile(n, want):
    if n <= want:
        return n
    t = want
    while t >= 8:
        if n % t == 0 and t % 8 == 0:
            return t
        t -= 8
    return n


def _dot_nt(a, b):
    return lax.dot_general(a, b, (((1,), (1,)), ((), ())), preferred_element_type=F32)


def _ada_kernel(c_ref, w_ref, b_ref, o_ref):
    cf = c_ref[...]
    s = (cf * jax.nn.sigmoid(cf)).astype(BF16)
    o_ref[...] = jnp.dot(s, w_ref[...].astype(BF16), preferred_element_type=F32) + b_ref[...]


def _ada(cond, ada_w, ada_b):
    depth, d, n = ada_w.shape
    tn = _tile(n, 1024)
    return pl.pallas_call(
        _ada_kernel,
        grid=(depth, n // tn),
        in_specs=[
            pl.BlockSpec((COND_ROWS, d), lambda l, j: (0, 0)),
            pl.BlockSpec((None, d, tn), lambda l, j: (l, 0, j)),
            pl.BlockSpec((None, 1, tn), lambda l, j: (l, 0, j)),
        ],
        out_specs=pl.BlockSpec((None, COND_ROWS, tn), lambda l, j: (l, 0, j)),
        out_shape=jax.ShapeDtypeStruct((depth, COND_ROWS, n), F32),
        compiler_params=_params(2, 2 * d * tn * 4 + 4 * COND_ROWS * (d + tn) * 4 + d * tn * 2),
        name="ada",
    )(cond, ada_w, ada_b.reshape(depth, 1, n))


def _mod_spec(layer, piece, row_fn, d, n_grid):
    if n_grid == 1:
        return pl.BlockSpec((None, None, 1, d), lambda i: (layer, row_fn(i), 0, piece))
    return pl.BlockSpec((None, None, 1, d), lambda i, j: (layer, row_fn(i), 0, piece))


def _modulated(x_ref, g_ref, sh_ref, sc_ref):
    xf = x_ref[...]
    ms = jnp.mean(xf * xf, axis=-1, keepdims=True)
    gain = g_ref[...] * (1.0 + sc_ref[...])
    return (xf * lax.rsqrt(ms + NORM_EPS) * gain + sh_ref[...]).astype(BF16)


def _swap_halves_32(a, first_half):
    fwd = pltpu.roll(a, 3 * LANES // 4, axis=1)
    bwd = pltpu.roll(a, LANES // 4, axis=1)
    return jnp.where(first_half, fwd, bwd)


def _proj_kernel(*refs, n_tok, n_rope_tok, with_feat, with_gates):
    x_ref, sh_ref, sc_ref, g_ref, wk_ref = refs[:5]
    n = 5
    if with_feat:
        wt_ref = refs[n]
        n += 1
    if with_gates:
        wg_ref, bg_ref = refs[n:n + 2]
        n += 2
    if n_rope_tok:
        cos_ref, sin_ref = refs[n:n + 2]
        n += 2
    k_ref = refs[n]
    n += 1
    if with_feat:
        ft_ref = refs[n]
        n += 1
    if with_gates:
        gate_ref = refs[n]
    h_scr = refs[-1]
    j = pl.program_id(1)
    tm = x_ref.shape[0]
    sub = _tile(tm, EDGE_ROWS)

    def token_tile(rows, rotate):
        if not rotate:
            k_ref[rows, :] = jnp.dot(h_scr[rows, :], wk_ref[...], preferred_element_type=F32).astype(k_ref.dtype)
            return
        cos = cos_ref[rows, :]
        sin = sin_ref[rows, :]
        lane = lax.broadcasted_iota(jnp.int32, cos.shape, 1)
        first_half = (lane % (LANES // 2)) < (LANES // 4)
        for c in range(wk_ref.shape[1] // MXU_COLS):
            acc = jnp.dot(h_scr[rows, :], wk_ref[:, c * MXU_COLS:(c + 1) * MXU_COLS],
                          preferred_element_type=F32)
            for half in range(MXU_COLS // LANES):
                a = acc[:, half * LANES:(half + 1) * LANES]
                r = a * cos + _swap_halves_32(a, first_half) * sin
                lo = c * MXU_COLS + half * LANES
                k_ref[rows, lo:lo + LANES] = r.astype(k_ref.dtype)

    @pl.when(j == 0)
    def _():
        for r0 in range(0, tm, sub):
            rows = slice(r0, r0 + sub)
            h_scr[rows, :] = _modulated(x_ref.at[rows, :], g_ref, sh_ref, sc_ref)
            if with_gates:
                gate_ref[rows, :] = jnp.dot(h_scr[rows, :], wg_ref[...], preferred_element_type=F32) + bg_ref[...]
            token_tile(rows, n_rope_tok > 0)

    if n_rope_tok > 1:
        @pl.when(jnp.logical_and(j > 0, j < n_rope_tok))
        def _():
            token_tile(slice(0, tm), True)

    if n_tok > max(n_rope_tok, 1):
        @pl.when(jnp.logical_and(j >= max(n_rope_tok, 1), j < n_tok))
        def _():
            token_tile(slice(0, tm), False)

    if with_feat:
        @pl.when(j >= n_tok)
        def _():
            ft_ref[...] = _dot_nt(wt_ref[...], h_scr[...]).astype(ft_ref.dtype)


def _proj(x, mods, layer, row_fn, g, wk, wt=None, *, tm, tn, gates=None, rope=None, rope_cols=0):
    m, d = x.shape
    nk = wk.shape[1]
    tm = _tile(m, tm)
    tnk = _tile(nk, tn)
    n_tok = nk // tnk
    assert rope_cols % tnk == 0

    def tok_tile(j):
        return jnp.minimum(j, n_tok - 1)

    def feat_tile(j):
        return jnp.maximum(j - n_tok, 0)

    in_specs = [
        pl.BlockSpec((tm, d), lambda i, j: (i, 0)),
        _mod_spec(layer, 0, row_fn, d, 2),
        _mod_spec(layer, 1, row_fn, d, 2),
        pl.BlockSpec((1, d), lambda i, j: (0, 0)),
        pl.BlockSpec((d, tnk), lambda i, j: (0, tok_tile(j))),
    ]
    args = [x, mods, mods, g, wk]
    out_specs = [pl.BlockSpec((tm, tnk), lambda i, j: (i, tok_tile(j)))]
    out_shape = [jax.ShapeDtypeStruct((m, nk), BF16)]
    n_feat = tnf = 0
    if wt is not None:
        nf = wt.shape[0]
        tnf = _tile(nf, tn)
        n_feat = nf // tnf
        in_specs.append(pl.BlockSpec((tnf, d), lambda i, j: (feat_tile(j), 0)))
        args.append(wt)
        out_specs.append(pl.BlockSpec((tnf, tm), lambda i, j: (feat_tile(j), i)))
        out_shape.append(jax.ShapeDtypeStruct((nf, m), BF16))
    if gates is not None:
        in_specs += [pl.BlockSpec((d, LANES), lambda i, j: (0, 0)),
                     pl.BlockSpec((1, LANES), lambda i, j: (0, 0))]
        args += list(gates)
        out_specs.append(pl.BlockSpec((tm, LANES), lambda i, j: (i, 0)))
        out_shape.append(jax.ShapeDtypeStruct((m, LANES), F32))
    if rope is not None:
        n_pos_tiles = rope[0].shape[0] // tm
        assert rope[0].shape[0] % tm == 0
        in_specs += [pl.BlockSpec((tm, LANES), lambda i, j: (i % n_pos_tiles, 0)),
                     pl.BlockSpec((tm, LANES), lambda i, j: (i % n_pos_tiles, 0))]
        args += list(rope)
    vmem = 2 * tm * d * 4 + tm * d * 2 + 2 * d * tnk * 2 + 2 * tnf * d * 2 + 2 * tm * tnk * 2 + 2 * tnf * tm * 2 \
        + 2 * tm * max(tnk, tnf) * 4 + 8 * tm * LANES * 4 + 2 * d * LANES * 2 + 3 * tm * d * 4
    out = pl.pallas_call(
        functools.partial(_proj_kernel, n_tok=n_tok, n_rope_tok=rope_cols // tnk if rope is not None else 0,
                          with_feat=wt is not None, with_gates=gates is not None),
        grid=(m // tm, n_tok + n_feat),
        in_specs=in_specs,
        out_specs=out_specs,
        out_shape=out_shape,
        scratch_shapes=[pltpu.VMEM((tm, d), BF16)],
        compiler_params=_params(2, vmem),
        name="proj",
    )(*args)
    return out if len(out) > 1 else out[0]


def _outproj_kernel(y_ref, w_ref, x_ref, gate_ref, g_ref, o_ref):
    t = jnp.dot(y_ref[...], w_ref[...], preferred_element_type=F32)
    ms = jnp.mean(t * t, axis=-1, keepdims=True)
    o_ref[...] = x_ref[...] + gate_ref[...] * (t * lax.rsqrt(ms + NORM_EPS) * g_ref[...])


def _outproj(y, w, x, mods, layer, row_fn, g, *, tm):
    m, d = x.shape
    kdim = y.shape[1]
    tm = _tile(m, tm)
    vmem = 2 * tm * kdim * 2 + 2 * kdim * d * 2 + 4 * tm * d * 4 + 3 * tm * d * 4
    return pl.pallas_call(
        _outproj_kernel,
        grid=(m // tm,),
        in_specs=[
            pl.BlockSpec((tm, kdim), lambda i: (i, 0)),
            pl.BlockSpec((kdim, d), lambda i: (0, 0)),
            pl.BlockSpec((tm, d), lambda i: (i, 0)),
            _mod_spec(layer, 2, row_fn, d, 1),
            pl.BlockSpec((1, d), lambda i: (0, 0)),
        ],
        out_specs=pl.BlockSpec((tm, d), lambda i: (i, 0)),
        out_shape=jax.ShapeDtypeStruct((m, d), F32),
        compiler_params=_params(1, vmem),
        name="outproj",
    )(y, w, x, mods, g)


def _mlp_kernel(x_ref, sh_ref, sc_ref, gate_ref, g_in_ref, g_out_ref, w1_ref, w2_ref, o_ref, h_scr):
    j = pl.program_id(1)
    last = pl.num_programs(1) - 1
    tm = o_ref.shape[0]
    sub = _tile(tm, EDGE_ROWS)

    def hidden(h):
        a = jnp.dot(h, w1_ref[...], preferred_element_type=F32)
        a = jnp.maximum(a, 0.0)
        return jnp.dot((a * a).astype(BF16), w2_ref[...], preferred_element_type=F32)

    @pl.when(j == 0)
    def _():
        for r0 in range(0, tm, sub):
            rows = slice(r0, r0 + sub)
            h_scr[rows, :] = _modulated(x_ref.at[rows, :], g_in_ref, sh_ref, sc_ref)
            o_ref[rows, :] = hidden(h_scr[rows, :])

    @pl.when(jnp.logical_and(j > 0, j < last))
    def _():
        o_ref[...] += hidden(h_scr[...])

    @pl.when(j == last)
    def _():
        for r0 in range(0, tm, sub):
            rows = slice(r0, r0 + sub)
            f = o_ref[rows, :] + hidden(h_scr[rows, :])
            ms = jnp.mean(f * f, axis=-1, keepdims=True)
            o_ref[rows, :] = x_ref[rows, :] + gate_ref[...] * (f * lax.rsqrt(ms + NORM_EPS) * g_out_ref[...])


def _mlp(x, mods, layer, row_fn, g_in, g_out, w1, w2, *, tm, tf):
    m, d = x.shape
    dff = w1.shape[1]
    tm = _tile(m, tm)
    tf = _tile(dff, tf)
    vmem = 4 * tm * d * 4 + tm * d * 2 + 4 * d * tf * 2 + 3 * tm * tf * 4 + 4 * 1024 * 1024
    return pl.pallas_call(
        _mlp_kernel,
        grid=(m // tm, dff // tf),
        in_specs=[
            pl.BlockSpec((tm, d), lambda i, j: (i, 0)),
            _mod_spec(layer, 3, row_fn, d, 2),
            _mod_spec(layer, 4, row_fn, d, 2),
            _mod_spec(layer, 5, row_fn, d, 2),
            pl.BlockSpec((1, d), lambda i, j: (0, 0)),
            pl.BlockSpec((1, d), lambda i, j: (0, 0)),
            pl.BlockSpec((d, tf), lambda i, j: (0, j)),
            pl.BlockSpec((tf, d), lambda i, j: (j, 0)),
        ],
        out_specs=pl.BlockSpec((tm, d), lambda i, j: (i, 0)),
        out_shape=jax.ShapeDtypeStruct((m, d), F32),
        scratch_shapes=[pltpu.VMEM((tm, d), BF16)],
        compiler_params=_params(2, vmem),
        name="mlp",
    )(x, mods, mods, mods, g_in, g_out, w1, w2)


def _split3(x):
    x1 = x.astype(BF16)
    r1 = x - x1.astype(F32)
    x2 = r1.astype(BF16)
    x3 = (r1 - x2.astype(F32)).astype(BF16)
    return x1, x2, x3


def _log_sigmoid(x):
    return jnp.minimum(x, 0.0) - jnp.log(1.0 + jnp.exp(-jnp.abs(x)))


def _running_max_lanes(x, reverse):
    lane = lax.broadcasted_iota(jnp.int32, (x.shape[0], LANES), 1)
    slabs = [x[:, j:j + LANES] for j in range(0, x.shape[1], LANES)]
    order = range(len(slabs) - 1, -1, -1) if reverse else range(len(slabs))
    carry = None
    for j in order:
        y = slabs[j]
        k = 1
        while k < LANES:
            if reverse:
                y = jnp.maximum(y, jnp.where(lane < LANES - k, pltpu.roll(y, LANES - k, axis=1), -jnp.inf))
            else:
                y = jnp.maximum(y, jnp.where(lane >= k, pltpu.roll(y, k, axis=1), -jnp.inf))
            k *= 2
        if carry is not None:
            y = jnp.maximum(y, carry)
        carry = jnp.max(y, axis=-1, keepdims=True)
        slabs[j] = y
    return jnp.concatenate(slabs, axis=1)


def _gate_scan_kernel(x_ref, o_ref):
    L = x_ref.shape[-1]
    ui = lax.broadcasted_iota(jnp.int32, (L, L), 0)
    si = lax.broadcasted_iota(jnp.int32, (L, L), 1)

    def cumulative(z, tri):
        return sum(jnp.dot(p, tri.astype(BF16), preferred_element_type=F32) for p in _split3(z))

    b_f = cumulative(_log_sigmoid(x_ref[1]), ui <= si) * LOG2E
    b_b = cumulative(_log_sigmoid(x_ref[3]), ui >= si) * LOG2E
    c_f = x_ref[0] * LOG2E - b_f
    c_b = x_ref[2] * LOG2E - b_b
    o_ref[0] = b_f
    o_ref[1] = c_f
    o_ref[2] = _running_max_lanes(c_f, False)
    o_ref[3] = b_b
    o_ref[4] = c_b
    o_ref[5] = _running_max_lanes(c_b, True)


def _gate_scan(gates, nb, t, L):
    nc = t // L
    g = gates[:, :4 * ML_HEADS].reshape(nb, nc, L, 4, ML_HEADS)
    g = g.transpose(3, 0, 4, 1, 2).reshape(4, nb * ML_HEADS * nc, L)
    nr = g.shape[1]
    tr = _tile(nr, 256)
    n_out = 6
    s = pl.pallas_call(
        _gate_scan_kernel,
        grid=(nr // tr,),
        in_specs=[pl.BlockSpec((4, tr, L), lambda i: (0, i, 0))],
        out_specs=pl.BlockSpec((n_out, tr, L), lambda i: (0, i, 0)),
        out_shape=jax.ShapeDtypeStruct((n_out, nr, L), F32),
        compiler_params=_params(1, 48 * tr * max(L, LANES) * 4),
        name="gate_scan",
    )(g)
    s = s.reshape(n_out, nb, ML_HEADS, nc, L)
    rows = jnp.pad(s.transpose(1, 2, 3, 0, 4), ((0, 0), (0, 0), (0, 0), (0, GATE_ROWS - n_out), (0, 0)))
    return rows


def _mlstm_kernel(kc_ref, qtc_ref, vtc_ref, ogtc_ref, grc_ref,
                  kl_ref, qtl_ref, vtl_ref, ogtl_ref, grl_ref, hg_ref,
                  oc_ref, ol_ref, hf_c, hb_c, hf_l, hb_l):
    L = grc_ref.shape[-1]
    dk = kc_ref.shape[1] // ML_HB
    dv = vtc_ref.shape[0] // ML_HB
    ncc = kc_ref.shape[0] // L
    ncl = kl_ref.shape[0] // L

    si = lax.broadcasted_iota(jnp.int32, (L, L), 0)
    ti = lax.broadcasted_iota(jnp.int32, (L, L), 1)
    ones_rows = (lax.broadcasted_iota(jnp.int32, (ML_AUG, L), 0) == 0).astype(BF16)

    gate_tiles = {}
    for gr_ref, nc in ((grc_ref, ncc), (grl_ref, ncl)):
        for hh in range(ML_HB):
            for c in range(nc):
                tile = gr_ref[hh, c]
                gate_tiles[(id(gr_ref), hh, c)] = (tile, tile.T)

    st_cache = {}

    def scores_t(k_ref, qt_ref, hh, c):
        key = (id(k_ref), hh, c)
        if key not in st_cache:
            tok = slice(c * L, (c + 1) * L)
            st_cache[key] = jnp.dot(k_ref[tok, hh * dk:(hh + 1) * dk], qt_ref[hh * dk:(hh + 1) * dk, tok],
                                    preferred_element_type=F32)
        return st_cache[key]

    def chunk(refs, c, hh, backward, state):
        k_ref, qt_ref, vt_ref, gr_ref, hs_ref = refs
        caug, m = state
        tok = slice(c * L, (c + 1) * L)
        kb = k_ref[tok, hh * dk:(hh + 1) * dk]
        qt = qt_ref[hh * dk:(hh + 1) * dk, tok]
        vt_aug = jnp.concatenate([vt_ref[hh * dv:(hh + 1) * dv, tok], ones_rows], axis=0)
        rows, cols = gate_tiles[(id(gr_ref), hh, c)]
        r = 3 if backward else 0
        b_row = rows[r:r + 1, :]
        c_row = rows[r + 1:r + 2, :]
        c_max = rows[r + 2:r + 3, :]
        c_col = cols[:, r + 1:r + 2]
        b_end = b_row[:, 0:1] if backward else b_row[:, L - 1:L]

        inter = b_row + m
        m_row = jnp.maximum(inter, b_row + c_max)
        mask = (si >= ti) if backward else (si <= ti)
        decay = jnp.where(mask, jnp.exp2(c_col + (b_row - m_row)), 0.0)
        w_row = jnp.exp2(inter - m_row)

        p_t = (scores_t(k_ref, qt_ref, hh, c) * decay).astype(BF16)
        lhs = jnp.concatenate([caug.astype(BF16), vt_aug], axis=1)
        rhs = jnp.concatenate([qt * w_row.astype(BF16), p_t], axis=0)
        out = jnp.dot(lhs, rhs, preferred_element_type=F32)
        den = out[dv:dv + 1, :]
        hs_ref[hh * dv:(hh + 1) * dv, tok] = out[0:dv, :] * (1.0 / jnp.maximum(jnp.abs(den), jnp.exp2(-m_row)))

        g_row = b_end + c_row
        m_new = jnp.maximum(b_end + m, jnp.max(g_row, axis=-1, keepdims=True))
        a = jnp.exp2(b_end + m - m_new)
        ws = jnp.exp2(g_row - m_new)
        caug_new = a * caug + jnp.dot(vt_aug * ws.astype(BF16), kb, preferred_element_type=F32)
        return caug_new, m_new

    def finish_chunk(hf_ref, hb_ref, ogt_ref, o_ref, c):
        tok = slice(c * L, (c + 1) * L)
        for hh in range(ML_HB):
            feat = slice(hh * dv, (hh + 1) * dv)
            hs = hf_ref[feat, tok] + hb_ref[feat, tok]
            scale = lax.rsqrt(jnp.mean(hs * hs, axis=0, keepdims=True) + NORM_EPS)
            head_g = jnp.concatenate([hg_ref[feat, :]] * (L // LANES), axis=1)
            y_t = hs * scale * head_g * jax.nn.sigmoid(ogt_ref[feat, tok].astype(F32))
            o_ref[tok, feat] = y_t.T.astype(o_ref.dtype)

    ctx_f = (kc_ref, qtc_ref, vtc_ref, grc_ref, hf_c)
    ctx_b = (kc_ref, qtc_ref, vtc_ref, grc_ref, hb_c)
    lat_f = (kl_ref, qtl_ref, vtl_ref, grl_ref, hf_l)
    lat_b = (kl_ref, qtl_ref, vtl_ref, grl_ref, hb_l)
    steps = [(ctx_f, s, ctx_b, ncc - 1 - s) for s in range(ncc)] \
        + [(lat_f, s, lat_b, ncl - 1 - s) for s in range(ncl)]

    def issue_scores(step):
        refs_f, c_f, refs_b, c_b = step
        for hh in range(ML_HB):
            scores_t(refs_f[0], refs_f[1], hh, c_f)
            scores_t(refs_b[0], refs_b[1], hh, c_b)

    zero = (jnp.zeros((dv + ML_AUG, dk), F32), jnp.zeros((1, 1), F32))
    states = [zero] * (2 * ML_HB)
    issue_scores(steps[0])
    for n, (refs_f, c_f, refs_b, c_b) in enumerate(steps):
        if n + 1 < len(steps):
            issue_scores(steps[n + 1])
        nxt = []
        for hh in range(ML_HB):
            nxt.append(chunk(refs_f, c_f, hh, False, states[2 * hh]))
            nxt.append(chunk(refs_b, c_b, hh, True, states[2 * hh + 1]))
        states = nxt
        if c_f >= c_b:
            ogt_ref, o_ref = (ogtc_ref, oc_ref) if refs_f is ctx_f else (ogtl_ref, ol_ref)
            for c in sorted({c_f, c_b}):
                finish_chunk(refs_f[4], refs_b[4], ogt_ref, o_ref, c)


def _mlstm(k_ctx, ft_ctx, g_ctx, k_lat, ft_lat, g_lat, head_g, nb, d):
    ctx_len = k_ctx.shape[0] // nb
    seq = k_lat.shape[0] // nb
    dv = ML_HB * (d // ML_HEADS)
    dk = dv // 2
    nh = ML_HEADS // ML_HB
    q_rows = ML_HEADS * (d // ML_HEADS // 2)
    v_blk = q_rows // dv
    og_blk = (q_rows + d) // dv
    L = math.gcd(math.gcd(ctx_len, seq), ML_BLOCK)
    assert L % LANES == 0
    grc = _gate_scan(g_ctx, nb, ctx_len, L)
    grl = _gate_scan(g_lat, nb, seq, L)
    hg = jnp.broadcast_to(head_g.reshape(d, 1), (d, LANES))

    def stream_specs(t):
        return [
            pl.BlockSpec((t, dk), lambda b, i: (b, i)),
            pl.BlockSpec((dk, t), lambda b, i: (i, b)),
            pl.BlockSpec((dv, t), lambda b, i: (v_blk + i, b)),
            pl.BlockSpec((dv, t), lambda b, i: (og_blk + i, b)),
            pl.BlockSpec((None, ML_HB, t // L, GATE_ROWS, L), lambda b, i: (b, i, 0, 0, 0)),
        ]

    t_all = ctx_len + seq
    gate_tiles = 2 * ML_HB * (t_all // L) * GATE_ROWS * L * 4
    vmem = 2 * t_all * (2 * dk * 2 + 3 * dv * 2) + 2 * t_all * dv * 4 + gate_tiles + 12 * 1024 * 1024
    return pl.pallas_call(
        _mlstm_kernel,
        grid=(nb, nh),
        in_specs=stream_specs(ctx_len) + stream_specs(seq) + [pl.BlockSpec((dv, LANES), lambda b, i: (i, 0))],
        out_specs=[pl.BlockSpec((ctx_len, dv), lambda b, i: (b, i)),
                   pl.BlockSpec((seq, dv), lambda b, i: (b, i))],
        out_shape=[jax.ShapeDtypeStruct((nb * ctx_len, d), BF16),
                   jax.ShapeDtypeStruct((nb * seq, d), BF16)],
        scratch_shapes=[pltpu.VMEM((dv, ctx_len), F32), pltpu.VMEM((dv, ctx_len), F32),
                        pltpu.VMEM((dv, seq), F32), pltpu.VMEM((dv, seq), F32)],
        compiler_params=_params(2, vmem),
        name="mlstm",
    )(k_ctx, ft_ctx, ft_ctx, ft_ctx, grc, k_lat, ft_lat, ft_lat, ft_lat, grl, hg)


def _attn_kernel(q_ref, kl_ref, vl_ref, kc_ref, vc_ref, lam_ref, gs_ref, o_ref, *, lam_init):
    dh = q_ref.shape[1] // 2
    seq = q_ref.shape[0]
    ctx_len = kc_ref.shape[0]
    lq = lam_ref[...]
    lam = jnp.exp(jnp.sum(lq[0:1] * lq[1:2], axis=-1, keepdims=True)) \
        - jnp.exp(jnp.sum(lq[2:3] * lq[3:4], axis=-1, keepdims=True)) + lam_init

    tq = _tile(seq, ATTN_TQ)
    tk = _tile(seq, ATTN_TK)
    blocks = [(kl_ref, vl_ref, j * tk, tk) for j in range(seq // tk)] + [(kc_ref, vc_ref, 0, ctx_len)]

    def scores(t, blk):
        k_ref, _, start, size = blocks[blk]
        rows = slice(t * tq, (t + 1) * tq)
        keys = slice(start, start + size)
        return jnp.concatenate([_dot_nt(q_ref[rows, 0:dh], k_ref[keys, 0:dh]),
                                _dot_nt(q_ref[rows, dh:2 * dh], k_ref[keys, dh:2 * dh])], axis=0)

    items = [(t, blk) for t in range(seq // tq) for blk in range(len(blocks))]
    s = scores(*items[0])
    m = l = acc = None
    for n, (t, blk) in enumerate(items):
        s_next = scores(*items[n + 1]) if n + 1 < len(items) else None
        _, v_ref, start, size = blocks[blk]
        mx = jnp.max(s, axis=-1, keepdims=True)
        m_new = mx if blk == 0 else jnp.maximum(m, mx)
        p = jnp.exp2(s - m_new)
        ps = jnp.sum(p, axis=-1, keepdims=True)
        pv = jnp.dot(p.astype(BF16), v_ref[start:start + size, :], preferred_element_type=F32)
        if blk == 0:
            l, acc = ps, pv
        else:
            alpha = jnp.exp2(m - m_new)
            l = alpha * l + ps
            acc = alpha * acc + pv
        m = m_new
        if blk == len(blocks) - 1:
            a = acc * (1.0 / l)
            o = a[0:tq] - lam * a[tq:2 * tq]
            scale = lax.rsqrt(jnp.mean(o * o, axis=-1, keepdims=True) + NORM_EPS)
            o_ref[t * tq:(t + 1) * tq, :] = (o * scale * gs_ref[...] * (1.0 - lam_init)).astype(o_ref.dtype)
        s = s_next


def _attn(p_lat, p_ctx, lam_qk, g_sub, nb, d, lam_init):
    seq = p_lat.shape[0] // nb
    ctx_len = p_ctx.shape[0] // nb
    h = DA_HEADS
    hd = d // h
    tq = _tile(seq, ATTN_TQ)
    tk = _tile(seq, ATTN_TK)
    vmem = 2 * (4 * seq + 2 * ctx_len) * hd * 2 + 48 * tq * tk * 4 + 32 * tq * hd * 4 + 8 * 1024 * 1024
    return pl.pallas_call(
        functools.partial(_attn_kernel, lam_init=lam_init),
        grid=(nb, h),
        in_specs=[
            pl.BlockSpec((seq, hd), lambda b, i: (b, i)),
            pl.BlockSpec((seq, hd), lambda b, i: (b, h + i)),
            pl.BlockSpec((seq, hd), lambda b, i: (b, 2 * h + i)),
            pl.BlockSpec((ctx_len, hd), lambda b, i: (b, i)),
            pl.BlockSpec((ctx_len, hd), lambda b, i: (b, h + i)),
            pl.BlockSpec(lam_qk.shape, lambda b, i: (0, 0)),
            pl.BlockSpec((1, hd), lambda b, i: (0, 0)),
        ],
        out_specs=pl.BlockSpec((seq, hd), lambda b, i: (b, i)),
        out_shape=jax.ShapeDtypeStruct((nb * seq, d), BF16),
        compiler_params=_params(2, vmem),
        name="attn",
    )(p_lat, p_lat, p_lat, p_ctx, p_ctx, lam_qk, g_sub)


def _rope_tables(seq, dh):
    nf = dh // 4
    n = jnp.arange(seq)
    row = (n // GRID_W).astype(F32)
    col = (n % GRID_W).astype(F32)
    freq = ROPE_BASE ** (-jnp.arange(nf, dtype=F32) / nf)
    ar = row[:, None] * freq
    ac = col[:, None] * freq
    cos = jnp.concatenate([jnp.cos(ar), jnp.cos(ar), jnp.cos(ac), jnp.cos(ac)], axis=1)
    sin = jnp.concatenate([-jnp.sin(ar), jnp.sin(ar), -jnp.sin(ac), jnp.sin(ac)], axis=1)
    return cos, sin


def kernel(x, c, ctx, c_ctx, ada_w, ada_b, norm_g, mlp_w1, mlp_w2, ml_w_in, ml_b_gates, ml_head_g,
           ml_w_out, da_w_in, da_lambda, da_sub_g, da_w_out):
    nb, seq, d = x.shape
    ctx_len = ctx.shape[1]
    depth = ada_w.shape[0]
    assert depth == 2 and nb < COND_ROWS and d // ML_HEADS == 2 * LANES
    assert seq % LANES == 0 and ctx_len % LANES == 0 and seq % GRID_W == 0

    xl = x.reshape(nb * seq, d)
    xc = ctx.reshape(nb * ctx_len, d)

    cond = jnp.zeros((COND_ROWS, d), F32).at[:nb].set(c).at[nb].set(c_ctx)
    mods = _ada(cond, ada_w, ada_b).reshape(depth, COND_ROWS, 1, N_MOD * d)

    tm = _tile(seq, 512)
    tm_mlp = _tile(seq, 1024)
    tmc = _tile(nb * ctx_len, 512)

    def lat_row_of(tile_rows):
        return lambda i: i // (seq // tile_rows)

    lat_row = lat_row_of(tm)
    lat_row_mlp = lat_row_of(tm_mlp)

    def ctx_row(i):
        return nb

    g = norm_g.reshape(depth, 4, 1, d)

    ml_qk = ML_HEADS * (d // ML_HEADS // 2)
    n_main = 2 * ml_qk + 2 * d
    w_in = ml_w_in[0]
    q_scale = (d // ML_HEADS // 2) ** -0.5
    w_k = w_in[:, ml_qk:2 * ml_qk].astype(BF16)
    w_feat_t = jnp.concatenate([w_in[:, :ml_qk] * q_scale, w_in[:, 2 * ml_qk:n_main]], axis=1).T.astype(BF16)
    n_gate = 4 * ML_HEADS
    w_gate = jnp.pad(w_in[:, n_main:], ((0, 0), (0, LANES - n_gate))).astype(BF16)
    b_gate = jnp.pad(ml_b_gates[0], (0, LANES - n_gate)).reshape(1, LANES)
    w_out0 = ml_w_out[0].astype(BF16)
    w1_0 = mlp_w1[0].astype(BF16)
    w2_0 = mlp_w2[0].astype(BF16)

    k_lat, ft_lat, g_lat = _proj(xl, mods, 0, lat_row_mlp, g[0, 0], w_k, w_feat_t, tm=tm_mlp, tn=1024,
                                 gates=(w_gate, b_gate))
    k_ctx, ft_ctx, g_ctx = _proj(xc, mods, 0, ctx_row, g[0, 0], w_k, w_feat_t, tm=tmc, tn=1024,
                                 gates=(w_gate, b_gate))
    y_ctx, y_lat = _mlstm(k_ctx, ft_ctx, g_ctx, k_lat, ft_lat, g_lat, ml_head_g[0], nb, d)
    xl = _outproj(y_lat, w_out0, xl, mods, 0, lat_row, g[0, 1], tm=tm)
    xc = _outproj(y_ctx, w_out0, xc, mods, 0, ctx_row, g[0, 1], tm=tmc)
    xl = _mlp(xl, mods, 0, lat_row_mlp, g[0, 2], g[0, 3], w1_0, w2_0, tm=tm_mlp, tf=512)
    xc = _mlp(xc, mods, 0, ctx_row, g[0, 2], g[0, 3], w1_0, w2_0, tm=tmc, tf=512)

    dh = d // (2 * DA_HEADS)
    w_in = da_w_in[0]
    w_qkv = jnp.concatenate([w_in[:, :d] * (dh ** -0.5 * LOG2E), w_in[:, d:]], axis=1).astype(BF16)
    w_kv = w_in[:, d:].astype(BF16)
    w_out1 = da_w_out[0].astype(BF16)
    w1_1 = mlp_w1[1].astype(BF16)
    w2_1 = mlp_w2[1].astype(BF16)
    layer_idx = 1
    lam_init = 0.8 - 0.6 * math.exp(-0.3 * layer_idx)

    p_lat = _proj(xl, mods, 1, lat_row_mlp, g[1, 0], w_qkv, tm=tm_mlp, tn=1024,
                  rope=_rope_tables(seq, dh), rope_cols=2 * d)
    p_ctx = _proj(xc, mods, 1, ctx_row, g[1, 0], w_kv, tm=tmc, tn=1024)
    y_lat = _attn(p_lat, p_ctx, da_lambda[0], da_sub_g[0].reshape(1, 2 * dh), nb, d, lam_init)
    xl = _outproj(y_lat, w_out1, xl, mods, 1, lat_row, g[1, 1], tm=tm)
    xl = _mlp(xl, mods, 1, lat_row_mlp, g[1, 2], g[1, 3], w1_1, w2_1, tm=tm_mlp, tf=512)
    return xl.reshape(nb, seq, d)
```

```python
import functools
import math

import jax
import jax.numpy as jnp
from jax import lax
from jax.experimental import pallas as pl
from jax.experimental.pallas import tpu as pltpu

F32 = jnp.float32
BF16 = jnp.bfloat16

NORM_EPS = 1e-6
GRID_W = 64
ROPE_BASE = 10000.0
LOG2E = math.log2(math.e)
ML_HEADS = 8
ML_BLOCK = 256
ML_HB = 2
ML_AUG = 16
DA_HEADS = 8
EDGE_ROWS = 256
ATTN_TQ = 256
ATTN_AHEAD = 2
N_MOD = 6
COND_ROWS = 32
LANES = 128
MXU_COLS = 256
GATE_ROWS = 8

VMEM_CAP_BYTES = 56 * 1024 * 1024


def _vmem_limit(nbytes):
    return int(min(VMEM_CAP_BYTES, max(16 * 1024 * 1024, nbytes)))


def _params(n_grid, vmem_bytes):
    return pltpu.CompilerParams(
        dimension_semantics=("arbitrary",) * n_grid,
        vmem_limit_bytes=_vmem_limit(vmem_bytes))


def _tile(n, want):
    if n <= want:
        return n
    t = want
    while t >= 8:
        if n % t == 0 and t % 8 == 0:
            return t
        t -= 8
    return n


def _dot_nt(a, b):
    return lax.dot_general(a, b, (((1,), (1,)), ((), ())), preferred_element_type=F32)


def _ada_kernel(c_ref, w_ref, b_ref, o_ref):
    cf = c_ref[...]
    s = (cf * jax.nn.sigmoid(cf)).astype(BF16)
    o_ref[...] = jnp.dot(s, w_ref[...].astype(BF16), preferred_element_type=F32) + b_ref[...]


def _ada(cond, ada_w, ada_b):
    depth, d, n = ada_w.shape
    tn = _tile(n, 1024)
    return pl.pallas_call(
        _ada_kernel,
        grid=(depth, n // tn),
        in_specs=[
            pl.BlockSpec((COND_ROWS, d), lambda l, j: (0, 0)),
            pl.BlockSpec((None, d, tn), lambda l, j: (l, 0, j)),
            pl.BlockSpec((None, 1, tn), lambda l, j: (l, 0, j)),
        ],
        out_specs=pl.BlockSpec((None, COND_ROWS, tn), lambda l, j: (l, 0, j)),
        out_shape=jax.ShapeDtypeStruct((depth, COND_ROWS, n), F32),
        compiler_params=_params(2, 2 * d * tn * 4 + 4 * COND_ROWS * (d + tn) * 4 + d * tn * 2),
        name="ada",
    )(cond, ada_w, ada_b.reshape(depth, 1, n))


def _mod_spec(layer, piece, row_fn, d, n_grid):
    if n_grid == 1:
        return pl.BlockSpec((None, None, 1, d), lambda i: (layer, row_fn(i), 0, piece))
    return pl.BlockSpec((None, None, 1, d), lambda i, j: (layer, row_fn(i), 0, piece))


def _modulated(x_ref, g_ref, sh_ref, sc_ref):
    xf = x_ref[...]
    ms = jnp.mean(xf * xf, axis=-1, keepdims=True)
    gain = g_ref[...] * (1.0 + sc_ref[...])
    return (xf * lax.rsqrt(ms + NORM_EPS) * gain + sh_ref[...]).astype(BF16)


def _swap_halves_32(a, first_half):
    fwd = pltpu.roll(a, 3 * LANES // 4, axis=1)
    bwd = pltpu.roll(a, LANES // 4, axis=1)
    return jnp.where(first_half, fwd, bwd)


def _proj_kernel(*refs, n_tok, n_rope_tok, with_feat, with_gates):
    x_ref, sh_ref, sc_ref, g_ref, wk_ref = refs[:5]
    n = 5
    if with_feat:
        wt_ref = refs[n]
        n += 1
    if with_gates:
        wg_ref, bg_ref = refs[n:n + 2]
        n += 2
    if n_rope_tok:
        cos_ref, sin_ref = refs[n:n + 2]
        n += 2
    k_ref = refs[n]
    n += 1
    if with_feat:
        ft_ref = refs[n]
        n += 1
    if with_gates:
        gate_ref = refs[n]
    h_scr = refs[-1]
    j = pl.program_id(1)
    tm = x_ref.shape[0]
    sub = _tile(tm, EDGE_ROWS)

    def token_tile(rows, rotate):
        if not rotate:
            k_ref[rows, :] = jnp.dot(h_scr[rows, :], wk_ref[...], preferred_element_type=F32).astype(k_ref.dtype)
            return
        cos = cos_ref[rows, :]
        sin = sin_ref[rows, :]
        lane = lax.broadcasted_iota(jnp.int32, cos.shape, 1)
        first_half = (lane % (LANES // 2)) < (LANES // 4)
        for c in range(wk_ref.shape[1] // MXU_COLS):
            acc = jnp.dot(h_scr[rows, :], wk_ref[:, c * MXU_COLS:(c + 1) * MXU_COLS],
                          preferred_element_type=F32)
            for half in range(MXU_COLS // LANES):
                a = acc[:, half * LANES:(half + 1) * LANES]
                r = a * cos + _swap_halves_32(a, first_half) * sin
                lo = c * MXU_COLS + half * LANES
                k_ref[rows, lo:lo + LANES] = r.astype(k_ref.dtype)

    @pl.when(j == 0)
    def _():
        for r0 in range(0, tm, sub):
            rows = slice(r0, r0 + sub)
            h_scr[rows, :] = _modulated(x_ref.at[rows, :], g_ref, sh_ref, sc_ref)
            if with_gates:
                gate_ref[rows, :] = jnp.dot(h_scr[rows, :], wg_ref[...], preferred_element_type=F32) + bg_ref[...]
            token_tile(rows, n_rope_tok > 0)

    if n_rope_tok > 1:
        @pl.when(jnp.logical_and(j > 0, j < n_rope_tok))
        def _():
            token_tile(slice(0, tm), True)

    if n_tok > max(n_rope_tok, 1):
        @pl.when(jnp.logical_and(j >= max(n_rope_tok, 1), j < n_tok))
        def _():
            token_tile(slice(0, tm), False)

    if with_feat:
        @pl.when(j >= n_tok)
        def _():
            ft_ref[...] = _dot_nt(wt_ref[...], h_scr[...]).astype(ft_ref.dtype)


def _proj(x, mods, layer, row_fn, g, wk, wt=None, *, tm, tn, gates=None, rope=None, rope_cols=0):
    m, d = x.shape
    nk = wk.shape[1]
    tm = _tile(m, tm)
    tnk = _tile(nk, tn)
    n_tok = nk // tnk
    assert rope_cols % tnk == 0

    def tok_tile(j):
        return jnp.minimum(j, n_tok - 1)

    def feat_tile(j):
        return jnp.maximum(j - n_tok, 0)

    in_specs = [
        pl.BlockSpec((tm, d), lambda i, j: (i, 0)),
        _mod_spec(layer, 0, row_fn, d, 2),
        _mod_spec(layer, 1, row_fn, d, 2),
        pl.BlockSpec((1, d), lambda i, j: (0, 0)),
        pl.BlockSpec((d, tnk), lambda i, j: (0, tok_tile(j))),
    ]
    args = [x, mods, mods, g, wk]
    out_specs = [pl.BlockSpec((tm, tnk), lambda i, j: (i, tok_tile(j)))]
    out_shape = [jax.ShapeDtypeStruct((m, nk), BF16)]
    n_feat = tnf = 0
    if wt is not None:
        nf = wt.shape[0]
        tnf = _tile(nf, tn)
        n_feat = nf // tnf
        in_specs.append(pl.BlockSpec((tnf, d), lambda i, j: (feat_tile(j), 0)))
        args.append(wt)
        out_specs.append(pl.BlockSpec((tnf, tm), lambda i, j: (feat_tile(j), i)))
        out_shape.append(jax.ShapeDtypeStruct((nf, m), BF16))
    if gates is not None:
        in_specs += [pl.BlockSpec((d, LANES), lambda i, j: (0, 0)),
                     pl.BlockSpec((1, LANES), lambda i, j: (0, 0))]
        args += list(gates)
        out_specs.append(pl.BlockSpec((tm, LANES), lambda i, j: (i, 0)))
        out_shape.append(jax.ShapeDtypeStruct((m, LANES), F32))
    if rope is not None:
        n_pos_tiles = rope[0].shape[0] // tm
        assert rope[0].shape[0] % tm == 0
        in_specs += [pl.BlockSpec((tm, LANES), lambda i, j: (i % n_pos_tiles, 0)),
                     pl.BlockSpec((tm, LANES), lambda i, j: (i % n_pos_tiles, 0))]
        args += list(rope)
    vmem = 2 * tm * d * 4 + tm * d * 2 + 2 * d * tnk * 2 + 2 * tnf * d * 2 + 2 * tm * tnk * 2 + 2 * tnf * tm * 2 \
        + 2 * tm * max(tnk, tnf) * 4 + 8 * tm * LANES * 4 + 2 * d * LANES * 2 + 3 * tm * d * 4
    out = pl.pallas_call(
        functools.partial(_proj_kernel, n_tok=n_tok, n_rope_tok=rope_cols // tnk if rope is not None else 0,
                          with_feat=wt is not None, with_gates=gates is not None),
        grid=(m // tm, n_tok + n_feat),
        in_specs=in_specs,
        out_specs=out_specs,
        out_shape=out_shape,
        scratch_shapes=[pltpu.VMEM((tm, d), BF16)],
        compiler_params=_params(2, vmem),
        name="proj",
    )(*args)
    return out if len(out) > 1 else out[0]


def _outproj_kernel(y_ref, w_ref, x_ref, gate_ref, g_ref, o_ref):
    t = jnp.dot(y_ref[...], w_ref[...], preferred_element_type=F32)
    ms = jnp.mean(t * t, axis=-1, keepdims=True)
    o_ref[...] = x_ref[...] + gate_ref[...] * (t * lax.rsqrt(ms + NORM_EPS) * g_ref[...])


def _outproj(y, w, x, mods, layer, row_fn, g, *, tm):
    m, d = x.shape
    kdim = y.shape[1]
    tm = _tile(m, tm)
    vmem = 2 * tm * kdim * 2 + 2 * kdim * d * 2 + 4 * tm * d * 4 + 3 * tm * d * 4
    return pl.pallas_call(
        _outproj_kernel,
        grid=(m // tm,),
        in_specs=[
            pl.BlockSpec((tm, kdim), lambda i: (i, 0)),
            pl.BlockSpec((kdim, d), lambda i: (0, 0)),
            pl.BlockSpec((tm, d), lambda i: (i, 0)),
            _mod_spec(layer, 2, row_fn, d, 1),
            pl.BlockSpec((1, d), lambda i: (0, 0)),
        ],
        out_specs=pl.BlockSpec((tm, d), lambda i: (i, 0)),
        out_shape=jax.ShapeDtypeStruct((m, d), F32),
        compiler_params=_params(1, vmem),
        name="outproj",
    )(y, w, x, mods, g)


def _mlp_kernel(x_ref, sh_ref, sc_ref, gate_ref, g_in_ref, g_out_ref, w1_ref, w2_ref, o_ref, h_scr):
    j = pl.program_id(1)
    last = pl.num_programs(1) - 1
    tm = o_ref.shape[0]
    sub = _tile(tm, EDGE_ROWS)

    def hidden(h):
        a = jnp.dot(h, w1_ref[...], preferred_element_type=F32)
        a = jnp.maximum(a, 0.0)
        return jnp.dot((a * a).astype(BF16), w2_ref[...], preferred_element_type=F32)

    @pl.when(j == 0)
    def _():
        for r0 in range(0, tm, sub):
            rows = slice(r0, r0 + sub)
            h_scr[rows, :] = _modulated(x_ref.at[rows, :], g_in_ref, sh_ref, sc_ref)
            o_ref[rows, :] = hidden(h_scr[rows, :])

    @pl.when(jnp.logical_and(j > 0, j < last))
    def _():
        o_ref[...] += hidden(h_scr[...])

    @pl.when(j == last)
    def _():
        for r0 in range(0, tm, sub):
            rows = slice(r0, r0 + sub)
            f = o_ref[rows, :] + hidden(h_scr[rows, :])
            ms = jnp.mean(f * f, axis=-1, keepdims=True)
            o_ref[rows, :] = x_ref[rows, :] + gate_ref[...] * (f * lax.rsqrt(ms + NORM_EPS) * g_out_ref[...])


def _mlp(x, mods, layer, row_fn, g_in, g_out, w1, w2, *, tm, tf):
    m, d = x.shape
    dff = w1.shape[1]
    tm = _tile(m, tm)
    tf = _tile(dff, tf)
    vmem = 4 * tm * d * 4 + tm * d * 2 + 4 * d * tf * 2 + 3 * tm * tf * 4 + 4 * 1024 * 1024
    return pl.pallas_call(
        _mlp_kernel,
        grid=(m // tm, dff // tf),
        in_specs=[
            pl.BlockSpec((tm, d), lambda i, j: (i, 0)),
            _mod_spec(layer, 3, row_fn, d, 2),
            _mod_spec(layer, 4, row_fn, d, 2),
            _mod_spec(layer, 5, row_fn, d, 2),
            pl.BlockSpec((1, d), lambda i, j: (0, 0)),
            pl.BlockSpec((1, d), lambda i, j: (0, 0)),
            pl.BlockSpec((d, tf), lambda i, j: (0, j)),
            pl.BlockSpec((tf, d), lambda i, j: (j, 0)),
        ],
        out_specs=pl.BlockSpec((tm, d), lambda i, j: (i, 0)),
        out_shape=jax.ShapeDtypeStruct((m, d), F32),
        scratch_shapes=[pltpu.VMEM((tm, d), BF16)],
        compiler_params=_params(2, vmem),
        name="mlp",
    )(x, mods, mods, mods, g_in, g_out, w1, w2)


def _split3(x):
    x1 = x.astype(BF16)
    r1 = x - x1.astype(F32)
    x2 = r1.astype(BF16)
    x3 = (r1 - x2.astype(F32)).astype(BF16)
    return x1, x2, x3


def _log_sigmoid(x):
    return jnp.minimum(x, 0.0) - jnp.log(1.0 + jnp.exp(-jnp.abs(x)))


def _running_max_lanes(x, reverse):
    lane = lax.broadcasted_iota(jnp.int32, (x.shape[0], LANES), 1)
    slabs = [x[:, j:j + LANES] for j in range(0, x.shape[1], LANES)]
    order = range(len(slabs) - 1, -1, -1) if reverse else range(len(slabs))
    carry = None
    for j in order:
        y = slabs[j]
        k = 1
        while k < LANES:
            if reverse:
                y = jnp.maximum(y, jnp.where(lane < LANES - k, pltpu.roll(y, LANES - k, axis=1), -jnp.inf))
            else:
                y = jnp.maximum(y, jnp.where(lane >= k, pltpu.roll(y, k, axis=1), -jnp.inf))
            k *= 2
        if carry is not None:
            y = jnp.maximum(y, carry)
        carry = jnp.max(y, axis=-1, keepdims=True)
        slabs[j] = y
    return jnp.concatenate(slabs, axis=1)


def _gate_scan_kernel(x_ref, o_ref):
    L = x_ref.shape[-1]
    ui = lax.broadcasted_iota(jnp.int32, (L, L), 0)
    si = lax.broadcasted_iota(jnp.int32, (L, L), 1)

    def cumulative(z, tri):
        return sum(jnp.dot(p, tri.astype(BF16), preferred_element_type=F32) for p in _split3(z))

    b_f = cumulative(_log_sigmoid(x_ref[1]), ui <= si) * LOG2E
    b_b = cumulative(_log_sigmoid(x_ref[3]), ui >= si) * LOG2E
    c_f = x_ref[0] * LOG2E - b_f
    c_b = x_ref[2] * LOG2E - b_b
    o_ref[0] = b_f
    o_ref[1] = c_f
    o_ref[2] = _running_max_lanes(c_f, False)
    o_ref[3] = b_b
    o_ref[4] = c_b
    o_ref[5] = _running_max_lanes(c_b, True)


def _gate_scan(gates, nb, t, L):
    nc = t // L
    g = gates[:, :4 * ML_HEADS].reshape(nb, nc, L, 4, ML_HEADS)
    g = g.transpose(3, 0, 4, 1, 2).reshape(4, nb * ML_HEADS * nc, L)
    nr = g.shape[1]
    tr = _tile(nr, 256)
    n_out = 6
    s = pl.pallas_call(
        _gate_scan_kernel,
        grid=(nr // tr,),
        in_specs=[pl.BlockSpec((4, tr, L), lambda i: (0, i, 0))],
        out_specs=pl.BlockSpec((n_out, tr, L), lambda i: (0, i, 0)),
        out_shape=jax.ShapeDtypeStruct((n_out, nr, L), F32),
        compiler_params=_params(1, 48 * tr * max(L, LANES) * 4),
        name="gate_scan",
    )(g)
    s = s.reshape(n_out, nb, ML_HEADS, nc, L)
    rows = jnp.pad(s.transpose(1, 2, 3, 0, 4), ((0, 0), (0, 0), (0, 0), (0, GATE_ROWS - n_out), (0, 0)))
    return rows


def _mlstm_kernel(kc_ref, qtc_ref, vtc_ref, ogtc_ref, grc_ref,
                  kl_ref, qtl_ref, vtl_ref, ogtl_ref, grl_ref, hg_ref,
                  oc_ref, ol_ref, hf_c, hb_c, hf_l, hb_l):
    L = grc_ref.shape[-1]
    dk = kc_ref.shape[1] // ML_HB
    dv = vtc_ref.shape[0] // ML_HB
    ncc = kc_ref.shape[0] // L
    ncl = kl_ref.shape[0] // L

    si = lax.broadcasted_iota(jnp.int32, (L, L), 0)
    ti = lax.broadcasted_iota(jnp.int32, (L, L), 1)
    ones_rows = (lax.broadcasted_iota(jnp.int32, (ML_AUG, L), 0) == 0).astype(BF16)

    gate_tiles = {}
    for gr_ref, nc in ((grc_ref, ncc), (grl_ref, ncl)):
        for hh in range(ML_HB):
            for c in range(nc):
                tile = gr_ref[hh, c]
                gate_tiles[(id(gr_ref), hh, c)] = (tile, tile.T)

    st_cache = {}

    def scores_t(k_ref, qt_ref, hh, c):
        key = (id(k_ref), hh, c)
        if key not in st_cache:
            tok = slice(c * L, (c + 1) * L)
            st_cache[key] = jnp.dot(k_ref[tok, hh * dk:(hh + 1) * dk], qt_ref[hh * dk:(hh + 1) * dk, tok],
                                    preferred_element_type=F32)
        return st_cache[key]

    def chunk(refs, c, hh, backward, state):
        k_ref, qt_ref, vt_ref, gr_ref, hs_ref = refs
        caug, m = state
        tok = slice(c * L, (c + 1) * L)
        kb = k_ref[tok, hh * dk:(hh + 1) * dk]
        qt = qt_ref[hh * dk:(hh + 1) * dk, tok]
        vt_aug = jnp.concatenate([vt_ref[hh * dv:(hh + 1) * dv, tok], ones_rows], axis=0)
        rows, cols = gate_tiles[(id(gr_ref), hh, c)]
        r = 3 if backward else 0
        b_row = rows[r:r + 1, :]
        c_row = rows[r + 1:r + 2, :]
        c_max = rows[r + 2:r + 3, :]
        c_col = cols[:, r + 1:r + 2]
        b_end = b_row[:, 0:1] if backward else b_row[:, L - 1:L]

        inter = b_row + m
        m_row = jnp.maximum(inter, b_row + c_max)
        mask = (si >= ti) if backward else (si <= ti)
        decay = jnp.where(mask, jnp.exp2(c_col + (b_row - m_row)), 0.0)
        w_row = jnp.exp2(inter - m_row)

        p_t = (scores_t(k_ref, qt_ref, hh, c) * decay).astype(BF16)
        lhs = jnp.concatenate([caug.astype(BF16), vt_aug], axis=1)
        rhs = jnp.concatenate([qt * w_row.astype(BF16), p_t], axis=0)
        out = jnp.dot(lhs, rhs, preferred_element_type=F32)
        den = out[dv:dv + 1, :]
        hs_ref[hh * dv:(hh + 1) * dv, tok] = out[0:dv, :] * (1.0 / jnp.maximum(jnp.abs(den), jnp.exp2(-m_row)))

        g_row = b_end + c_row
        m_new = jnp.maximum(b_end + m, jnp.max(g_row, axis=-1, keepdims=True))
        a = jnp.exp2(b_end + m - m_new)
        ws = jnp.exp2(g_row - m_new)
        caug_new = a * caug + jnp.dot(vt_aug * ws.astype(BF16), kb, preferred_element_type=F32)
        return caug_new, m_new

    def finish_chunk(hf_ref, hb_ref, ogt_ref, o_ref, c):
        tok = slice(c * L, (c + 1) * L)
        for hh in range(ML_HB):
            feat = slice(hh * dv, (hh + 1) * dv)
            hs = hf_ref[feat, tok] + hb_ref[feat, tok]
            scale = lax.rsqrt(jnp.mean(hs * hs, axis=0, keepdims=True) + NORM_EPS)
            head_g = jnp.concatenate([hg_ref[feat, :]] * (L // LANES), axis=1)
            y_t = hs * scale * head_g * jax.nn.sigmoid(ogt_ref[feat, tok].astype(F32))
            o_ref[tok, feat] = y_t.T.astype(o_ref.dtype)

    ctx_f = (kc_ref, qtc_ref, vtc_ref, grc_ref, hf_c)
    ctx_b = (kc_ref, qtc_ref, vtc_ref, grc_ref, hb_c)
    lat_f = (kl_ref, qtl_ref, vtl_ref, grl_ref, hf_l)
    lat_b = (kl_ref, qtl_ref, vtl_ref, grl_ref, hb_l)
    steps = [(ctx_f, s, ctx_b, ncc - 1 - s) for s in range(ncc)] \
        + [(lat_f, s, lat_b, ncl - 1 - s) for s in range(ncl)]

    def issue_scores(step):
        refs_f, c_f, refs_b, c_b = step
        for hh in range(ML_HB):
            scores_t(refs_f[0], refs_f[1], hh, c_f)
            scores_t(refs_b[0], refs_b[1], hh, c_b)

    zero = (jnp.zeros((dv + ML_AUG, dk), F32), jnp.zeros((1, 1), F32))
    states = [zero] * (2 * ML_HB)
    issue_scores(steps[0])
    for n, (refs_f, c_f, refs_b, c_b) in enumerate(steps):
        if n + 1 < len(steps):
            issue_scores(steps[n + 1])
        nxt = []
        for hh in range(ML_HB):
            nxt.append(chunk(refs_f, c_f, hh, False, states[2 * hh]))
            nxt.append(chunk(refs_b, c_b, hh, True, states[2 * hh + 1]))
        states = nxt
        if c_f >= c_b:
            ogt_ref, o_ref = (ogtc_ref, oc_ref) if refs_f is ctx_f else (ogtl_ref, ol_ref)
            for c in sorted({c_f, c_b}):
                finish_chunk(refs_f[4], refs_b[4], ogt_ref, o_ref, c)


def _mlstm(k_ctx, ft_ctx, g_ctx, k_lat, ft_lat, g_lat, head_g, nb, d):
    ctx_len = k_ctx.shape[0] // nb
    seq = k_lat.shape[0] // nb
    dv = ML_HB * (d // ML_HEADS)
    dk = dv // 2
    nh = ML_HEADS // ML_HB
    q_rows = ML_HEADS * (d // ML_HEADS // 2)
    v_blk = q_rows // dv
    og_blk = (q_rows + d) // dv
    L = math.gcd(math.gcd(ctx_len, seq), ML_BLOCK)
    assert L % LANES == 0
    grc = _gate_scan(g_ctx, nb, ctx_len, L)
    grl = _gate_scan(g_lat, nb, seq, L)
    hg = jnp.broadcast_to(head_g.reshape(d, 1), (d, LANES))

    def stream_specs(t):
        return [
            pl.BlockSpec((t, dk), lambda b, i: (b, i)),
            pl.BlockSpec((dk, t), lambda b, i: (i, b)),
            pl.BlockSpec((dv, t), lambda b, i: (v_blk + i, b)),
            pl.BlockSpec((dv, t), lambda b, i: (og_blk + i, b)),
            pl.BlockSpec((None, ML_HB, t // L, GATE_ROWS, L), lambda b, i: (b, i, 0, 0, 0)),
        ]

    t_all = ctx_len + seq
    gate_tiles = 2 * ML_HB * (t_all // L) * GATE_ROWS * L * 4
    vmem = 2 * t_all * (2 * dk * 2 + 3 * dv * 2) + 2 * t_all * dv * 4 + gate_tiles + 12 * 1024 * 1024
    return pl.pallas_call(
        _mlstm_kernel,
        grid=(nb, nh),
        in_specs=stream_specs(ctx_len) + stream_specs(seq) + [pl.BlockSpec((dv, LANES), lambda b, i: (i, 0))],
        out_specs=[pl.BlockSpec((ctx_len, dv), lambda b, i: (b, i)),
                   pl.BlockSpec((seq, dv), lambda b, i: (b, i))],
        out_shape=[jax.ShapeDtypeStruct((nb * ctx_len, d), BF16),
                   jax.ShapeDtypeStruct((nb * seq, d), BF16)],
        scratch_shapes=[pltpu.VMEM((dv, ctx_len), F32), pltpu.VMEM((dv, ctx_len), F32),
                        pltpu.VMEM((dv, seq), F32), pltpu.VMEM((dv, seq), F32)],
        compiler_params=_params(2, vmem),
        name="mlstm",
    )(k_ctx, ft_ctx, ft_ctx, ft_ctx, grc, k_lat, ft_lat, ft_lat, ft_lat, grl, hg)


def _attn_kernel(q_ref, kl_ref, vl_ref, kc_ref, vc_ref, lam_ref, gs_ref, o_ref, *, lam_init):
    dh = q_ref.shape[1] // 2
    seq = q_ref.shape[0]
    ctx_len = kc_ref.shape[0]
    lq = lam_ref[...]
    lam = jnp.exp(jnp.sum(lq[0:1] * lq[1:2], axis=-1, keepdims=True)) \
        - jnp.exp(jnp.sum(lq[2:3] * lq[3:4], axis=-1, keepdims=True)) + lam_init

    tq = _tile(seq, ATTN_TQ)
    key_sets = ((kl_ref, vl_ref), (kc_ref, vc_ref))

    def scores(t):
        rows = slice(t * tq, (t + 1) * tq)
        return [[_dot_nt(q_ref[rows, c * dh:(c + 1) * dh], k_ref[:, c * dh:(c + 1) * dh]) for c in range(2)]
                for k_ref, _ in key_sets]

    def lane_slabs(x):
        return [x[:, c:c + LANES] for c in range(0, x.shape[1], LANES)]

    def softmax_terms(s_sets):
        mx = functools.reduce(jnp.maximum, [slab for s in s_sets for slab in lane_slabs(s)])
        mx = jnp.max(mx, axis=-1, keepdims=True)
        p_sets = [jnp.exp2((s - mx).astype(BF16)) for s in s_sets]
        slabs = [slab for p in p_sets for slab in lane_slabs(p)]
        if len(slabs) % 2 == 0:
            slabs = [a + b for a, b in zip(slabs[0::2], slabs[1::2])]
        total = functools.reduce(jnp.add, [x.astype(F32) for x in slabs])
        return p_sets, 1.0 / jnp.sum(total, axis=-1, keepdims=True)

    n_tiles = seq // tq
    pending = [scores(t) for t in range(min(ATTN_AHEAD, n_tiles))]
    for t in range(n_tiles):
        if t + ATTN_AHEAD < n_tiles:
            pending.append(scores(t + ATTN_AHEAD))
        s = pending.pop(0)
        p0, r0 = softmax_terms([s[ks][0] for ks in range(2)])
        p1, r1 = softmax_terms([s[ks][1] for ks in range(2)])
        w0 = r0.astype(BF16)
        w1 = (lam * r1).astype(BF16)
        o = sum(jnp.dot(p0[ks] * w0 - p1[ks] * w1, v_ref[...], preferred_element_type=F32)
                for ks, (_, v_ref) in enumerate(key_sets))
        scale = lax.rsqrt(jnp.mean(o * o, axis=-1, keepdims=True) + NORM_EPS)
        o_ref[t * tq:(t + 1) * tq, :] = (o * scale * gs_ref[...] * (1.0 - lam_init)).astype(o_ref.dtype)


def _attn(p_lat, p_ctx, lam_qk, g_sub, nb, d, lam_init):
    seq = p_lat.shape[0] // nb
    ctx_len = p_ctx.shape[0] // nb
    h = DA_HEADS
    hd = d // h
    tq = _tile(seq, ATTN_TQ)
    vmem = 2 * (4 * seq + 2 * ctx_len) * hd * 2 + (ATTN_AHEAD + 3) * 2 * tq * (seq + ctx_len) * 4 \
        + 32 * tq * hd * 4 + 8 * 1024 * 1024
    return pl.pallas_call(
        functools.partial(_attn_kernel, lam_init=lam_init),
        grid=(nb, h),
        in_specs=[
            pl.BlockSpec((seq, hd), lambda b, i: (b, i)),
            pl.BlockSpec((seq, hd), lambda b, i: (b, h + i)),
            pl.BlockSpec((seq, hd), lambda b, i: (b, 2 * h + i)),
            pl.BlockSpec((ctx_len, hd), lambda b, i: (b, i)),
            pl.BlockSpec((ctx_len, hd), lambda b, i: (b, h + i)),
            pl.BlockSpec(lam_qk.shape, lambda b, i: (0, 0)),
            pl.BlockSpec((1, hd), lambda b, i: (0, 0)),
        ],
        out_specs=pl.BlockSpec((seq, hd), lambda b, i: (b, i)),
        out_shape=jax.ShapeDtypeStruct((nb * seq, d), BF16),
        compiler_params=_params(2, vmem),
        name="attn",
    )(p_lat, p_lat, p_lat, p_ctx, p_ctx, lam_qk, g_sub)


def _rope_tables(seq, dh):
    nf = dh // 4
    n = jnp.arange(seq)
    row = (n // GRID_W).astype(F32)
    col = (n % GRID_W).astype(F32)
    freq = ROPE_BASE ** (-jnp.arange(nf, dtype=F32) / nf)
    ar = row[:, None] * freq
    ac = col[:, None] * freq
    cos = jnp.concatenate([jnp.cos(ar), jnp.cos(ar), jnp.cos(ac), jnp.cos(ac)], axis=1)
    sin = jnp.concatenate([-jnp.sin(ar), jnp.sin(ar), -jnp.sin(ac), jnp.sin(ac)], axis=1)
    return cos, sin


def kernel(x, c, ctx, c_ctx, ada_w, ada_b, norm_g, mlp_w1, mlp_w2, ml_w_in, ml_b_gates, ml_head_g,
           ml_w_out, da_w_in, da_lambda, da_sub_g, da_w_out):
    nb, seq, d = x.shape
    ctx_len = ctx.shape[1]
    depth = ada_w.shape[0]
    assert depth == 2 and nb < COND_ROWS and d // ML_HEADS == 2 * LANES
    assert seq % LANES == 0 and ctx_len % LANES == 0 and seq % GRID_W == 0

    xl = x.reshape(nb * seq, d)
    xc = ctx.reshape(nb * ctx_len, d)

    cond = jnp.zeros((COND_ROWS, d), F32).at[:nb].set(c).at[nb].set(c_ctx)
    mods = _ada(cond, ada_w, ada_b).reshape(depth, COND_ROWS, 1, N_MOD * d)

    tm = _tile(seq, 512)
    tm_mlp = _tile(seq, 1024)
    tmc = _tile(nb * ctx_len, 512)

    def lat_row_of(tile_rows):
        return lambda i: i // (seq // tile_rows)

    lat_row = lat_row_of(tm)
    lat_row_mlp = lat_row_of(tm_mlp)

    def ctx_row(i):
        return nb

    g = norm_g.reshape(depth, 4, 1, d)

    ml_qk = ML_HEADS * (d // ML_HEADS // 2)
    n_main = 2 * ml_qk + 2 * d
    w_in = ml_w_in[0]
    q_scale = (d // ML_HEADS // 2) ** -0.5
    w_k = w_in[:, ml_qk:2 * ml_qk].astype(BF16)
    w_feat_t = jnp.concatenate([w_in[:, :ml_qk] * q_scale, w_in[:, 2 * ml_qk:n_main]], axis=1).T.astype(BF16)
    n_gate = 4 * ML_HEADS
    w_gate = jnp.pad(w_in[:, n_main:], ((0, 0), (0, LANES - n_gate))).astype(BF16)
    b_gate = jnp.pad(ml_b_gates[0], (0, LANES - n_gate)).reshape(1, LANES)
    w_out0 = ml_w_out[0].astype(BF16)
    w1_0 = mlp_w1[0].astype(BF16)
    w2_0 = mlp_w2[0].astype(BF16)

    k_lat, ft_lat, g_lat = _proj(xl, mods, 0, lat_row_mlp, g[0, 0], w_k, w_feat_t, tm=tm_mlp, tn=1024,
                                 gates=(w_gate, b_gate))
    k_ctx, ft_ctx, g_ctx = _proj(xc, mods, 0, ctx_row, g[0, 0], w_k, w_feat_t, tm=tmc, tn=1024,
                                 gates=(w_gate, b_gate))
    y_ctx, y_lat = _mlstm(k_ctx, ft_ctx, g_ctx, k_lat, ft_lat, g_lat, ml_head_g[0], nb, d)
    xl = _outproj(y_lat, w_out0, xl, mods, 0, lat_row, g[0, 1], tm=tm)
    xc = _outproj(y_ctx, w_out0, xc, mods, 0, ctx_row, g[0, 1], tm=tmc)
    xl = _mlp(xl, mods, 0, lat_row_mlp, g[0, 2], g[0, 3], w1_0, w2_0, tm=tm_mlp, tf=512)
    xc = _mlp(xc, mods, 0, ctx_row, g[0, 2], g[0, 3], w1_0, w2_0, tm=tmc, tf=512)

    dh = d // (2 * DA_HEADS)
    w_in = da_w_in[0]
    w_qkv = jnp.concatenate([w_in[:, :d] * (dh ** -0.5 * LOG2E), w_in[:, d:]], axis=1).astype(BF16)
    w_kv = w_in[:, d:].astype(BF16)
    w_out1 = da_w_out[0].astype(BF16)
    w1_1 = mlp_w1[1].astype(BF16)
    w2_1 = mlp_w2[1].astype(BF16)
    layer_idx = 1
    lam_init = 0.8 - 0.6 * math.exp(-0.3 * layer_idx)

    p_lat = _proj(xl, mods, 1, lat_row_mlp, g[1, 0], w_qkv, tm=tm_mlp, tn=1024,
                  rope=_rope_tables(seq, dh), rope_cols=2 * d)
    p_ctx = _proj(xc, mods, 1, ctx_row, g[1, 0], w_kv, tm=tmc, tn=1024)
    y_lat = _attn(p_lat, p_ctx, da_lambda[0], da_sub_g[0].reshape(1, 2 * dh), nb, d, lam_init)
    xl = _outproj(y_lat, w_out1, xl, mods, 1, lat_row, g[1, 1], tm=tm)
    xl = _mlp(xl, mods, 1, lat_row_mlp, g[1, 2], g[1, 3], w1_1, w2_1, tm=tm_mlp, tf=512)
    return xl.reshape(nb, seq, d)
```

```python
import functools
import math

import jax
import jax.numpy as jnp
from jax import lax
from jax.experimental import pallas as pl
from jax.experimental.pallas import tpu as pltpu

F32 = jnp.float32
BF16 = jnp.bfloat16

NORM_EPS = 1e-6
GRID_W = 64
ROPE_BASE = 10000.0
LOG2E = math.log2(math.e)
ML_HEADS = 8
ML_BLOCK = 256
ML_HB = 2
ML_AUG = 16
DA_HEADS = 8
EDGE_ROWS = 256
ATTN_TQ = 256
ATTN_AHEAD = 2
N_MOD = 6
COND_ROWS = 32
LANES = 128
MXU_COLS = 256
GATE_ROWS = 8

VMEM_CAP_BYTES = 56 * 1024 * 1024


def _vmem_limit(nbytes):
    return int(min(VMEM_CAP_BYTES, max(16 * 1024 * 1024, nbytes)))


def _params(n_grid, vmem_bytes):
    return pltpu.CompilerParams(
        dimension_semantics=("arbitrary",) * n_grid,
        vmem_limit_bytes=_vmem_limit(vmem_bytes))


def _tile(n, want):
    if n <= want:
        return n
    t = want
    while t >= 8:
        if n % t == 0 and t % 8 == 0:
            return t
        t -= 8
    return n


def _dot_nt(a, b):
    return lax.dot_general(a, b, (((1,), (1,)), ((), ())), preferred_element_type=F32)


def _ada_kernel(c_ref, w_ref, b_ref, o_ref):
    cf = c_ref[...]
    s = (cf * jax.nn.sigmoid(cf)).astype(BF16)
    o_ref[...] = jnp.dot(s, w_ref[...].astype(BF16), preferred_element_type=F32) + b_ref[...]


def _ada(cond, ada_w, ada_b):
    depth, d, n = ada_w.shape
    tn = _tile(n, 1024)
    return pl.pallas_call(
        _ada_kernel,
        grid=(depth, n // tn),
        in_specs=[
            pl.BlockSpec((COND_ROWS, d), lambda l, j: (0, 0)),
            pl.BlockSpec((None, d, tn), lambda l, j: (l, 0, j)),
            pl.BlockSpec((None, 1, tn), lambda l, j: (l, 0, j)),
        ],
        out_specs=pl.BlockSpec((None, COND_ROWS, tn), lambda l, j: (l, 0, j)),
        out_shape=jax.ShapeDtypeStruct((depth, COND_ROWS, n), F32),
        compiler_params=_params(2, 2 * d * tn * 4 + 4 * COND_ROWS * (d + tn) * 4 + d * tn * 2),
        name="ada",
    )(cond, ada_w, ada_b.reshape(depth, 1, n))


def _mod_spec(layer, piece, row_fn, d, n_grid):
    if n_grid == 1:
        return pl.BlockSpec((None, None, 1, d), lambda i: (layer, row_fn(i), 0, piece))
    return pl.BlockSpec((None, None, 1, d), lambda i, j: (layer, row_fn(i), 0, piece))


def _modulated(x_ref, g_ref, sh_ref, sc_ref):
    xf = x_ref[...]
    ms = jnp.mean(xf * xf, axis=-1, keepdims=True)
    gain = g_ref[...] * (1.0 + sc_ref[...])
    return (xf * lax.rsqrt(ms + NORM_EPS) * gain + sh_ref[...]).astype(BF16)


def _proj_kernel(*refs, n_tok, n_rope_tok, with_feat, with_gates):
    x_ref, sh_ref, sc_ref, g_ref, wk_ref = refs[:5]
    n = 5
    if with_feat:
        wt_ref = refs[n]
        n += 1
    if with_gates:
        wg_ref, bg_ref = refs[n:n + 2]
        n += 2
    if n_rope_tok:
        cos_ref, sin_ref = refs[n:n + 2]
        n += 2
    k_ref = refs[n]
    n += 1
    if with_feat:
        ft_ref = refs[n]
        n += 1
    if with_gates:
        gate_ref = refs[n]
    h_scr = refs[-1]
    j = pl.program_id(1)
    tm = x_ref.shape[0]
    sub = _tile(tm, EDGE_ROWS)

    def token_tile(rows, rotate):
        if not rotate:
            k_ref[rows, :] = jnp.dot(h_scr[rows, :], wk_ref[...], preferred_element_type=F32).astype(k_ref.dtype)
            return
        cos = cos_ref[rows, :]
        sin = sin_ref[rows, :]
        def piece(c):
            return jnp.dot(h_scr[rows, :], wk_ref[:, c * MXU_COLS:(c + 1) * MXU_COLS], preferred_element_type=F32)

        n_pieces = wk_ref.shape[1] // MXU_COLS
        nxt = piece(0)
        for c in range(n_pieces):
            acc = nxt
            if c + 1 < n_pieces:
                nxt = piece(c + 1)
            for half in range(MXU_COLS // LANES):
                a = acc[:, half * LANES:(half + 1) * LANES]
                r = a * cos + pltpu.roll(a, LANES // 2, axis=1) * sin
                lo = c * MXU_COLS + half * LANES
                k_ref[rows, lo:lo + LANES] = r.astype(k_ref.dtype)

    @pl.when(j == 0)
    def _():
        for r0 in range(0, tm, sub):
            rows = slice(r0, r0 + sub)
            h_scr[rows, :] = _modulated(x_ref.at[rows, :], g_ref, sh_ref, sc_ref)
            if with_gates:
                gate_ref[rows, :] = jnp.dot(h_scr[rows, :], wg_ref[...], preferred_element_type=F32) + bg_ref[...]
            token_tile(rows, n_rope_tok > 0)

    if n_rope_tok > 1:
        @pl.when(jnp.logical_and(j > 0, j < n_rope_tok))
        def _():
            token_tile(slice(0, tm), True)

    if n_tok > max(n_rope_tok, 1):
        @pl.when(jnp.logical_and(j >= max(n_rope_tok, 1), j < n_tok))
        def _():
            token_tile(slice(0, tm), False)

    if with_feat:
        @pl.when(j >= n_tok)
        def _():
            ft_ref[...] = _dot_nt(wt_ref[...], h_scr[...]).astype(ft_ref.dtype)


def _proj(x, mods, layer, row_fn, g, wk, wt=None, *, tm, tn, gates=None, rope=None, rope_cols=0):
    m, d = x.shape
    nk = wk.shape[1]
    tm = _tile(m, tm)
    tnk = _tile(nk, tn)
    n_tok = nk // tnk
    assert rope_cols % tnk == 0

    def tok_tile(j):
        return jnp.minimum(j, n_tok - 1)

    def feat_tile(j):
        return jnp.maximum(j - n_tok, 0)

    in_specs = [
        pl.BlockSpec((tm, d), lambda i, j: (i, 0)),
        _mod_spec(layer, 0, row_fn, d, 2),
        _mod_spec(layer, 1, row_fn, d, 2),
        pl.BlockSpec((1, d), lambda i, j: (0, 0)),
        pl.BlockSpec((d, tnk), lambda i, j: (0, tok_tile(j))),
    ]
    args = [x, mods, mods, g, wk]
    out_specs = [pl.BlockSpec((tm, tnk), lambda i, j: (i, tok_tile(j)))]
    out_shape = [jax.ShapeDtypeStruct((m, nk), BF16)]
    n_feat = tnf = 0
    if wt is not None:
        nf = wt.shape[0]
        tnf = _tile(nf, tn)
        n_feat = nf // tnf
        in_specs.append(pl.BlockSpec((tnf, d), lambda i, j: (feat_tile(j), 0)))
        args.append(wt)
        out_specs.append(pl.BlockSpec((tnf, tm), lambda i, j: (feat_tile(j), i)))
        out_shape.append(jax.ShapeDtypeStruct((nf, m), BF16))
    if gates is not None:
        in_specs += [pl.BlockSpec((d, LANES), lambda i, j: (0, 0)),
                     pl.BlockSpec((1, LANES), lambda i, j: (0, 0))]
        args += list(gates)
        out_specs.append(pl.BlockSpec((tm, LANES), lambda i, j: (i, 0)))
        out_shape.append(jax.ShapeDtypeStruct((m, LANES), F32))
    if rope is not None:
        n_pos_tiles = rope[0].shape[0] // tm
        assert rope[0].shape[0] % tm == 0
        in_specs += [pl.BlockSpec((tm, LANES), lambda i, j: (i % n_pos_tiles, 0)),
                     pl.BlockSpec((tm, LANES), lambda i, j: (i % n_pos_tiles, 0))]
        args += list(rope)
    vmem = 2 * tm * d * 4 + tm * d * 2 + 2 * d * tnk * 2 + 2 * tnf * d * 2 + 2 * tm * tnk * 2 + 2 * tnf * tm * 2 \
        + 2 * tm * max(tnk, tnf) * 4 + 8 * tm * LANES * 4 + 2 * d * LANES * 2 + 3 * tm * d * 4
    out = pl.pallas_call(
        functools.partial(_proj_kernel, n_tok=n_tok, n_rope_tok=rope_cols // tnk if rope is not None else 0,
                          with_feat=wt is not None, with_gates=gates is not None),
        grid=(m // tm, n_tok + n_feat),
        in_specs=in_specs,
        out_specs=out_specs,
        out_shape=out_shape,
        scratch_shapes=[pltpu.VMEM((tm, d), BF16)],
        compiler_params=_params(2, vmem),
        name="proj",
    )(*args)
    return out if len(out) > 1 else out[0]


def _outproj_kernel(y_ref, w_ref, x_ref, gate_ref, g_ref, o_ref):
    t = jnp.dot(y_ref[...], w_ref[...], preferred_element_type=F32)
    ms = jnp.mean(t * t, axis=-1, keepdims=True)
    o_ref[...] = x_ref[...] + gate_ref[...] * (t * lax.rsqrt(ms + NORM_EPS) * g_ref[...])


def _outproj(y, w, x, mods, layer, row_fn, g, *, tm):
    m, d = x.shape
    kdim = y.shape[1]
    tm = _tile(m, tm)
    vmem = 2 * tm * kdim * 2 + 2 * kdim * d * 2 + 4 * tm * d * 4 + 3 * tm * d * 4
    return pl.pallas_call(
        _outproj_kernel,
        grid=(m // tm,),
        in_specs=[
            pl.BlockSpec((tm, kdim), lambda i: (i, 0)),
            pl.BlockSpec((kdim, d), lambda i: (0, 0)),
            pl.BlockSpec((tm, d), lambda i: (i, 0)),
            _mod_spec(layer, 2, row_fn, d, 1),
            pl.BlockSpec((1, d), lambda i: (0, 0)),
        ],
        out_specs=pl.BlockSpec((tm, d), lambda i: (i, 0)),
        out_shape=jax.ShapeDtypeStruct((m, d), F32),
        compiler_params=_params(1, vmem),
        name="outproj",
    )(y, w, x, mods, g)


def _mlp_kernel(x_ref, sh_ref, sc_ref, gate_ref, g_in_ref, g_out_ref, w1_ref, w2_ref, o_ref, h_scr):
    j = pl.program_id(1)
    last = pl.num_programs(1) - 1
    tm = o_ref.shape[0]
    sub = _tile(tm, EDGE_ROWS)

    def hidden(h):
        a = jnp.dot(h, w1_ref[...], preferred_element_type=F32)
        a = jnp.maximum(a, 0.0)
        return jnp.dot((a * a).astype(BF16), w2_ref[...], preferred_element_type=F32)

    @pl.when(j == 0)
    def _():
        for r0 in range(0, tm, sub):
            rows = slice(r0, r0 + sub)
            h_scr[rows, :] = _modulated(x_ref.at[rows, :], g_in_ref, sh_ref, sc_ref)
            o_ref[rows, :] = hidden(h_scr[rows, :])

    @pl.when(jnp.logical_and(j > 0, j < last))
    def _():
        o_ref[...] += hidden(h_scr[...])

    @pl.when(j == last)
    def _():
        for r0 in range(0, tm, sub):
            rows = slice(r0, r0 + sub)
            f = o_ref[rows, :] + hidden(h_scr[rows, :])
            ms = jnp.mean(f * f, axis=-1, keepdims=True)
            o_ref[rows, :] = x_ref[rows, :] + gate_ref[...] * (f * lax.rsqrt(ms + NORM_EPS) * g_out_ref[...])


def _mlp(x, mods, layer, row_fn, g_in, g_out, w1, w2, *, tm, tf):
    m, d = x.shape
    dff = w1.shape[1]
    tm = _tile(m, tm)
    tf = _tile(dff, tf)
    vmem = 4 * tm * d * 4 + tm * d * 2 + 4 * d * tf * 2 + 3 * tm * tf * 4 + 4 * 1024 * 1024
    return pl.pallas_call(
        _mlp_kernel,
        grid=(m // tm, dff // tf),
        in_specs=[
            pl.BlockSpec((tm, d), lambda i, j: (i, 0)),
            _mod_spec(layer, 3, row_fn, d, 2),
            _mod_spec(layer, 4, row_fn, d, 2),
            _mod_spec(layer, 5, row_fn, d, 2),
            pl.BlockSpec((1, d), lambda i, j: (0, 0)),
            pl.BlockSpec((1, d), lambda i, j: (0, 0)),
            pl.BlockSpec((d, tf), lambda i, j: (0, j)),
            pl.BlockSpec((tf, d), lambda i, j: (j, 0)),
        ],
        out_specs=pl.BlockSpec((tm, d), lambda i, j: (i, 0)),
        out_shape=jax.ShapeDtypeStruct((m, d), F32),
        scratch_shapes=[pltpu.VMEM((tm, d), BF16)],
        compiler_params=_params(2, vmem),
        name="mlp",
    )(x, mods, mods, mods, g_in, g_out, w1, w2)


def _split3(x):
    x1 = x.astype(BF16)
    r1 = x - x1.astype(F32)
    x2 = r1.astype(BF16)
    x3 = (r1 - x2.astype(F32)).astype(BF16)
    return x1, x2, x3


def _log_sigmoid(x):
    return jnp.minimum(x, 0.0) - jnp.log(1.0 + jnp.exp(-jnp.abs(x)))


def _running_max_lanes(x, reverse):
    lane = lax.broadcasted_iota(jnp.int32, (x.shape[0], LANES), 1)
    slabs = [x[:, j:j + LANES] for j in range(0, x.shape[1], LANES)]
    order = range(len(slabs) - 1, -1, -1) if reverse else range(len(slabs))
    carry = None
    for j in order:
        y = slabs[j]
        k = 1
        while k < LANES:
            if reverse:
                y = jnp.maximum(y, jnp.where(lane < LANES - k, pltpu.roll(y, LANES - k, axis=1), -jnp.inf))
            else:
                y = jnp.maximum(y, jnp.where(lane >= k, pltpu.roll(y, k, axis=1), -jnp.inf))
            k *= 2
        if carry is not None:
            y = jnp.maximum(y, carry)
        carry = jnp.max(y, axis=-1, keepdims=True)
        slabs[j] = y
    return jnp.concatenate(slabs, axis=1)


def _gate_scan_kernel(x_ref, o_ref):
    L = x_ref.shape[-1]
    ui = lax.broadcasted_iota(jnp.int32, (L, L), 0)
    si = lax.broadcasted_iota(jnp.int32, (L, L), 1)

    def cumulative(z, tri):
        return sum(jnp.dot(p, tri.astype(BF16), preferred_element_type=F32) for p in _split3(z))

    b_f = cumulative(_log_sigmoid(x_ref[1]), ui <= si) * LOG2E
    b_b = cumulative(_log_sigmoid(x_ref[3]), ui >= si) * LOG2E
    c_f = x_ref[0] * LOG2E - b_f
    c_b = x_ref[2] * LOG2E - b_b
    o_ref[0] = b_f
    o_ref[1] = c_f
    o_ref[2] = _running_max_lanes(c_f, False)
    o_ref[3] = b_b
    o_ref[4] = c_b
    o_ref[5] = _running_max_lanes(c_b, True)


def _gate_scan(gates, nb, t, L):
    nc = t // L
    g = gates[:, :4 * ML_HEADS].reshape(nb, nc, L, 4, ML_HEADS)
    g = g.transpose(3, 0, 4, 1, 2).reshape(4, nb * ML_HEADS * nc, L)
    nr = g.shape[1]
    tr = _tile(nr, 256)
    n_out = 6
    s = pl.pallas_call(
        _gate_scan_kernel,
        grid=(nr // tr,),
        in_specs=[pl.BlockSpec((4, tr, L), lambda i: (0, i, 0))],
        out_specs=pl.BlockSpec((n_out, tr, L), lambda i: (0, i, 0)),
        out_shape=jax.ShapeDtypeStruct((n_out, nr, L), F32),
        compiler_params=_params(1, 48 * tr * max(L, LANES) * 4),
        name="gate_scan",
    )(g)
    s = s.reshape(n_out, nb, ML_HEADS, nc, L)
    rows = jnp.pad(s.transpose(1, 2, 3, 0, 4), ((0, 0), (0, 0), (0, 0), (0, GATE_ROWS - n_out), (0, 0)))
    return rows


def _mlstm_kernel(kc_ref, qtc_ref, vtc_ref, ogtc_ref, grc_ref,
                  kl_ref, qtl_ref, vtl_ref, ogtl_ref, grl_ref, hg_ref,
                  oc_ref, ol_ref, hf_c, hb_c, hf_l, hb_l):
    L = grc_ref.shape[-1]
    dk = kc_ref.shape[1] // ML_HB
    dv = vtc_ref.shape[0] // ML_HB
    ncc = kc_ref.shape[0] // L
    ncl = kl_ref.shape[0] // L

    si = lax.broadcasted_iota(jnp.int32, (L, L), 0)
    ti = lax.broadcasted_iota(jnp.int32, (L, L), 1)
    ones_rows = (lax.broadcasted_iota(jnp.int32, (ML_AUG, L), 0) == 0).astype(BF16)

    gate_tiles = {}
    for gr_ref, nc in ((grc_ref, ncc), (grl_ref, ncl)):
        for hh in range(ML_HB):
            for c in range(nc):
                tile = gr_ref[hh, c]
                gate_tiles[(id(gr_ref), hh, c)] = (tile, tile.T)

    st_cache = {}

    def scores_t(k_ref, qt_ref, hh, c):
        key = (id(k_ref), hh, c)
        if key not in st_cache:
            tok = slice(c * L, (c + 1) * L)
            st_cache[key] = jnp.dot(k_ref[tok, hh * dk:(hh + 1) * dk], qt_ref[hh * dk:(hh + 1) * dk, tok],
                                    preferred_element_type=F32)
        return st_cache[key]

    delta_cache = {}

    def state_delta(refs, c, hh, backward):
        k_ref, _, vt_ref, gr_ref, _ = refs
        key = (id(k_ref), hh, c, backward)
        if key not in delta_cache:
            tok = slice(c * L, (c + 1) * L)
            rows, _ = gate_tiles[(id(gr_ref), hh, c)]
            r = 3 if backward else 0
            b_row, c_row, c_max = rows[r:r + 1, :], rows[r + 1:r + 2, :], rows[r + 2:r + 3, :]
            b_end = b_row[:, 0:1] if backward else b_row[:, L - 1:L]
            g_max = b_end + (c_max[:, 0:1] if backward else c_max[:, L - 1:L])
            ws = jnp.exp2(b_end + c_row - g_max).astype(BF16)
            vt_aug = jnp.concatenate([vt_ref[hh * dv:(hh + 1) * dv, tok], ones_rows], axis=0)
            delta = jnp.dot(vt_aug * ws, k_ref[tok, hh * dk:(hh + 1) * dk], preferred_element_type=F32)
            delta_cache[key] = (delta, g_max)
        return delta_cache[key]

    def chunk(refs, c, hh, backward, state):
        k_ref, qt_ref, vt_ref, gr_ref, hs_ref = refs
        caug, m = state
        tok = slice(c * L, (c + 1) * L)
        qt = qt_ref[hh * dk:(hh + 1) * dk, tok]
        vt_aug = jnp.concatenate([vt_ref[hh * dv:(hh + 1) * dv, tok], ones_rows], axis=0)
        rows, cols = gate_tiles[(id(gr_ref), hh, c)]
        r = 3 if backward else 0
        b_row = rows[r:r + 1, :]
        c_max = rows[r + 2:r + 3, :]
        c_col = cols[:, r + 1:r + 2]
        b_end = b_row[:, 0:1] if backward else b_row[:, L - 1:L]

        inter = b_row + m
        m_row = jnp.maximum(inter, b_row + c_max)
        mask = (si >= ti) if backward else (si <= ti)
        decay = jnp.where(mask, jnp.exp2(c_col + (b_row - m_row)), 0.0)
        w_row = jnp.exp2(inter - m_row)

        p_t = (scores_t(k_ref, qt_ref, hh, c) * decay).astype(BF16)
        lhs = jnp.concatenate([caug.astype(BF16), vt_aug], axis=1)
        rhs = jnp.concatenate([qt * w_row.astype(BF16), p_t], axis=0)
        out = jnp.dot(lhs, rhs, preferred_element_type=F32)
        den = out[dv:dv + 1, :]
        hs_ref[hh * dv:(hh + 1) * dv, tok] = out[0:dv, :] * (1.0 / jnp.maximum(jnp.abs(den), jnp.exp2(-m_row)))

        d_state, g_max = state_delta(refs, c, hh, backward)
        m_new = jnp.maximum(b_end + m, g_max)
        caug_new = jnp.exp2(b_end + m - m_new) * caug + jnp.exp2(g_max - m_new) * d_state
        return caug_new, m_new

    def finish_chunk(hf_ref, hb_ref, ogt_ref, o_ref, c):
        tok = slice(c * L, (c + 1) * L)
        for hh in range(ML_HB):
            feat = slice(hh * dv, (hh + 1) * dv)
            hs = hf_ref[feat, tok] + hb_ref[feat, tok]
            scale = lax.rsqrt(jnp.mean(hs * hs, axis=0, keepdims=True) + NORM_EPS)
            head_g = jnp.concatenate([hg_ref[feat, :]] * (L // LANES), axis=1)
            y_t = hs * scale * head_g * jax.nn.sigmoid(ogt_ref[feat, tok].astype(F32))
            o_ref[tok, feat] = y_t.T.astype(o_ref.dtype)

    ctx_f = (kc_ref, qtc_ref, vtc_ref, grc_ref, hf_c)
    ctx_b = (kc_ref, qtc_ref, vtc_ref, grc_ref, hb_c)
    lat_f = (kl_ref, qtl_ref, vtl_ref, grl_ref, hf_l)
    lat_b = (kl_ref, qtl_ref, vtl_ref, grl_ref, hb_l)
    steps = [(ctx_f, s, ctx_b, ncc - 1 - s) for s in range(ncc)] \
        + [(lat_f, s, lat_b, ncl - 1 - s) for s in range(ncl)]

    def issue_scores(step):
        refs_f, c_f, refs_b, c_b = step
        for hh in range(ML_HB):
            scores_t(refs_f[0], refs_f[1], hh, c_f)
            scores_t(refs_b[0], refs_b[1], hh, c_b)
            state_delta(refs_f, c_f, hh, False)
            state_delta(refs_b, c_b, hh, True)

    zero = (jnp.zeros((dv + ML_AUG, dk), F32), jnp.zeros((1, 1), F32))
    states = [zero] * (2 * ML_HB)
    issue_scores(steps[0])
    for n, (refs_f, c_f, refs_b, c_b) in enumerate(steps):
        if n + 1 < len(steps):
            issue_scores(steps[n + 1])
        nxt = []
        for hh in range(ML_HB):
            nxt.append(chunk(refs_f, c_f, hh, False, states[2 * hh]))
            nxt.append(chunk(refs_b, c_b, hh, True, states[2 * hh + 1]))
        states = nxt
        if c_f >= c_b:
            ogt_ref, o_ref = (ogtc_ref, oc_ref) if refs_f is ctx_f else (ogtl_ref, ol_ref)
            for c in sorted({c_f, c_b}):
                finish_chunk(refs_f[4], refs_b[4], ogt_ref, o_ref, c)


def _mlstm(k_ctx, ft_ctx, g_ctx, k_lat, ft_lat, g_lat, head_g, nb, d):
    ctx_len = k_ctx.shape[0] // nb
    seq = k_lat.shape[0] // nb
    dv = ML_HB * (d // ML_HEADS)
    dk = dv // 2
    nh = ML_HEADS // ML_HB
    q_rows = ML_HEADS * (d // ML_HEADS // 2)
    v_blk = q_rows // dv
    og_blk = (q_rows + d) // dv
    L = math.gcd(math.gcd(ctx_len, seq), ML_BLOCK)
    assert L % LANES == 0
    grc = _gate_scan(g_ctx, nb, ctx_len, L)
    grl = _gate_scan(g_lat, nb, seq, L)
    hg = jnp.broadcast_to(head_g.reshape(d, 1), (d, LANES))

    def stream_specs(t):
        return [
            pl.BlockSpec((t, dk), lambda b, i: (b, i)),
            pl.BlockSpec((dk, t), lambda b, i: (i, b)),
            pl.BlockSpec((dv, t), lambda b, i: (v_blk + i, b)),
            pl.BlockSpec((dv, t), lambda b, i: (og_blk + i, b)),
            pl.BlockSpec((None, ML_HB, t // L, GATE_ROWS, L), lambda b, i: (b, i, 0, 0, 0)),
        ]

    t_all = ctx_len + seq
    gate_tiles = 2 * ML_HB * (t_all // L) * GATE_ROWS * L * 4
    vmem = 2 * t_all * (2 * dk * 2 + 3 * dv * 2) + 2 * t_all * dv * 4 + gate_tiles + 12 * 1024 * 1024
    return pl.pallas_call(
        _mlstm_kernel,
        grid=(nb, nh),
        in_specs=stream_specs(ctx_len) + stream_specs(seq) + [pl.BlockSpec((dv, LANES), lambda b, i: (i, 0))],
        out_specs=[pl.BlockSpec((ctx_len, dv), lambda b, i: (b, i)),
                   pl.BlockSpec((seq, dv), lambda b, i: (b, i))],
        out_shape=[jax.ShapeDtypeStruct((nb * ctx_len, d), BF16),
                   jax.ShapeDtypeStruct((nb * seq, d), BF16)],
        scratch_shapes=[pltpu.VMEM((dv, ctx_len), F32), pltpu.VMEM((dv, ctx_len), F32),
                        pltpu.VMEM((dv, seq), F32), pltpu.VMEM((dv, seq), F32)],
        compiler_params=_params(2, vmem),
        name="mlstm",
    )(k_ctx, ft_ctx, ft_ctx, ft_ctx, grc, k_lat, ft_lat, ft_lat, ft_lat, grl, hg)


def _attn_kernel(q_ref, kl_ref, vl_ref, kc_ref, vc_ref, lam_ref, gs_ref, o_ref, *, lam_init):
    dh = q_ref.shape[1] // 2
    seq = q_ref.shape[0]
    ctx_len = kc_ref.shape[0]
    lq = lam_ref[...]
    lam = jnp.exp(jnp.sum(lq[0:1] * lq[1:2], axis=-1, keepdims=True)) \
        - jnp.exp(jnp.sum(lq[2:3] * lq[3:4], axis=-1, keepdims=True)) + lam_init

    tq = _tile(seq, ATTN_TQ)
    key_sets = ((kl_ref, vl_ref), (kc_ref, vc_ref))

    def scores(t):
        rows = slice(t * tq, (t + 1) * tq)
        return [[_dot_nt(q_ref[rows, c * dh:(c + 1) * dh], k_ref[:, c * dh:(c + 1) * dh]) for c in range(2)]
                for k_ref, _ in key_sets]

    def lane_slabs(x):
        return [x[:, c:c + LANES] for c in range(0, x.shape[1], LANES)]

    def softmax_terms(s_sets):
        mx = functools.reduce(jnp.maximum, [slab for s in s_sets for slab in lane_slabs(s)])
        mx = jnp.max(mx, axis=-1, keepdims=True)
        p_sets = [jnp.exp2((s - mx).astype(BF16)) for s in s_sets]
        slabs = [slab for p in p_sets for slab in lane_slabs(p)]
        if len(slabs) % 2 == 0:
            slabs = [a + b for a, b in zip(slabs[0::2], slabs[1::2])]
        total = functools.reduce(jnp.add, [x.astype(F32) for x in slabs])
        return p_sets, 1.0 / jnp.sum(total, axis=-1, keepdims=True)

    n_tiles = seq // tq
    pending = [scores(t) for t in range(min(ATTN_AHEAD, n_tiles))]
    for t in range(n_tiles):
        if t + ATTN_AHEAD < n_tiles:
            pending.append(scores(t + ATTN_AHEAD))
        s = pending.pop(0)
        p0, r0 = softmax_terms([s[ks][0] for ks in range(2)])
        p1, r1 = softmax_terms([s[ks][1] for ks in range(2)])
        w0 = r0.astype(BF16)
        w1 = (lam * r1).astype(BF16)
        o = sum(jnp.dot(p0[ks] * w0 - p1[ks] * w1, v_ref[...], preferred_element_type=F32)
                for ks, (_, v_ref) in enumerate(key_sets))
        scale = lax.rsqrt(jnp.mean(o * o, axis=-1, keepdims=True) + NORM_EPS)
        o_ref[t * tq:(t + 1) * tq, :] = (o * scale * gs_ref[...] * (1.0 - lam_init)).astype(o_ref.dtype)


def _attn(p_lat, p_ctx, lam_qk, g_sub, nb, d, lam_init):
    seq = p_lat.shape[0] // nb
    ctx_len = p_ctx.shape[0] // nb
    h = DA_HEADS
    hd = d // h
    tq = _tile(seq, ATTN_TQ)
    vmem = 2 * (4 * seq + 2 * ctx_len) * hd * 2 + (ATTN_AHEAD + 3) * 2 * tq * (seq + ctx_len) * 4 \
        + 32 * tq * hd * 4 + 8 * 1024 * 1024
    return pl.pallas_call(
        functools.partial(_attn_kernel, lam_init=lam_init),
        grid=(nb, h),
        in_specs=[
            pl.BlockSpec((seq, hd), lambda b, i: (b, i)),
            pl.BlockSpec((seq, hd), lambda b, i: (b, h + i)),
            pl.BlockSpec((seq, hd), lambda b, i: (b, 2 * h + i)),
            pl.BlockSpec((ctx_len, hd), lambda b, i: (b, i)),
            pl.BlockSpec((ctx_len, hd), lambda b, i: (b, h + i)),
            pl.BlockSpec(lam_qk.shape, lambda b, i: (0, 0)),
            pl.BlockSpec((1, hd), lambda b, i: (0, 0)),
        ],
        out_specs=pl.BlockSpec((seq, hd), lambda b, i: (b, i)),
        out_shape=jax.ShapeDtypeStruct((nb * seq, d), BF16),
        compiler_params=_params(2, vmem),
        name="attn",
    )(p_lat, p_lat, p_lat, p_ctx, p_ctx, lam_qk, g_sub)


def _rope_tables(seq, dh):
    nf = dh // 4
    n = jnp.arange(seq)
    row = (n // GRID_W).astype(F32)
    col = (n % GRID_W).astype(F32)
    freq = ROPE_BASE ** (-jnp.arange(nf, dtype=F32) / nf)
    ar = row[:, None] * freq
    ac = col[:, None] * freq
    cos = jnp.concatenate([jnp.cos(ar), jnp.cos(ac), jnp.cos(ar), jnp.cos(ac)], axis=1)
    sin = jnp.concatenate([-jnp.sin(ar), -jnp.sin(ac), jnp.sin(ar), jnp.sin(ac)], axis=1)
    return cos, sin


def _rope_order(w, dh):
    nf = dh // 4
    g = w.reshape(w.shape[0], -1, 4, nf)
    return g[:, :, jnp.array([0, 2, 1, 3]), :].reshape(w.shape)


def kernel(x, c, ctx, c_ctx, ada_w, ada_b, norm_g, mlp_w1, mlp_w2, ml_w_in, ml_b_gates, ml_head_g,
           ml_w_out, da_w_in, da_lambda, da_sub_g, da_w_out):
    nb, seq, d = x.shape
    ctx_len = ctx.shape[1]
    depth = ada_w.shape[0]
    assert depth == 2 and nb < COND_ROWS and d // ML_HEADS == 2 * LANES
    assert seq % LANES == 0 and ctx_len % LANES == 0 and seq % GRID_W == 0

    xl = x.reshape(nb * seq, d)
    xc = ctx.reshape(nb * ctx_len, d)

    cond = jnp.zeros((COND_ROWS, d), F32).at[:nb].set(c).at[nb].set(c_ctx)
    mods = _ada(cond, ada_w, ada_b).reshape(depth, COND_ROWS, 1, N_MOD * d)

    tm = _tile(seq, 512)
    tm_mlp = _tile(seq, 1024)
    tmc = _tile(nb * ctx_len, 512)

    def lat_row_of(tile_rows):
        return lambda i: i // (seq // tile_rows)

    lat_row = lat_row_of(tm)
    lat_row_mlp = lat_row_of(tm_mlp)

    def ctx_row(i):
        return nb

    g = norm_g.reshape(depth, 4, 1, d)

    ml_qk = ML_HEADS * (d // ML_HEADS // 2)
    n_main = 2 * ml_qk + 2 * d
    w_in = ml_w_in[0]
    q_scale = (d // ML_HEADS // 2) ** -0.5
    w_k = w_in[:, ml_qk:2 * ml_qk].astype(BF16)
    w_feat_t = jnp.concatenate([w_in[:, :ml_qk] * q_scale, w_in[:, 2 * ml_qk:n_main]], axis=1).T.astype(BF16)
    n_gate = 4 * ML_HEADS
    w_gate = jnp.pad(w_in[:, n_main:], ((0, 0), (0, LANES - n_gate))).astype(BF16)
    b_gate = jnp.pad(ml_b_gates[0], (0, LANES - n_gate)).reshape(1, LANES)
    w_out0 = ml_w_out[0].astype(BF16)
    w1_0 = mlp_w1[0].astype(BF16)
    w2_0 = mlp_w2[0].astype(BF16)

    k_lat, ft_lat, g_lat = _proj(xl, mods, 0, lat_row_mlp, g[0, 0], w_k, w_feat_t, tm=tm_mlp, tn=1024,
                                 gates=(w_gate, b_gate))
    k_ctx, ft_ctx, g_ctx = _proj(xc, mods, 0, ctx_row, g[0, 0], w_k, w_feat_t, tm=tmc, tn=1024,
                                 gates=(w_gate, b_gate))
    y_ctx, y_lat = _mlstm(k_ctx, ft_ctx, g_ctx, k_lat, ft_lat, g_lat, ml_head_g[0], nb, d)
    xl = _outproj(y_lat, w_out0, xl, mods, 0, lat_row, g[0, 1], tm=tm)
    xc = _outproj(y_ctx, w_out0, xc, mods, 0, ctx_row, g[0, 1], tm=tmc)
    xl = _mlp(xl, mods, 0, lat_row_mlp, g[0, 2], g[0, 3], w1_0, w2_0, tm=tm_mlp, tf=512)
    xc = _mlp(xc, mods, 0, ctx_row, g[0, 2], g[0, 3], w1_0, w2_0, tm=tmc, tf=512)

    dh = d // (2 * DA_HEADS)
    w_in = da_w_in[0]
    w_q = _rope_order(w_in[:, :d] * (dh ** -0.5 * LOG2E), dh)
    w_k = _rope_order(w_in[:, d:2 * d], dh)
    w_qkv = jnp.concatenate([w_q, w_k, w_in[:, 2 * d:]], axis=1).astype(BF16)
    w_kv = jnp.concatenate([w_k, w_in[:, 2 * d:]], axis=1).astype(BF16)
    w_out1 = da_w_out[0].astype(BF16)
    w1_1 = mlp_w1[1].astype(BF16)
    w2_1 = mlp_w2[1].astype(BF16)
    layer_idx = 1
    lam_init = 0.8 - 0.6 * math.exp(-0.3 * layer_idx)

    p_lat = _proj(xl, mods, 1, lat_row_mlp, g[1, 0], w_qkv, tm=tm_mlp, tn=1024,
                  rope=_rope_tables(seq, dh), rope_cols=2 * d)
    p_ctx = _proj(xc, mods, 1, ctx_row, g[1, 0], w_kv, tm=tmc, tn=1024)
    y_lat = _attn(p_lat, p_ctx, da_lambda[0], da_sub_g[0].reshape(1, 2 * dh), nb, d, lam_init)
    xl = _outproj(y_lat, w_out1, xl, mods, 1, lat_row, g[1, 1], tm=tm)
    xl = _mlp(xl, mods, 1, lat_row_mlp, g[1, 2], g[1, 3], w1_1, w2_1, tm=tm_mlp, tf=512)
    return xl.reshape(nb, seq, d)
```

```python
import functools
import math

import jax
import jax.numpy as jnp
from jax import lax
from jax.experimental import pallas as pl
from jax.experimental.pallas import tpu as pltpu

F32 = jnp.float32
BF16 = jnp.bfloat16

NORM_EPS = 1e-6
GRID_W = 64
ROPE_BASE = 10000.0
LOG2E = math.log2(math.e)
ML_HEADS = 8
ML_BLOCK = 256
ML_HB = 2
ML_AUG = 16
DA_HEADS = 8
EDGE_ROWS = 512
ATTN_TQ = 256
ATTN_AHEAD = 2
N_MOD = 6
COND_ROWS = 32
LANES = 128
MXU_COLS = 256
GATE_ROWS = 8

VMEM_CAP_BYTES = 56 * 1024 * 1024


def _vmem_limit(nbytes):
    return int(min(VMEM_CAP_BYTES, max(16 * 1024 * 1024, nbytes)))


def _params(n_grid, vmem_bytes):
    return pltpu.CompilerParams(
        dimension_semantics=("arbitrary",) * n_grid,
        vmem_limit_bytes=_vmem_limit(vmem_bytes))


def _tile(n, want):
    if n <= want:
        return n
    t = want
    while t >= 8:
        if n % t == 0 and t % 8 == 0:
            return t
        t -= 8
    return n


def _dot_nt(a, b):
    return lax.dot_general(a, b, (((1,), (1,)), ((), ())), preferred_element_type=F32)


def _ada_kernel(c_ref, w_ref, b_ref, o_ref):
    cf = c_ref[...]
    s = (cf * jax.nn.sigmoid(cf)).astype(BF16)
    o_ref[...] = jnp.dot(s, w_ref[...].astype(BF16), preferred_element_type=F32) + b_ref[...]


def _ada(cond, ada_w, ada_b):
    depth, d, n = ada_w.shape
    tn = _tile(n, 1024)
    return pl.pallas_call(
        _ada_kernel,
        grid=(depth, n // tn),
        in_specs=[
            pl.BlockSpec((COND_ROWS, d), lambda l, j: (0, 0)),
            pl.BlockSpec((None, d, tn), lambda l, j: (l, 0, j)),
            pl.BlockSpec((None, 1, tn), lambda l, j: (l, 0, j)),
        ],
        out_specs=pl.BlockSpec((None, COND_ROWS, tn), lambda l, j: (l, 0, j)),
        out_shape=jax.ShapeDtypeStruct((depth, COND_ROWS, n), F32),
        compiler_params=_params(2, 2 * d * tn * 4 + 4 * COND_ROWS * (d + tn) * 4 + d * tn * 2),
        name="ada",
    )(cond, ada_w, ada_b.reshape(depth, 1, n))


def _mod_spec(layer, piece, row_fn, d, n_grid):
    if n_grid == 1:
        return pl.BlockSpec((None, None, 1, d), lambda i: (layer, row_fn(i), 0, piece))
    return pl.BlockSpec((None, None, 1, d), lambda i, j: (layer, row_fn(i), 0, piece))


def _modulated(x_ref, g_ref, sh_ref, sc_ref):
    xf = x_ref[...]
    ms = jnp.mean(xf * xf, axis=-1, keepdims=True)
    gain = g_ref[...] * (1.0 + sc_ref[...])
    return (xf * lax.rsqrt(ms + NORM_EPS) * gain + sh_ref[...]).astype(BF16)


def _proj_kernel(*refs, n_tok, n_rope_tok, with_feat, with_gates):
    x_ref, sh_ref, sc_ref, g_ref, wk_ref = refs[:5]
    n = 5
    if with_feat:
        wt_ref = refs[n]
        n += 1
    if with_gates:
        wg_ref, bg_ref = refs[n:n + 2]
        n += 2
    if n_rope_tok:
        cos_ref, sin_ref = refs[n:n + 2]
        n += 2
    k_ref = refs[n]
    n += 1
    if with_feat:
        ft_ref = refs[n]
        n += 1
    if with_gates:
        gate_ref = refs[n]
    h_scr = refs[-1]
    j = pl.program_id(1)
    tm = x_ref.shape[0]
    sub = _tile(tm, EDGE_ROWS)

    def token_tile(rows, rotate):
        if not rotate:
            k_ref[rows, :] = jnp.dot(h_scr[rows, :], wk_ref[...], preferred_element_type=F32).astype(k_ref.dtype)
            return
        cos = cos_ref[rows, :]
        sin = sin_ref[rows, :]
        def piece(c):
            return jnp.dot(h_scr[rows, :], wk_ref[:, c * MXU_COLS:(c + 1) * MXU_COLS], preferred_element_type=F32)

        n_pieces = wk_ref.shape[1] // MXU_COLS
        nxt = piece(0)
        for c in range(n_pieces):
            acc = nxt
            if c + 1 < n_pieces:
                nxt = piece(c + 1)
            for half in range(MXU_COLS // LANES):
                a = acc[:, half * LANES:(half + 1) * LANES]
                r = a * cos + pltpu.roll(a, LANES // 2, axis=1) * sin
                lo = c * MXU_COLS + half * LANES
                k_ref[rows, lo:lo + LANES] = r.astype(k_ref.dtype)

    @pl.when(j == 0)
    def _():
        for r0 in range(0, tm, sub):
            rows = slice(r0, r0 + sub)
            h_scr[rows, :] = _modulated(x_ref.at[rows, :], g_ref, sh_ref, sc_ref)
            if with_gates:
                gate_ref[rows, :] = jnp.dot(h_scr[rows, :], wg_ref[...], preferred_element_type=F32) + bg_ref[...]
            token_tile(rows, n_rope_tok > 0)

    if n_rope_tok > 1:
        @pl.when(jnp.logical_and(j > 0, j < n_rope_tok))
        def _():
            token_tile(slice(0, tm), True)

    if n_tok > max(n_rope_tok, 1):
        @pl.when(jnp.logical_and(j >= max(n_rope_tok, 1), j < n_tok))
        def _():
            token_tile(slice(0, tm), False)

    if with_feat:
        @pl.when(j >= n_tok)
        def _():
            ft_ref[...] = _dot_nt(wt_ref[...], h_scr[...]).astype(ft_ref.dtype)


def _proj(x, mods, layer, row_fn, g, wk, wt=None, *, tm, tn, gates=None, rope=None, rope_cols=0):
    m, d = x.shape
    nk = wk.shape[1]
    tm = _tile(m, tm)
    tnk = _tile(nk, tn)
    n_tok = nk // tnk
    assert rope_cols % tnk == 0

    def tok_tile(j):
        return jnp.minimum(j, n_tok - 1)

    def feat_tile(j):
        return jnp.maximum(j - n_tok, 0)

    in_specs = [
        pl.BlockSpec((tm, d), lambda i, j: (i, 0)),
        _mod_spec(layer, 0, row_fn, d, 2),
        _mod_spec(layer, 1, row_fn, d, 2),
        pl.BlockSpec((1, d), lambda i, j: (0, 0)),
        pl.BlockSpec((d, tnk), lambda i, j: (0, tok_tile(j))),
    ]
    args = [x, mods, mods, g, wk]
    out_specs = [pl.BlockSpec((tm, tnk), lambda i, j: (i, tok_tile(j)))]
    out_shape = [jax.ShapeDtypeStruct((m, nk), BF16)]
    n_feat = tnf = 0
    if wt is not None:
        nf = wt.shape[0]
        tnf = _tile(nf, tn)
        n_feat = nf // tnf
        in_specs.append(pl.BlockSpec((tnf, d), lambda i, j: (feat_tile(j), 0)))
        args.append(wt)
        out_specs.append(pl.BlockSpec((tnf, tm), lambda i, j: (feat_tile(j), i)))
        out_shape.append(jax.ShapeDtypeStruct((nf, m), BF16))
    if gates is not None:
        in_specs += [pl.BlockSpec((d, LANES), lambda i, j: (0, 0)),
                     pl.BlockSpec((1, LANES), lambda i, j: (0, 0))]
        args += list(gates)
        out_specs.append(pl.BlockSpec((tm, LANES), lambda i, j: (i, 0)))
        out_shape.append(jax.ShapeDtypeStruct((m, LANES), F32))
    if rope is not None:
        n_pos_tiles = rope[0].shape[0] // tm
        assert rope[0].shape[0] % tm == 0
        in_specs += [pl.BlockSpec((tm, LANES), lambda i, j: (i % n_pos_tiles, 0)),
                     pl.BlockSpec((tm, LANES), lambda i, j: (i % n_pos_tiles, 0))]
        args += list(rope)
    vmem = 2 * tm * d * 4 + tm * d * 2 + 2 * d * tnk * 2 + 2 * tnf * d * 2 + 2 * tm * tnk * 2 + 2 * tnf * tm * 2 \
        + 2 * tm * max(tnk, tnf) * 4 + 8 * tm * LANES * 4 + 2 * d * LANES * 2 + 3 * tm * d * 4
    out = pl.pallas_call(
        functools.partial(_proj_kernel, n_tok=n_tok, n_rope_tok=rope_cols // tnk if rope is not None else 0,
                          with_feat=wt is not None, with_gates=gates is not None),
        grid=(m // tm, n_tok + n_feat),
        in_specs=in_specs,
        out_specs=out_specs,
        out_shape=out_shape,
        scratch_shapes=[pltpu.VMEM((tm, d), BF16)],
        compiler_params=_params(2, vmem),
        name="proj",
    )(*args)
    return out if len(out) > 1 else out[0]


def _outproj_kernel(y_ref, w_ref, x_ref, gate_ref, g_ref, o_ref):
    t = jnp.dot(y_ref[...], w_ref[...], preferred_element_type=F32)
    ms = jnp.mean(t * t, axis=-1, keepdims=True)
    o_ref[...] = x_ref[...] + gate_ref[...] * (t * lax.rsqrt(ms + NORM_EPS) * g_ref[...])


def _outproj(y, w, x, mods, layer, row_fn, g, *, tm):
    m, d = x.shape
    kdim = y.shape[1]
    tm = _tile(m, tm)
    vmem = 2 * tm * kdim * 2 + 2 * kdim * d * 2 + 4 * tm * d * 4 + 3 * tm * d * 4
    return pl.pallas_call(
        _outproj_kernel,
        grid=(m // tm,),
        in_specs=[
            pl.BlockSpec((tm, kdim), lambda i: (i, 0)),
            pl.BlockSpec((kdim, d), lambda i: (0, 0)),
            pl.BlockSpec((tm, d), lambda i: (i, 0)),
            _mod_spec(layer, 2, row_fn, d, 1),
            pl.BlockSpec((1, d), lambda i: (0, 0)),
        ],
        out_specs=pl.BlockSpec((tm, d), lambda i: (i, 0)),
        out_shape=jax.ShapeDtypeStruct((m, d), F32),
        compiler_params=_params(1, vmem),
        name="outproj",
    )(y, w, x, mods, g)


def _mlp_kernel(x_ref, sh_ref, sc_ref, gate_ref, g_in_ref, g_out_ref, w1_ref, w2_ref, o_ref, h_scr):
    j = pl.program_id(1)
    last = pl.num_programs(1) - 1
    tm = o_ref.shape[0]
    sub = _tile(tm, EDGE_ROWS)

    def hidden(h):
        a = jnp.dot(h, w1_ref[...], preferred_element_type=F32)
        a = jnp.maximum(a, 0.0)
        return jnp.dot((a * a).astype(BF16), w2_ref[...], preferred_element_type=F32)

    @pl.when(j == 0)
    def _():
        for r0 in range(0, tm, sub):
            rows = slice(r0, r0 + sub)
            h_scr[rows, :] = _modulated(x_ref.at[rows, :], g_in_ref, sh_ref, sc_ref)
            o_ref[rows, :] = hidden(h_scr[rows, :])

    @pl.when(jnp.logical_and(j > 0, j < last))
    def _():
        o_ref[...] += hidden(h_scr[...])

    @pl.when(j == last)
    def _():
        for r0 in range(0, tm, sub):
            rows = slice(r0, r0 + sub)
            f = o_ref[rows, :] + hidden(h_scr[rows, :])
            ms = jnp.mean(f * f, axis=-1, keepdims=True)
            o_ref[rows, :] = x_ref[rows, :] + gate_ref[...] * (f * lax.rsqrt(ms + NORM_EPS) * g_out_ref[...])


def _mlp(x, mods, layer, row_fn, g_in, g_out, w1, w2, *, tm, tf):
    m, d = x.shape
    dff = w1.shape[1]
    tm = _tile(m, tm)
    tf = _tile(dff, tf)
    vmem = 4 * tm * d * 4 + tm * d * 2 + 4 * d * tf * 2 + 3 * tm * tf * 4 + 4 * 1024 * 1024
    return pl.pallas_call(
        _mlp_kernel,
        grid=(m // tm, dff // tf),
        in_specs=[
            pl.BlockSpec((tm, d), lambda i, j: (i, 0)),
            _mod_spec(layer, 3, row_fn, d, 2),
            _mod_spec(layer, 4, row_fn, d, 2),
            _mod_spec(layer, 5, row_fn, d, 2),
            pl.BlockSpec((1, d), lambda i, j: (0, 0)),
            pl.BlockSpec((1, d), lambda i, j: (0, 0)),
            pl.BlockSpec((d, tf), lambda i, j: (0, j)),
            pl.BlockSpec((tf, d), lambda i, j: (j, 0)),
        ],
        out_specs=pl.BlockSpec((tm, d), lambda i, j: (i, 0)),
        out_shape=jax.ShapeDtypeStruct((m, d), F32),
        scratch_shapes=[pltpu.VMEM((tm, d), BF16)],
        compiler_params=_params(2, vmem),
        name="mlp",
    )(x, mods, mods, mods, g_in, g_out, w1, w2)


def _split3(x):
    x1 = x.astype(BF16)
    r1 = x - x1.astype(F32)
    x2 = r1.astype(BF16)
    x3 = (r1 - x2.astype(F32)).astype(BF16)
    return x1, x2, x3


def _log_sigmoid(x):
    return jnp.minimum(x, 0.0) - jnp.log(1.0 + jnp.exp(-jnp.abs(x)))


def _running_max_lanes(x, reverse):
    lane = lax.broadcasted_iota(jnp.int32, (x.shape[0], LANES), 1)
    slabs = [x[:, j:j + LANES] for j in range(0, x.shape[1], LANES)]
    order = range(len(slabs) - 1, -1, -1) if reverse else range(len(slabs))
    carry = None
    for j in order:
        y = slabs[j]
        k = 1
        while k < LANES:
            if reverse:
                y = jnp.maximum(y, jnp.where(lane < LANES - k, pltpu.roll(y, LANES - k, axis=1), -jnp.inf))
            else:
                y = jnp.maximum(y, jnp.where(lane >= k, pltpu.roll(y, k, axis=1), -jnp.inf))
            k *= 2
        if carry is not None:
            y = jnp.maximum(y, carry)
        carry = jnp.max(y, axis=-1, keepdims=True)
        slabs[j] = y
    return jnp.concatenate(slabs, axis=1)


def _gate_scan_kernel(x_ref, o_ref):
    L = x_ref.shape[-1]
    ui = lax.broadcasted_iota(jnp.int32, (L, L), 0)
    si = lax.broadcasted_iota(jnp.int32, (L, L), 1)

    def cumulative(z, tri):
        return sum(jnp.dot(p, tri.astype(BF16), preferred_element_type=F32) for p in _split3(z))

    b_f = cumulative(_log_sigmoid(x_ref[1]), ui <= si) * LOG2E
    b_b = cumulative(_log_sigmoid(x_ref[3]), ui >= si) * LOG2E
    c_f = x_ref[0] * LOG2E - b_f
    c_b = x_ref[2] * LOG2E - b_b
    o_ref[0] = b_f
    o_ref[1] = c_f
    o_ref[2] = _running_max_lanes(c_f, False)
    o_ref[3] = b_b
    o_ref[4] = c_b
    o_ref[5] = _running_max_lanes(c_b, True)


def _gate_scan(gates, nb, t, L):
    nc = t // L
    g = gates[:, :4 * ML_HEADS].reshape(nb, nc, L, 4, ML_HEADS)
    g = g.transpose(3, 0, 4, 1, 2).reshape(4, nb * ML_HEADS * nc, L)
    nr = g.shape[1]
    tr = _tile(nr, 256)
    n_out = 6
    s = pl.pallas_call(
        _gate_scan_kernel,
        grid=(nr // tr,),
        in_specs=[pl.BlockSpec((4, tr, L), lambda i: (0, i, 0))],
        out_specs=pl.BlockSpec((n_out, tr, L), lambda i: (0, i, 0)),
        out_shape=jax.ShapeDtypeStruct((n_out, nr, L), F32),
        compiler_params=_params(1, 48 * tr * max(L, LANES) * 4),
        name="gate_scan",
    )(g)
    s = s.reshape(n_out, nb, ML_HEADS, nc, L)
    rows = jnp.pad(s.transpose(1, 2, 3, 0, 4), ((0, 0), (0, 0), (0, 0), (0, GATE_ROWS - n_out), (0, 0)))
    return rows


def _mlstm_kernel(kc_ref, qtc_ref, vtc_ref, ogtc_ref, grc_ref,
                  kl_ref, qtl_ref, vtl_ref, ogtl_ref, grl_ref, hg_ref,
                  oc_ref, ol_ref, hf_c, hb_c, hf_l, hb_l):
    L = grc_ref.shape[-1]
    dk = kc_ref.shape[1] // ML_HB
    dv = vtc_ref.shape[0] // ML_HB
    ncc = kc_ref.shape[0] // L
    ncl = kl_ref.shape[0] // L

    si = lax.broadcasted_iota(jnp.int32, (L, L), 0)
    ti = lax.broadcasted_iota(jnp.int32, (L, L), 1)
    ones_rows = (lax.broadcasted_iota(jnp.int32, (ML_AUG, L), 0) == 0).astype(BF16)

    gate_tiles = {}
    for gr_ref, nc in ((grc_ref, ncc), (grl_ref, ncl)):
        for hh in range(ML_HB):
            for c in range(nc):
                tile = gr_ref[hh, c]
                gate_tiles[(id(gr_ref), hh, c)] = (tile, tile.T)

    st_cache = {}

    def scores_t(k_ref, qt_ref, hh, c):
        key = (id(k_ref), hh, c)
        if key not in st_cache:
            tok = slice(c * L, (c + 1) * L)
            st_cache[key] = jnp.dot(k_ref[tok, hh * dk:(hh + 1) * dk], qt_ref[hh * dk:(hh + 1) * dk, tok],
                                    preferred_element_type=F32)
        return st_cache[key]

    delta_cache = {}

    def state_delta(refs, c, hh, backward):
        k_ref, _, vt_ref, gr_ref, _ = refs
        key = (id(k_ref), hh, c, backward)
        if key not in delta_cache:
            tok = slice(c * L, (c + 1) * L)
            rows, _ = gate_tiles[(id(gr_ref), hh, c)]
            r = 3 if backward else 0
            b_row, c_row, c_max = rows[r:r + 1, :], rows[r + 1:r + 2, :], rows[r + 2:r + 3, :]
            b_end = b_row[:, 0:1] if backward else b_row[:, L - 1:L]
            g_max = b_end + (c_max[:, 0:1] if backward else c_max[:, L - 1:L])
            ws = jnp.exp2(b_end + c_row - g_max).astype(BF16)
            vt_aug = jnp.concatenate([vt_ref[hh * dv:(hh + 1) * dv, tok], ones_rows], axis=0)
            delta = jnp.dot(vt_aug * ws, k_ref[tok, hh * dk:(hh + 1) * dk], preferred_element_type=F32)
            delta_cache[key] = (delta, g_max)
        return delta_cache[key]

    def chunk(refs, c, hh, backward, state):
        k_ref, qt_ref, vt_ref, gr_ref, hs_ref = refs
        caug, m = state
        tok = slice(c * L, (c + 1) * L)
        qt = qt_ref[hh * dk:(hh + 1) * dk, tok]
        vt_aug = jnp.concatenate([vt_ref[hh * dv:(hh + 1) * dv, tok], ones_rows], axis=0)
        rows, cols = gate_tiles[(id(gr_ref), hh, c)]
        r = 3 if backward else 0
        b_row = rows[r:r + 1, :]
        c_max = rows[r + 2:r + 3, :]
        c_col = cols[:, r + 1:r + 2]
        b_end = b_row[:, 0:1] if backward else b_row[:, L - 1:L]

        inter = b_row + m
        m_row = jnp.maximum(inter, b_row + c_max)
        mask = (si >= ti) if backward else (si <= ti)
        decay = jnp.where(mask, jnp.exp2(c_col + (b_row - m_row)), 0.0)
        w_row = jnp.exp2(inter - m_row)

        p_t = (scores_t(k_ref, qt_ref, hh, c) * decay).astype(BF16)
        lhs = jnp.concatenate([caug.astype(BF16), vt_aug], axis=1)
        rhs = jnp.concatenate([qt * w_row.astype(BF16), p_t], axis=0)
        out = jnp.dot(lhs, rhs, preferred_element_type=F32)
        den = out[dv:dv + 1, :]
        hs_ref[hh * dv:(hh + 1) * dv, tok] = out[0:dv, :] * (1.0 / jnp.maximum(jnp.abs(den), jnp.exp2(-m_row)))

        d_state, g_max = state_delta(refs, c, hh, backward)
        m_new = jnp.maximum(b_end + m, g_max)
        caug_new = jnp.exp2(b_end + m - m_new) * caug + jnp.exp2(g_max - m_new) * d_state
        return caug_new, m_new

    def finish_chunk(hf_ref, hb_ref, ogt_ref, o_ref, c):
        tok = slice(c * L, (c + 1) * L)
        for hh in range(ML_HB):
            feat = slice(hh * dv, (hh + 1) * dv)
            hs = hf_ref[feat, tok] + hb_ref[feat, tok]
            scale = lax.rsqrt(jnp.mean(hs * hs, axis=0, keepdims=True) + NORM_EPS)
            head_g = jnp.concatenate([hg_ref[feat, :]] * (L // LANES), axis=1)
            y_t = hs * scale * head_g * jax.nn.sigmoid(ogt_ref[feat, tok].astype(F32))
            o_ref[tok, feat] = y_t.T.astype(o_ref.dtype)

    ctx_f = (kc_ref, qtc_ref, vtc_ref, grc_ref, hf_c)
    ctx_b = (kc_ref, qtc_ref, vtc_ref, grc_ref, hb_c)
    lat_f = (kl_ref, qtl_ref, vtl_ref, grl_ref, hf_l)
    lat_b = (kl_ref, qtl_ref, vtl_ref, grl_ref, hb_l)
    steps = [(ctx_f, s, ctx_b, ncc - 1 - s) for s in range(ncc)] \
        + [(lat_f, s, lat_b, ncl - 1 - s) for s in range(ncl)]

    def issue_scores(step):
        refs_f, c_f, refs_b, c_b = step
        for hh in range(ML_HB):
            scores_t(refs_f[0], refs_f[1], hh, c_f)
            scores_t(refs_b[0], refs_b[1], hh, c_b)
            state_delta(refs_f, c_f, hh, False)
            state_delta(refs_b, c_b, hh, True)

    zero = (jnp.zeros((dv + ML_AUG, dk), F32), jnp.zeros((1, 1), F32))
    states = [zero] * (2 * ML_HB)
    issue_scores(steps[0])
    for n, (refs_f, c_f, refs_b, c_b) in enumerate(steps):
        if n + 1 < len(steps):
            issue_scores(steps[n + 1])
        nxt = []
        for hh in range(ML_HB):
            nxt.append(chunk(refs_f, c_f, hh, False, states[2 * hh]))
            nxt.append(chunk(refs_b, c_b, hh, True, states[2 * hh + 1]))
        states = nxt
        if c_f >= c_b:
            ogt_ref, o_ref = (ogtc_ref, oc_ref) if refs_f is ctx_f else (ogtl_ref, ol_ref)
            for c in sorted({c_f, c_b}):
                finish_chunk(refs_f[4], refs_b[4], ogt_ref, o_ref, c)


def _mlstm(k_ctx, ft_ctx, g_ctx, k_lat, ft_lat, g_lat, head_g, nb, d):
    ctx_len = k_ctx.shape[0] // nb
    seq = k_lat.shape[0] // nb
    dv = ML_HB * (d // ML_HEADS)
    dk = dv // 2
    nh = ML_HEADS // ML_HB
    q_rows = ML_HEADS * (d // ML_HEADS // 2)
    v_blk = q_rows // dv
    og_blk = (q_rows + d) // dv
    L = math.gcd(math.gcd(ctx_len, seq), ML_BLOCK)
    assert L % LANES == 0
    grc = _gate_scan(g_ctx, nb, ctx_len, L)
    grl = _gate_scan(g_lat, nb, seq, L)
    hg = jnp.broadcast_to(head_g.reshape(d, 1), (d, LANES))

    def stream_specs(t):
        return [
            pl.BlockSpec((t, dk), lambda b, i: (b, i)),
            pl.BlockSpec((dk, t), lambda b, i: (i, b)),
            pl.BlockSpec((dv, t), lambda b, i: (v_blk + i, b)),
            pl.BlockSpec((dv, t), lambda b, i: (og_blk + i, b)),
            pl.BlockSpec((None, ML_HB, t // L, GATE_ROWS, L), lambda b, i: (b, i, 0, 0, 0)),
        ]

    t_all = ctx_len + seq
    gate_tiles = 2 * ML_HB * (t_all // L) * GATE_ROWS * L * 4
    vmem = 2 * t_all * (2 * dk * 2 + 3 * dv * 2) + 2 * t_all * dv * 4 + gate_tiles + 12 * 1024 * 1024
    return pl.pallas_call(
        _mlstm_kernel,
        grid=(nb, nh),
        in_specs=stream_specs(ctx_len) + stream_specs(seq) + [pl.BlockSpec((dv, LANES), lambda b, i: (i, 0))],
        out_specs=[pl.BlockSpec((ctx_len, dv), lambda b, i: (b, i)),
                   pl.BlockSpec((seq, dv), lambda b, i: (b, i))],
        out_shape=[jax.ShapeDtypeStruct((nb * ctx_len, d), BF16),
                   jax.ShapeDtypeStruct((nb * seq, d), BF16)],
        scratch_shapes=[pltpu.VMEM((dv, ctx_len), F32), pltpu.VMEM((dv, ctx_len), F32),
                        pltpu.VMEM((dv, seq), F32), pltpu.VMEM((dv, seq), F32)],
        compiler_params=_params(2, vmem),
        name="mlstm",
    )(k_ctx, ft_ctx, ft_ctx, ft_ctx, grc, k_lat, ft_lat, ft_lat, ft_lat, grl, hg)


def _attn_kernel(q_ref, kl_ref, vl_ref, kc_ref, vc_ref, lam_ref, gs_ref, o_ref, *, lam_init):
    dh = q_ref.shape[1] // 2
    seq = q_ref.shape[0]
    ctx_len = kc_ref.shape[0]
    lq = lam_ref[...]
    lam = jnp.exp(jnp.sum(lq[0:1] * lq[1:2], axis=-1, keepdims=True)) \
        - jnp.exp(jnp.sum(lq[2:3] * lq[3:4], axis=-1, keepdims=True)) + lam_init

    tq = _tile(seq, ATTN_TQ)
    key_sets = ((kl_ref, vl_ref), (kc_ref, vc_ref))

    def scores(t):
        rows = slice(t * tq, (t + 1) * tq)
        return [[_dot_nt(q_ref[rows, c * dh:(c + 1) * dh], k_ref[:, c * dh:(c + 1) * dh]) for c in range(2)]
                for k_ref, _ in key_sets]

    def lane_slabs(x):
        return [x[:, c:c + LANES] for c in range(0, x.shape[1], LANES)]

    def softmax_terms(s_sets):
        mx = functools.reduce(jnp.maximum, [slab for s in s_sets for slab in lane_slabs(s)])
        mx = jnp.max(mx, axis=-1, keepdims=True)
        p_sets = [jnp.exp2((s - mx).astype(BF16)) for s in s_sets]
        slabs = [slab for p in p_sets for slab in lane_slabs(p)]
        if len(slabs) % 2 == 0:
            slabs = [a + b for a, b in zip(slabs[0::2], slabs[1::2])]
        total = functools.reduce(jnp.add, [x.astype(F32) for x in slabs])
        return p_sets, 1.0 / jnp.sum(total, axis=-1, keepdims=True)

    n_tiles = seq // tq
    pending = [scores(t) for t in range(min(ATTN_AHEAD, n_tiles))]
    for t in range(n_tiles):
        if t + ATTN_AHEAD < n_tiles:
            pending.append(scores(t + ATTN_AHEAD))
        s = pending.pop(0)
        p0, r0 = softmax_terms([s[ks][0] for ks in range(2)])
        p1, r1 = softmax_terms([s[ks][1] for ks in range(2)])
        w0 = r0.astype(BF16)
        w1 = (lam * r1).astype(BF16)
        o = sum(jnp.dot(p0[ks] * w0 - p1[ks] * w1, v_ref[...], preferred_element_type=F32)
                for ks, (_, v_ref) in enumerate(key_sets))
        scale = lax.rsqrt(jnp.mean(o * o, axis=-1, keepdims=True) + NORM_EPS)
        o_ref[t * tq:(t + 1) * tq, :] = (o * scale * gs_ref[...] * (1.0 - lam_init)).astype(o_ref.dtype)


def _attn(p_lat, p_ctx, lam_qk, g_sub, nb, d, lam_init):
    seq = p_lat.shape[0] // nb
    ctx_len = p_ctx.shape[0] // nb
    h = DA_HEADS
    hd = d // h
    tq = _tile(seq, ATTN_TQ)
    vmem = 2 * (4 * seq + 2 * ctx_len) * hd * 2 + (ATTN_AHEAD + 3) * 2 * tq * (seq + ctx_len) * 4 \
        + 32 * tq * hd * 4 + 8 * 1024 * 1024
    return pl.pallas_call(
        functools.partial(_attn_kernel, lam_init=lam_init),
        grid=(nb, h),
        in_specs=[
            pl.BlockSpec((seq, hd), lambda b, i: (b, i)),
            pl.BlockSpec((seq, hd), lambda b, i: (b, h + i)),
            pl.BlockSpec((seq, hd), lambda b, i: (b, 2 * h + i)),
            pl.BlockSpec((ctx_len, hd), lambda b, i: (b, i)),
            pl.BlockSpec((ctx_len, hd), lambda b, i: (b, h + i)),
            pl.BlockSpec(lam_qk.shape, lambda b, i: (0, 0)),
            pl.BlockSpec((1, hd), lambda b, i: (0, 0)),
        ],
        out_specs=pl.BlockSpec((seq, hd), lambda b, i: (b, i)),
        out_shape=jax.ShapeDtypeStruct((nb * seq, d), BF16),
        compiler_params=_params(2, vmem),
        name="attn",
    )(p_lat, p_lat, p_lat, p_ctx, p_ctx, lam_qk, g_sub)


def _rope_tables(seq, dh):
    nf = dh // 4
    n = jnp.arange(seq)
    row = (n // GRID_W).astype(F32)
    col = (n % GRID_W).astype(F32)
    freq = ROPE_BASE ** (-jnp.arange(nf, dtype=F32) / nf)
    ar = row[:, None] * freq
    ac = col[:, None] * freq
    cos = jnp.concatenate([jnp.cos(ar), jnp.cos(ac), jnp.cos(ar), jnp.cos(ac)], axis=1)
    sin = jnp.concatenate([-jnp.sin(ar), -jnp.sin(ac), jnp.sin(ar), jnp.sin(ac)], axis=1)
    return cos, sin


def _rope_order(w, dh):
    nf = dh // 4
    g = w.reshape(w.shape[0], -1, 4, nf)
    return g[:, :, jnp.array([0, 2, 1, 3]), :].reshape(w.shape)


def kernel(x, c, ctx, c_ctx, ada_w, ada_b, norm_g, mlp_w1, mlp_w2, ml_w_in, ml_b_gates, ml_head_g,
           ml_w_out, da_w_in, da_lambda, da_sub_g, da_w_out):
    nb, seq, d = x.shape
    ctx_len = ctx.shape[1]
    depth = ada_w.shape[0]
    assert depth == 2 and nb < COND_ROWS and d // ML_HEADS == 2 * LANES
    assert seq % LANES == 0 and ctx_len % LANES == 0 and seq % GRID_W == 0

    xl = x.reshape(nb * seq, d)
    xc = ctx.reshape(nb * ctx_len, d)

    cond = jnp.zeros((COND_ROWS, d), F32).at[:nb].set(c).at[nb].set(c_ctx)
    mods = _ada(cond, ada_w, ada_b).reshape(depth, COND_ROWS, 1, N_MOD * d)

    tm = _tile(seq, 512)
    tm_mlp = _tile(seq, 1024)
    tmc = _tile(nb * ctx_len, 1024)

    def lat_row_of(tile_rows):
        return lambda i: i // (seq // tile_rows)

    lat_row = lat_row_of(tm)
    lat_row_mlp = lat_row_of(tm_mlp)

    def ctx_row(i):
        return nb

    g = norm_g.reshape(depth, 4, 1, d)

    ml_qk = ML_HEADS * (d // ML_HEADS // 2)
    n_main = 2 * ml_qk + 2 * d
    w_in = ml_w_in[0]
    q_scale = (d // ML_HEADS // 2) ** -0.5
    w_k = w_in[:, ml_qk:2 * ml_qk].astype(BF16)
    w_feat_t = jnp.concatenate([w_in[:, :ml_qk] * q_scale, w_in[:, 2 * ml_qk:n_main]], axis=1).T.astype(BF16)
    n_gate = 4 * ML_HEADS
    w_gate = jnp.pad(w_in[:, n_main:], ((0, 0), (0, LANES - n_gate))).astype(BF16)
    b_gate = jnp.pad(ml_b_gates[0], (0, LANES - n_gate)).reshape(1, LANES)
    w_out0 = ml_w_out[0].astype(BF16)
    w1_0 = mlp_w1[0].astype(BF16)
    w2_0 = mlp_w2[0].astype(BF16)

    k_lat, ft_lat, g_lat = _proj(xl, mods, 0, lat_row_mlp, g[0, 0], w_k, w_feat_t, tm=tm_mlp, tn=1024,
                                 gates=(w_gate, b_gate))
    k_ctx, ft_ctx, g_ctx = _proj(xc, mods, 0, ctx_row, g[0, 0], w_k, w_feat_t, tm=tmc, tn=1024,
                                 gates=(w_gate, b_gate))
    y_ctx, y_lat = _mlstm(k_ctx, ft_ctx, g_ctx, k_lat, ft_lat, g_lat, ml_head_g[0], nb, d)
    xl = _outproj(y_lat, w_out0, xl, mods, 0, lat_row, g[0, 1], tm=tm)
    xc = _outproj(y_ctx, w_out0, xc, mods, 0, ctx_row, g[0, 1], tm=tm)
    xl = _mlp(xl, mods, 0, lat_row_mlp, g[0, 2], g[0, 3], w1_0, w2_0, tm=tm_mlp, tf=512)
    xc = _mlp(xc, mods, 0, ctx_row, g[0, 2], g[0, 3], w1_0, w2_0, tm=tmc, tf=512)

    dh = d // (2 * DA_HEADS)
    w_in = da_w_in[0]
    w_q = _rope_order(w_in[:, :d] * (dh ** -0.5 * LOG2E), dh)
    w_k = _rope_order(w_in[:, d:2 * d], dh)
    w_qkv = jnp.concatenate([w_q, w_k, w_in[:, 2 * d:]], axis=1).astype(BF16)
    w_kv = jnp.concatenate([w_k, w_in[:, 2 * d:]], axis=1).astype(BF16)
    w_out1 = da_w_out[0].astype(BF16)
    w1_1 = mlp_w1[1].astype(BF16)
    w2_1 = mlp_w2[1].astype(BF16)
    layer_idx = 1
    lam_init = 0.8 - 0.6 * math.exp(-0.3 * layer_idx)

    p_lat = _proj(xl, mods, 1, lat_row_mlp, g[1, 0], w_qkv, tm=tm_mlp, tn=1024,
                  rope=_rope_tables(seq, dh), rope_cols=2 * d)
    p_ctx = _proj(xc, mods, 1, ctx_row, g[1, 0], w_kv, tm=tmc, tn=1024)
    y_lat = _attn(p_lat, p_ctx, da_lambda[0], da_sub_g[0].reshape(1, 2 * dh), nb, d, lam_init)
    xl = _outproj(y_lat, w_out1, xl, mods, 1, lat_row, g[1, 1], tm=tm)
    xl = _mlp(xl, mods, 1, lat_row_mlp, g[1, 2], g[1, 3], w1_1, w2_1, tm=tm_mlp, tf=512)
    return xl.reshape(nb, seq, d)
```

```python
import functools
import math

import jax
import jax.numpy as jnp
from jax import lax
from jax.experimental import pallas as pl
from jax.experimental.pallas import tpu as pltpu

F32 = jnp.float32
BF16 = jnp.bfloat16

NORM_EPS = 1e-6
GRID_W = 64
ROPE_BASE = 10000.0
LOG2E = math.log2(math.e)
ML_HEADS = 8
ML_BLOCK = 256
ML_HB = 2
ML_AUG = 16
DA_HEADS = 8
EDGE_ROWS = 512
ATTN_TQ = 256
ATTN_AHEAD = 2
N_MOD = 6
COND_ROWS = 32
LANES = 128
MXU_COLS = 256
GATE_ROWS = 8

VMEM_CAP_BYTES = 56 * 1024 * 1024


def _vmem_limit(nbytes):
    return int(min(VMEM_CAP_BYTES, max(16 * 1024 * 1024, nbytes)))


def _params(n_grid, vmem_bytes):
    return pltpu.CompilerParams(
        dimension_semantics=("arbitrary",) * n_grid,
        vmem_limit_bytes=_vmem_limit(vmem_bytes))


def _tile(n, want):
    if n <= want:
        return n
    t = want
    while t >= 8:
        if n % t == 0 and t % 8 == 0:
            return t
        t -= 8
    return n


def _dot_nt(a, b):
    return lax.dot_general(a, b, (((1,), (1,)), ((), ())), preferred_element_type=F32)


def _ada_kernel(c_ref, w_ref, b_ref, o_ref):
    cf = c_ref[...]
    s = (cf * jax.nn.sigmoid(cf)).astype(BF16)
    o_ref[...] = jnp.dot(s, w_ref[...].astype(BF16), preferred_element_type=F32) + b_ref[...]


def _ada(cond, ada_w, ada_b):
    depth, d, n = ada_w.shape
    tn = _tile(n, 1024)
    return pl.pallas_call(
        _ada_kernel,
        grid=(depth, n // tn),
        in_specs=[
            pl.BlockSpec((COND_ROWS, d), lambda l, j: (0, 0)),
            pl.BlockSpec((None, d, tn), lambda l, j: (l, 0, j)),
            pl.BlockSpec((None, 1, tn), lambda l, j: (l, 0, j)),
        ],
        out_specs=pl.BlockSpec((None, COND_ROWS, tn), lambda l, j: (l, 0, j)),
        out_shape=jax.ShapeDtypeStruct((depth, COND_ROWS, n), F32),
        compiler_params=_params(2, 2 * d * tn * 4 + 4 * COND_ROWS * (d + tn) * 4 + d * tn * 2),
        name="ada",
    )(cond, ada_w, ada_b.reshape(depth, 1, n))


def _mod_spec(layer, piece, row_fn, d, n_grid):
    if n_grid == 1:
        return pl.BlockSpec((None, None, 1, d), lambda i: (layer, row_fn(i), 0, piece))
    return pl.BlockSpec((None, None, 1, d), lambda i, j: (layer, row_fn(i), 0, piece))


def _modulated(x_ref, g_ref, sh_ref, sc_ref):
    xf = x_ref[...]
    ms = jnp.mean(xf * xf, axis=-1, keepdims=True)
    gain = g_ref[...] * (1.0 + sc_ref[...])
    return (xf * lax.rsqrt(ms + NORM_EPS) * gain + sh_ref[...]).astype(BF16)


def _proj_kernel(*refs, n_tok, n_rope_tok, with_feat, with_gates):
    x_ref, sh_ref, sc_ref, g_ref, wk_ref = refs[:5]
    n = 5
    if with_feat:
        wt_ref = refs[n]
        n += 1
    if with_gates:
        wg_ref, bg_ref = refs[n:n + 2]
        n += 2
    if n_rope_tok:
        cos_ref, sin_ref = refs[n:n + 2]
        n += 2
    k_ref = refs[n]
    n += 1
    if with_feat:
        ft_ref = refs[n]
        n += 1
    if with_gates:
        gate_ref = refs[n]
    h_scr = refs[-1]
    j = pl.program_id(1)
    tm = x_ref.shape[0]
    sub = _tile(tm, EDGE_ROWS)

    def token_tile(rows, rotate):
        if rotate:
            cos = cos_ref[rows, :]
            sin = sin_ref[rows, :]
        for c in range(wk_ref.shape[1] // MXU_COLS):
            cols = slice(c * MXU_COLS, (c + 1) * MXU_COLS)
            acc = jnp.dot(h_scr[rows, :], wk_ref[:, cols], preferred_element_type=F32)
            if not rotate:
                k_ref[rows, cols] = acc.astype(k_ref.dtype)
                continue
            for half in range(MXU_COLS // LANES):
                a = acc[:, half * LANES:(half + 1) * LANES]
                r = a * cos + pltpu.roll(a, LANES // 2, axis=1) * sin
                lo = c * MXU_COLS + half * LANES
                k_ref[rows, lo:lo + LANES] = r.astype(k_ref.dtype)

    @pl.when(j == 0)
    def _():
        for r0 in range(0, tm, sub):
            rows = slice(r0, r0 + sub)
            h_scr[rows, :] = _modulated(x_ref.at[rows, :], g_ref, sh_ref, sc_ref)
            if with_gates:
                gate_ref[rows, :] = jnp.dot(h_scr[rows, :], wg_ref[...], preferred_element_type=F32) + bg_ref[...]
            token_tile(rows, n_rope_tok > 0)

    if n_rope_tok > 1:
        @pl.when(jnp.logical_and(j > 0, j < n_rope_tok))
        def _():
            token_tile(slice(0, tm), True)

    if n_tok > max(n_rope_tok, 1):
        @pl.when(jnp.logical_and(j >= max(n_rope_tok, 1), j < n_tok))
        def _():
            token_tile(slice(0, tm), False)

    if with_feat:
        @pl.when(j >= n_tok)
        def _():
            ft_ref[...] = _dot_nt(wt_ref[...], h_scr[...]).astype(ft_ref.dtype)


def _proj(x, mods, layer, row_fn, g, wk, wt=None, *, tm, tn, gates=None, rope=None, rope_cols=0):
    m, d = x.shape
    nk = wk.shape[1]
    tm = _tile(m, tm)
    tnk = _tile(nk, tn)
    n_tok = nk // tnk
    assert rope_cols % tnk == 0

    def tok_tile(j):
        return jnp.minimum(j, n_tok - 1)

    def feat_tile(j):
        return jnp.maximum(j - n_tok, 0)

    in_specs = [
        pl.BlockSpec((tm, d), lambda i, j: (i, 0)),
        _mod_spec(layer, 0, row_fn, d, 2),
        _mod_spec(layer, 1, row_fn, d, 2),
        pl.BlockSpec((1, d), lambda i, j: (0, 0)),
        pl.BlockSpec((d, tnk), lambda i, j: (0, tok_tile(j))),
    ]
    args = [x, mods, mods, g, wk]
    out_specs = [pl.BlockSpec((tm, tnk), lambda i, j: (i, tok_tile(j)))]
    out_shape = [jax.ShapeDtypeStruct((m, nk), BF16)]
    n_feat = tnf = 0
    if wt is not None:
        nf = wt.shape[0]
        tnf = _tile(nf, tn)
        n_feat = nf // tnf
        in_specs.append(pl.BlockSpec((tnf, d), lambda i, j: (feat_tile(j), 0)))
        args.append(wt)
        out_specs.append(pl.BlockSpec((tnf, tm), lambda i, j: (feat_tile(j), i)))
        out_shape.append(jax.ShapeDtypeStruct((nf, m), BF16))
    if gates is not None:
        in_specs += [pl.BlockSpec((d, LANES), lambda i, j: (0, 0)),
                     pl.BlockSpec((1, LANES), lambda i, j: (0, 0))]
        args += list(gates)
        out_specs.append(pl.BlockSpec((tm, LANES), lambda i, j: (i, 0)))
        out_shape.append(jax.ShapeDtypeStruct((m, LANES), F32))
    if rope is not None:
        n_pos_tiles = rope[0].shape[0] // tm
        assert rope[0].shape[0] % tm == 0
        in_specs += [pl.BlockSpec((tm, LANES), lambda i, j: (i % n_pos_tiles, 0)),
                     pl.BlockSpec((tm, LANES), lambda i, j: (i % n_pos_tiles, 0))]
        args += list(rope)
    vmem = 2 * tm * d * 4 + tm * d * 2 + 2 * d * tnk * 2 + 2 * tnf * d * 2 + 2 * tm * tnk * 2 + 2 * tnf * tm * 2 \
        + 2 * tm * max(MXU_COLS, tnf) * 4 + 8 * tm * LANES * 4 + 2 * d * LANES * 2 + 3 * EDGE_ROWS * d * 4
    out = pl.pallas_call(
        functools.partial(_proj_kernel, n_tok=n_tok, n_rope_tok=rope_cols // tnk if rope is not None else 0,
                          with_feat=wt is not None, with_gates=gates is not None),
        grid=(m // tm, n_tok + n_feat),
        in_specs=in_specs,
        out_specs=out_specs,
        out_shape=out_shape,
        scratch_shapes=[pltpu.VMEM((tm, d), BF16)],
        compiler_params=_params(2, vmem),
        name="proj",
    )(*args)
    return out if len(out) > 1 else out[0]


def _outproj_kernel(y_ref, w_ref, x_ref, gate_ref, g_ref, o_ref):
    t = jnp.dot(y_ref[...], w_ref[...], preferred_element_type=F32)
    ms = jnp.mean(t * t, axis=-1, keepdims=True)
    o_ref[...] = x_ref[...] + gate_ref[...] * (t * lax.rsqrt(ms + NORM_EPS) * g_ref[...])


def _outproj(y, w, x, mods, layer, row_fn, g, *, tm):
    m, d = x.shape
    kdim = y.shape[1]
    tm = _tile(m, tm)
    vmem = 2 * tm * kdim * 2 + 2 * kdim * d * 2 + 4 * tm * d * 4 + 3 * tm * d * 4
    return pl.pallas_call(
        _outproj_kernel,
        grid=(m // tm,),
        in_specs=[
            pl.BlockSpec((tm, kdim), lambda i: (i, 0)),
            pl.BlockSpec((kdim, d), lambda i: (0, 0)),
            pl.BlockSpec((tm, d), lambda i: (i, 0)),
            _mod_spec(layer, 2, row_fn, d, 1),
            pl.BlockSpec((1, d), lambda i: (0, 0)),
        ],
        out_specs=pl.BlockSpec((tm, d), lambda i: (i, 0)),
        out_shape=jax.ShapeDtypeStruct((m, d), F32),
        compiler_params=_params(1, vmem),
        name="outproj",
    )(y, w, x, mods, g)


def _mlp_kernel(x_ref, sh_ref, sc_ref, gate_ref, g_in_ref, g_out_ref, w1_ref, w2_ref, o_ref, h_scr):
    j = pl.program_id(1)
    last = pl.num_programs(1) - 1
    tm = o_ref.shape[0]
    sub = _tile(tm, EDGE_ROWS)

    def hidden(h):
        a = jnp.dot(h, w1_ref[...], preferred_element_type=F32)
        a = jnp.maximum(a, 0.0)
        return jnp.dot((a * a).astype(BF16), w2_ref[...], preferred_element_type=F32)

    @pl.when(j == 0)
    def _():
        for r0 in range(0, tm, sub):
            rows = slice(r0, r0 + sub)
            h_scr[rows, :] = _modulated(x_ref.at[rows, :], g_in_ref, sh_ref, sc_ref)
            o_ref[rows, :] = hidden(h_scr[rows, :])

    @pl.when(jnp.logical_and(j > 0, j < last))
    def _():
        o_ref[...] += hidden(h_scr[...])

    @pl.when(j == last)
    def _():
        for r0 in range(0, tm, sub):
            rows = slice(r0, r0 + sub)
            f = o_ref[rows, :] + hidden(h_scr[rows, :])
            ms = jnp.mean(f * f, axis=-1, keepdims=True)
            o_ref[rows, :] = x_ref[rows, :] + gate_ref[...] * (f * lax.rsqrt(ms + NORM_EPS) * g_out_ref[...])


def _mlp(x, mods, layer, row_fn, g_in, g_out, w1, w2, *, tm, tf):
    m, d = x.shape
    dff = w1.shape[1]
    tm = _tile(m, tm)
    tf = _tile(dff, tf)
    vmem = 4 * tm * d * 4 + tm * d * 2 + 4 * d * tf * 2 + 3 * tm * tf * 4 + 4 * 1024 * 1024
    return pl.pallas_call(
        _mlp_kernel,
        grid=(m // tm, dff // tf),
        in_specs=[
            pl.BlockSpec((tm, d), lambda i, j: (i, 0)),
            _mod_spec(layer, 3, row_fn, d, 2),
            _mod_spec(layer, 4, row_fn, d, 2),
            _mod_spec(layer, 5, row_fn, d, 2),
            pl.BlockSpec((1, d), lambda i, j: (0, 0)),
            pl.BlockSpec((1, d), lambda i, j: (0, 0)),
            pl.BlockSpec((d, tf), lambda i, j: (0, j)),
            pl.BlockSpec((tf, d), lambda i, j: (j, 0)),
        ],
        out_specs=pl.BlockSpec((tm, d), lambda i, j: (i, 0)),
        out_shape=jax.ShapeDtypeStruct((m, d), F32),
        scratch_shapes=[pltpu.VMEM((tm, d), BF16)],
        compiler_params=_params(2, vmem),
        name="mlp",
    )(x, mods, mods, mods, g_in, g_out, w1, w2)


def _split3(x):
    x1 = x.astype(BF16)
    r1 = x - x1.astype(F32)
    x2 = r1.astype(BF16)
    x3 = (r1 - x2.astype(F32)).astype(BF16)
    return x1, x2, x3


def _log_sigmoid(x):
    return jnp.minimum(x, 0.0) - jnp.log(1.0 + jnp.exp(-jnp.abs(x)))


def _running_max_lanes(x, reverse):
    lane = lax.broadcasted_iota(jnp.int32, (x.shape[0], LANES), 1)
    slabs = [x[:, j:j + LANES] for j in range(0, x.shape[1], LANES)]
    order = range(len(slabs) - 1, -1, -1) if reverse else range(len(slabs))
    carry = None
    for j in order:
        y = slabs[j]
        k = 1
        while k < LANES:
            if reverse:
                y = jnp.maximum(y, jnp.where(lane < LANES - k, pltpu.roll(y, LANES - k, axis=1), -jnp.inf))
            else:
                y = jnp.maximum(y, jnp.where(lane >= k, pltpu.roll(y, k, axis=1), -jnp.inf))
            k *= 2
        if carry is not None:
            y = jnp.maximum(y, carry)
        carry = jnp.max(y, axis=-1, keepdims=True)
        slabs[j] = y
    return jnp.concatenate(slabs, axis=1)


def _gate_scan_kernel(x_ref, o_ref):
    L = x_ref.shape[-1]
    ui = lax.broadcasted_iota(jnp.int32, (L, L), 0)
    si = lax.broadcasted_iota(jnp.int32, (L, L), 1)

    def cumulative(z, tri):
        return sum(jnp.dot(p, tri.astype(BF16), preferred_element_type=F32) for p in _split3(z))

    b_f = cumulative(_log_sigmoid(x_ref[1]), ui <= si) * LOG2E
    b_b = cumulative(_log_sigmoid(x_ref[3]), ui >= si) * LOG2E
    c_f = x_ref[0] * LOG2E - b_f
    c_b = x_ref[2] * LOG2E - b_b
    o_ref[0] = b_f
    o_ref[1] = c_f
    o_ref[2] = _running_max_lanes(c_f, False)
    o_ref[3] = b_b
    o_ref[4] = c_b
    o_ref[5] = _running_max_lanes(c_b, True)


def _gate_scan(gates, nb, t, L):
    nc = t // L
    g = gates[:, :4 * ML_HEADS].reshape(nb, nc, L, 4, ML_HEADS)
    g = g.transpose(3, 0, 4, 1, 2).reshape(4, nb * ML_HEADS * nc, L)
    nr = g.shape[1]
    tr = _tile(nr, 256)
    n_out = 6
    s = pl.pallas_call(
        _gate_scan_kernel,
        grid=(nr // tr,),
        in_specs=[pl.BlockSpec((4, tr, L), lambda i: (0, i, 0))],
        out_specs=pl.BlockSpec((n_out, tr, L), lambda i: (0, i, 0)),
        out_shape=jax.ShapeDtypeStruct((n_out, nr, L), F32),
        compiler_params=_params(1, 48 * tr * max(L, LANES) * 4),
        name="gate_scan",
    )(g)
    s = s.reshape(n_out, nb, ML_HEADS, nc, L)
    rows = jnp.pad(s.transpose(1, 2, 3, 0, 4), ((0, 0), (0, 0), (0, 0), (0, GATE_ROWS - n_out), (0, 0)))
    return rows


def _mlstm_kernel(kc_ref, qtc_ref, vtc_ref, ogtc_ref, grc_ref,
                  kl_ref, qtl_ref, vtl_ref, ogtl_ref, grl_ref, hg_ref,
                  oc_ref, ol_ref, hf_c, hb_c, hf_l, hb_l):
    L = grc_ref.shape[-1]
    dk = kc_ref.shape[1] // ML_HB
    dv = vtc_ref.shape[0] // ML_HB
    ncc = kc_ref.shape[0] // L
    ncl = kl_ref.shape[0] // L

    si = lax.broadcasted_iota(jnp.int32, (L, L), 0)
    ti = lax.broadcasted_iota(jnp.int32, (L, L), 1)
    ones_rows = (lax.broadcasted_iota(jnp.int32, (ML_AUG, L), 0) == 0).astype(BF16)

    gate_tiles = {}
    for gr_ref, nc in ((grc_ref, ncc), (grl_ref, ncl)):
        for hh in range(ML_HB):
            for c in range(nc):
                tile = gr_ref[hh, c]
                gate_tiles[(id(gr_ref), hh, c)] = (tile, tile.T)

    st_cache = {}

    def scores_t(k_ref, qt_ref, hh, c):
        key = (id(k_ref), hh, c)
        if key not in st_cache:
            tok = slice(c * L, (c + 1) * L)
            st_cache[key] = jnp.dot(k_ref[tok, hh * dk:(hh + 1) * dk], qt_ref[hh * dk:(hh + 1) * dk, tok],
                                    preferred_element_type=F32)
        return st_cache[key]

    delta_cache = {}

    def state_delta(refs, c, hh, backward):
        k_ref, _, vt_ref, gr_ref, _ = refs
        key = (id(k_ref), hh, c, backward)
        if key not in delta_cache:
            tok = slice(c * L, (c + 1) * L)
            rows, _ = gate_tiles[(id(gr_ref), hh, c)]
            r = 3 if backward else 0
            b_row, c_row, c_max = rows[r:r + 1, :], rows[r + 1:r + 2, :], rows[r + 2:r + 3, :]
            b_end = b_row[:, 0:1] if backward else b_row[:, L - 1:L]
            g_max = b_end + (c_max[:, 0:1] if backward else c_max[:, L - 1:L])
            ws = jnp.exp2(b_end + c_row - g_max).astype(BF16)
            vt_aug = jnp.concatenate([vt_ref[hh * dv:(hh + 1) * dv, tok], ones_rows], axis=0)
            delta = jnp.dot(vt_aug * ws, k_ref[tok, hh * dk:(hh + 1) * dk], preferred_element_type=F32)
            delta_cache[key] = (delta, g_max)
        return delta_cache[key]

    def chunk(refs, c, hh, backward, state):
        k_ref, qt_ref, vt_ref, gr_ref, hs_ref = refs
        caug, m = state
        tok = slice(c * L, (c + 1) * L)
        qt = qt_ref[hh * dk:(hh + 1) * dk, tok]
        vt_aug = jnp.concatenate([vt_ref[hh * dv:(hh + 1) * dv, tok], ones_rows], axis=0)
        rows, cols = gate_tiles[(id(gr_ref), hh, c)]
        r = 3 if backward else 0
        b_row = rows[r:r + 1, :]
        c_max = rows[r + 2:r + 3, :]
        c_col = cols[:, r + 1:r + 2]
        b_end = b_row[:, 0:1] if backward else b_row[:, L - 1:L]

        inter = b_row + m
        m_row = jnp.maximum(inter, b_row + c_max)
        mask = (si >= ti) if backward else (si <= ti)
        decay = jnp.where(mask, jnp.exp2(c_col + (b_row - m_row)), 0.0)
        w_row = jnp.exp2(inter - m_row)

        p_t = (scores_t(k_ref, qt_ref, hh, c) * decay).astype(BF16)
        lhs = jnp.concatenate([caug.astype(BF16), vt_aug], axis=1)
        rhs = jnp.concatenate([qt * w_row.astype(BF16), p_t], axis=0)
        out = jnp.dot(lhs, rhs, preferred_element_type=F32)
        den = out[dv:dv + 1, :]
        hs_ref[hh * dv:(hh + 1) * dv, tok] = out[0:dv, :] * (1.0 / jnp.maximum(jnp.abs(den), jnp.exp2(-m_row)))

        d_state, g_max = state_delta(refs, c, hh, backward)
        m_new = jnp.maximum(b_end + m, g_max)
        caug_new = jnp.exp2(b_end + m - m_new) * caug + jnp.exp2(g_max - m_new) * d_state
        return caug_new, m_new

    def finish_chunk(hf_ref, hb_ref, ogt_ref, o_ref, c):
        tok = slice(c * L, (c + 1) * L)
        for hh in range(ML_HB):
            feat = slice(hh * dv, (hh + 1) * dv)
            hs = hf_ref[feat, tok] + hb_ref[feat, tok]
            scale = lax.rsqrt(jnp.mean(hs * hs, axis=0, keepdims=True) + NORM_EPS)
            head_g = jnp.concatenate([hg_ref[feat, :]] * (L // LANES), axis=1)
            y_t = hs * scale * head_g * jax.nn.sigmoid(ogt_ref[feat, tok].astype(F32))
            o_ref[tok, feat] = y_t.T.astype(o_ref.dtype)

    ctx_f = (kc_ref, qtc_ref, vtc_ref, grc_ref, hf_c)
    ctx_b = (kc_ref, qtc_ref, vtc_ref, grc_ref, hb_c)
    lat_f = (kl_ref, qtl_ref, vtl_ref, grl_ref, hf_l)
    lat_b = (kl_ref, qtl_ref, vtl_ref, grl_ref, hb_l)
    steps = [(ctx_f, s, ctx_b, ncc - 1 - s) for s in range(ncc)] \
        + [(lat_f, s, lat_b, ncl - 1 - s) for s in range(ncl)]

    def issue_scores(step):
        refs_f, c_f, refs_b, c_b = step
        for hh in range(ML_HB):
            scores_t(refs_f[0], refs_f[1], hh, c_f)
            scores_t(refs_b[0], refs_b[1], hh, c_b)
            state_delta(refs_f, c_f, hh, False)
            state_delta(refs_b, c_b, hh, True)

    zero = (jnp.zeros((dv + ML_AUG, dk), F32), jnp.zeros((1, 1), F32))
    states = [zero] * (2 * ML_HB)
    issue_scores(steps[0])
    for n, (refs_f, c_f, refs_b, c_b) in enumerate(steps):
        if n + 1 < len(steps):
            issue_scores(steps[n + 1])
        nxt = []
        for hh in range(ML_HB):
            nxt.append(chunk(refs_f, c_f, hh, False, states[2 * hh]))
            nxt.append(chunk(refs_b, c_b, hh, True, states[2 * hh + 1]))
        states = nxt
        if c_f >= c_b:
            ogt_ref, o_ref = (ogtc_ref, oc_ref) if refs_f is ctx_f else (ogtl_ref, ol_ref)
            for c in sorted({c_f, c_b}):
                finish_chunk(refs_f[4], refs_b[4], ogt_ref, o_ref, c)


def _mlstm(k_ctx, ft_ctx, g_ctx, k_lat, ft_lat, g_lat, head_g, nb, d):
    ctx_len = k_ctx.shape[0] // nb
    seq = k_lat.shape[0] // nb
    dv = ML_HB * (d // ML_HEADS)
    dk = dv // 2
    nh = ML_HEADS // ML_HB
    q_rows = ML_HEADS * (d // ML_HEADS // 2)
    v_blk = q_rows // dv
    og_blk = (q_rows + d) // dv
    L = math.gcd(math.gcd(ctx_len, seq), ML_BLOCK)
    assert L % LANES == 0
    grc = _gate_scan(g_ctx, nb, ctx_len, L)
    grl = _gate_scan(g_lat, nb, seq, L)
    hg = jnp.broadcast_to(head_g.reshape(d, 1), (d, LANES))

    def stream_specs(t):
        return [
            pl.BlockSpec((t, dk), lambda b, i: (b, i)),
            pl.BlockSpec((dk, t), lambda b, i: (i, b)),
            pl.BlockSpec((dv, t), lambda b, i: (v_blk + i, b)),
            pl.BlockSpec((dv, t), lambda b, i: (og_blk + i, b)),
            pl.BlockSpec((None, ML_HB, t // L, GATE_ROWS, L), lambda b, i: (b, i, 0, 0, 0)),
        ]

    t_all = ctx_len + seq
    gate_tiles = 2 * ML_HB * (t_all // L) * GATE_ROWS * L * 4
    vmem = 2 * t_all * (2 * dk * 2 + 3 * dv * 2) + 2 * t_all * dv * 4 + gate_tiles + 12 * 1024 * 1024
    return pl.pallas_call(
        _mlstm_kernel,
        grid=(nb, nh),
        in_specs=stream_specs(ctx_len) + stream_specs(seq) + [pl.BlockSpec((dv, LANES), lambda b, i: (i, 0))],
        out_specs=[pl.BlockSpec((ctx_len, dv), lambda b, i: (b, i)),
                   pl.BlockSpec((seq, dv), lambda b, i: (b, i))],
        out_shape=[jax.ShapeDtypeStruct((nb * ctx_len, d), BF16),
                   jax.ShapeDtypeStruct((nb * seq, d), BF16)],
        scratch_shapes=[pltpu.VMEM((dv, ctx_len), F32), pltpu.VMEM((dv, ctx_len), F32),
                        pltpu.VMEM((dv, seq), F32), pltpu.VMEM((dv, seq), F32)],
        compiler_params=_params(2, vmem),
        name="mlstm",
    )(k_ctx, ft_ctx, ft_ctx, ft_ctx, grc, k_lat, ft_lat, ft_lat, ft_lat, grl, hg)


def _attn_kernel(q_ref, kl_ref, vl_ref, kc_ref, vc_ref, lam_ref, gs_ref, o_ref, *, lam_init):
    dh = q_ref.shape[1] // 2
    seq = q_ref.shape[0]
    ctx_len = kc_ref.shape[0]
    lq = lam_ref[...]
    lam = jnp.exp(jnp.sum(lq[0:1] * lq[1:2], axis=-1, keepdims=True)) \
        - jnp.exp(jnp.sum(lq[2:3] * lq[3:4], axis=-1, keepdims=True)) + lam_init

    tq = _tile(seq, ATTN_TQ)
    key_sets = ((kl_ref, vl_ref), (kc_ref, vc_ref))

    def scores(t):
        rows = slice(t * tq, (t + 1) * tq)
        return [[_dot_nt(q_ref[rows, c * dh:(c + 1) * dh], k_ref[:, c * dh:(c + 1) * dh]) for c in range(2)]
                for k_ref, _ in key_sets]

    def lane_slabs(x):
        return [x[:, c:c + LANES] for c in range(0, x.shape[1], LANES)]

    def softmax_terms(s_sets):
        mx = functools.reduce(jnp.maximum, [slab for s in s_sets for slab in lane_slabs(s)])
        mx = jnp.max(mx, axis=-1, keepdims=True)
        p_sets = [jnp.exp2((s - mx).astype(BF16)) for s in s_sets]
        slabs = [slab for p in p_sets for slab in lane_slabs(p)]
        if len(slabs) % 2 == 0:
            slabs = [a + b for a, b in zip(slabs[0::2], slabs[1::2])]
        total = functools.reduce(jnp.add, [x.astype(F32) for x in slabs])
        return p_sets, 1.0 / jnp.sum(total, axis=-1, keepdims=True)

    n_tiles = seq // tq
    pending = [scores(t) for t in range(min(ATTN_AHEAD, n_tiles))]
    for t in range(n_tiles):
        if t + ATTN_AHEAD < n_tiles:
            pending.append(scores(t + ATTN_AHEAD))
        s = pending.pop(0)
        p0, r0 = softmax_terms([s[ks][0] for ks in range(2)])
        p1, r1 = softmax_terms([s[ks][1] for ks in range(2)])
        w0 = r0.astype(BF16)
        w1 = (lam * r1).astype(BF16)
        o = sum(jnp.dot(p0[ks] * w0 - p1[ks] * w1, v_ref[...], preferred_element_type=F32)
                for ks, (_, v_ref) in enumerate(key_sets))
        scale = lax.rsqrt(jnp.mean(o * o, axis=-1, keepdims=True) + NORM_EPS)
        o_ref[t * tq:(t + 1) * tq, :] = (o * scale * gs_ref[...] * (1.0 - lam_init)).astype(o_ref.dtype)


def _attn(p_lat, p_ctx, lam_qk, g_sub, nb, d, lam_init):
    seq = p_lat.shape[0] // nb
    ctx_len = p_ctx.shape[0] // nb
    h = DA_HEADS
    hd = d // h
    tq = _tile(seq, ATTN_TQ)
    vmem = 2 * (4 * seq + 2 * ctx_len) * hd * 2 + (ATTN_AHEAD + 3) * 2 * tq * (seq + ctx_len) * 4 \
        + 32 * tq * hd * 4 + 8 * 1024 * 1024
    return pl.pallas_call(
        functools.partial(_attn_kernel, lam_init=lam_init),
        grid=(nb, h),
        in_specs=[
            pl.BlockSpec((seq, hd), lambda b, i: (b, i)),
            pl.BlockSpec((seq, hd), lambda b, i: (b, h + i)),
            pl.BlockSpec((seq, hd), lambda b, i: (b, 2 * h + i)),
            pl.BlockSpec((ctx_len, hd), lambda b, i: (b, i)),
            pl.BlockSpec((ctx_len, hd), lambda b, i: (b, h + i)),
            pl.BlockSpec(lam_qk.shape, lambda b, i: (0, 0)),
            pl.BlockSpec((1, hd), lambda b, i: (0, 0)),
        ],
        out_specs=pl.BlockSpec((seq, hd), lambda b, i: (b, i)),
        out_shape=jax.ShapeDtypeStruct((nb * seq, d), BF16),
        compiler_params=_params(2, vmem),
        name="attn",
    )(p_lat, p_lat, p_lat, p_ctx, p_ctx, lam_qk, g_sub)


def _rope_tables(seq, dh):
    nf = dh // 4
    n = jnp.arange(seq)
    row = (n // GRID_W).astype(F32)
    col = (n % GRID_W).astype(F32)
    freq = ROPE_BASE ** (-jnp.arange(nf, dtype=F32) / nf)
    ar = row[:, None] * freq
    ac = col[:, None] * freq
    cos = jnp.concatenate([jnp.cos(ar), jnp.cos(ac), jnp.cos(ar), jnp.cos(ac)], axis=1)
    sin = jnp.concatenate([-jnp.sin(ar), -jnp.sin(ac), jnp.sin(ar), jnp.sin(ac)], axis=1)
    return cos, sin


def _rope_order(w, dh):
    nf = dh // 4
    g = w.reshape(w.shape[0], -1, 4, nf)
    return g[:, :, jnp.array([0, 2, 1, 3]), :].reshape(w.shape)


def kernel(x, c, ctx, c_ctx, ada_w, ada_b, norm_g, mlp_w1, mlp_w2, ml_w_in, ml_b_gates, ml_head_g,
           ml_w_out, da_w_in, da_lambda, da_sub_g, da_w_out):
    nb, seq, d = x.shape
    ctx_len = ctx.shape[1]
    depth = ada_w.shape[0]
    assert depth == 2 and nb < COND_ROWS and d // ML_HEADS == 2 * LANES
    assert seq % LANES == 0 and ctx_len % LANES == 0 and seq % GRID_W == 0

    xl = x.reshape(nb * seq, d)
    xc = ctx.reshape(nb * ctx_len, d)

    cond = jnp.zeros((COND_ROWS, d), F32).at[:nb].set(c).at[nb].set(c_ctx)
    mods = _ada(cond, ada_w, ada_b).reshape(depth, COND_ROWS, 1, N_MOD * d)

    tm = _tile(seq, 512)
    tm_mlp = _tile(seq, 1024)
    tmc = _tile(nb * ctx_len, 1024)

    def lat_row_of(tile_rows):
        return lambda i: i // (seq // tile_rows)

    lat_row = lat_row_of(tm)
    lat_row_mlp = lat_row_of(tm_mlp)

    def ctx_row(i):
        return nb

    g = norm_g.reshape(depth, 4, 1, d)

    ml_qk = ML_HEADS * (d // ML_HEADS // 2)
    n_main = 2 * ml_qk + 2 * d
    w_in = ml_w_in[0]
    q_scale = (d // ML_HEADS // 2) ** -0.5
    w_k = w_in[:, ml_qk:2 * ml_qk].astype(BF16)
    w_feat_t = jnp.concatenate([w_in[:, :ml_qk] * q_scale, w_in[:, 2 * ml_qk:n_main]], axis=1).T.astype(BF16)
    n_gate = 4 * ML_HEADS
    w_gate = jnp.pad(w_in[:, n_main:], ((0, 0), (0, LANES - n_gate))).astype(BF16)
    b_gate = jnp.pad(ml_b_gates[0], (0, LANES - n_gate)).reshape(1, LANES)
    w_out0 = ml_w_out[0].astype(BF16)
    w1_0 = mlp_w1[0].astype(BF16)
    w2_0 = mlp_w2[0].astype(BF16)

    k_lat, ft_lat, g_lat = _proj(xl, mods, 0, lat_row_mlp, g[0, 0], w_k, w_feat_t, tm=tm_mlp, tn=1024,
                                 gates=(w_gate, b_gate))
    k_ctx, ft_ctx, g_ctx = _proj(xc, mods, 0, ctx_row, g[0, 0], w_k, w_feat_t, tm=tmc, tn=1024,
                                 gates=(w_gate, b_gate))
    y_ctx, y_lat = _mlstm(k_ctx, ft_ctx, g_ctx, k_lat, ft_lat, g_lat, ml_head_g[0], nb, d)
    xl = _outproj(y_lat, w_out0, xl, mods, 0, lat_row, g[0, 1], tm=tm)
    xc = _outproj(y_ctx, w_out0, xc, mods, 0, ctx_row, g[0, 1], tm=tm)
    xl = _mlp(xl, mods, 0, lat_row_mlp, g[0, 2], g[0, 3], w1_0, w2_0, tm=tm_mlp, tf=512)
    xc = _mlp(xc, mods, 0, ctx_row, g[0, 2], g[0, 3], w1_0, w2_0, tm=tmc, tf=512)

    dh = d // (2 * DA_HEADS)
    w_in = da_w_in[0]
    w_q = _rope_order(w_in[:, :d] * (dh ** -0.5 * LOG2E), dh)
    w_k = _rope_order(w_in[:, d:2 * d], dh)
    w_qkv = jnp.concatenate([w_q, w_k, w_in[:, 2 * d:]], axis=1).astype(BF16)
    w_kv = jnp.concatenate([w_k, w_in[:, 2 * d:]], axis=1).astype(BF16)
    w_out1 = da_w_out[0].astype(BF16)
    w1_1 = mlp_w1[1].astype(BF16)
    w2_1 = mlp_w2[1].astype(BF16)
    layer_idx = 1
    lam_init = 0.8 - 0.6 * math.exp(-0.3 * layer_idx)

    p_lat = _proj(xl, mods, 1, lat_row_mlp, g[1, 0], w_qkv, tm=tm_mlp, tn=2048,
                  rope=_rope_tables(seq, dh), rope_cols=2 * d)
    p_ctx = _proj(xc, mods, 1, ctx_row, g[1, 0], w_kv, tm=tmc, tn=2048)
    y_lat = _attn(p_lat, p_ctx, da_lambda[0], da_sub_g[0].reshape(1, 2 * dh), nb, d, lam_init)
    xl = _outproj(y_lat, w_out1, xl, mods, 1, lat_row, g[1, 1], tm=tm)
    xl = _mlp(xl, mods, 1, lat_row_mlp, g[1, 2], g[1, 3], w1_1, w2_1, tm=tm_mlp, tf=512)
    return xl.reshape(nb, seq, d)
```

```python
import functools
import math

import jax
import jax.numpy as jnp
from jax import lax
from jax.experimental import pallas as pl
from jax.experimental.pallas import tpu as pltpu

F32 = jnp.float32
BF16 = jnp.bfloat16

NORM_EPS = 1e-6
GRID_W = 64
ROPE_BASE = 10000.0
LOG2E = math.log2(math.e)
ML_HEADS = 8
ML_BLOCK = 256
ML_HB = 2
ML_AUG = 16
DA_HEADS = 8
EDGE_ROWS = 512
ATTN_TQ = 256
ATTN_AHEAD = 2
N_MOD = 6
COND_ROWS = 32
LANES = 128
MXU_COLS = 256
GATE_ROWS = 8

VMEM_CAP_BYTES = 58 * 1024 * 1024


def _vmem_limit(nbytes):
    return int(min(VMEM_CAP_BYTES, max(16 * 1024 * 1024, nbytes)))


def _params(n_grid, vmem_bytes):
    return pltpu.CompilerParams(
        dimension_semantics=("arbitrary",) * n_grid,
        vmem_limit_bytes=_vmem_limit(vmem_bytes))


def _tile(n, want):
    if n <= want:
        return n
    t = want
    while t >= 8:
        if n % t == 0 and t % 8 == 0:
            return t
        t -= 8
    return n


def _dot_nt(a, b):
    return lax.dot_general(a, b, (((1,), (1,)), ((), ())), preferred_element_type=F32)


def _ada_kernel(c_ref, w_ref, b_ref, o_ref):
    cf = c_ref[...]
    s = (cf * jax.nn.sigmoid(cf)).astype(BF16)
    o_ref[...] = jnp.dot(s, w_ref[...].astype(BF16), preferred_element_type=F32) + b_ref[...]


def _ada(cond, ada_w, ada_b):
    depth, d, n = ada_w.shape
    tn = _tile(n, 1024)
    return pl.pallas_call(
        _ada_kernel,
        grid=(depth, n // tn),
        in_specs=[
            pl.BlockSpec((COND_ROWS, d), lambda l, j: (0, 0)),
            pl.BlockSpec((None, d, tn), lambda l, j: (l, 0, j)),
            pl.BlockSpec((None, 1, tn), lambda l, j: (l, 0, j)),
        ],
        out_specs=pl.BlockSpec((None, COND_ROWS, tn), lambda l, j: (l, 0, j)),
        out_shape=jax.ShapeDtypeStruct((depth, COND_ROWS, n), F32),
        compiler_params=_params(2, 2 * d * tn * 4 + 4 * COND_ROWS * (d + tn) * 4 + d * tn * 2),
        name="ada",
    )(cond, ada_w, ada_b.reshape(depth, 1, n))


def _mod_spec(layer, piece, row_fn, d, n_grid):
    if n_grid == 1:
        return pl.BlockSpec((None, None, 1, d), lambda i: (layer, row_fn(i), 0, piece))
    return pl.BlockSpec((None, None, 1, d), lambda i, j: (layer, row_fn(i), 0, piece))


def _modulated(x_ref, g_ref, sh_ref, sc_ref):
    xf = x_ref[...]
    ms = jnp.mean(xf * xf, axis=-1, keepdims=True)
    gain = g_ref[...] * (1.0 + sc_ref[...])
    return (xf * lax.rsqrt(ms + NORM_EPS) * gain + sh_ref[...]).astype(BF16)


def _proj_kernel(*refs, n_tok, n_rope_tok, with_feat, with_gates):
    x_ref, sh_ref, sc_ref, g_ref, wk_ref = refs[:5]
    n = 5
    if with_feat:
        wt_ref = refs[n]
        n += 1
    if with_gates:
        wg_ref, bg_ref = refs[n:n + 2]
        n += 2
    if n_rope_tok:
        cos_ref, sin_ref = refs[n:n + 2]
        n += 2
    k_ref = refs[n]
    n += 1
    if with_feat:
        ft_ref = refs[n]
        n += 1
    if with_gates:
        gate_ref = refs[n]
    h_scr = refs[-1]
    j = pl.program_id(1)
    tm = x_ref.shape[0]
    sub = _tile(tm, EDGE_ROWS)

    def token_tile(rows, rotate):
        if rotate:
            cos = cos_ref[rows, :]
            sin = sin_ref[rows, :]
        for c in range(wk_ref.shape[1] // MXU_COLS):
            cols = slice(c * MXU_COLS, (c + 1) * MXU_COLS)
            acc = jnp.dot(h_scr[rows, :], wk_ref[:, cols], preferred_element_type=F32)
            if not rotate:
                k_ref[rows, cols] = acc.astype(k_ref.dtype)
                continue
            for half in range(MXU_COLS // LANES):
                a = acc[:, half * LANES:(half + 1) * LANES]
                r = a * cos + pltpu.roll(a, LANES // 2, axis=1) * sin
                lo = c * MXU_COLS + half * LANES
                k_ref[rows, lo:lo + LANES] = r.astype(k_ref.dtype)

    @pl.when(j == 0)
    def _():
        for r0 in range(0, tm, sub):
            rows = slice(r0, r0 + sub)
            h_scr[rows, :] = _modulated(x_ref.at[rows, :], g_ref, sh_ref, sc_ref)
            if with_gates:
                gate_ref[rows, :] = jnp.dot(h_scr[rows, :], wg_ref[...], preferred_element_type=F32) + bg_ref[...]
            token_tile(rows, n_rope_tok > 0)

    if n_rope_tok > 1:
        @pl.when(jnp.logical_and(j > 0, j < n_rope_tok))
        def _():
            token_tile(slice(0, tm), True)

    if n_tok > max(n_rope_tok, 1):
        @pl.when(jnp.logical_and(j >= max(n_rope_tok, 1), j < n_tok))
        def _():
            token_tile(slice(0, tm), False)

    if with_feat:
        @pl.when(j >= n_tok)
        def _():
            ft_ref[...] = _dot_nt(wt_ref[...], h_scr[...]).astype(ft_ref.dtype)


def _proj(x, mods, layer, row_fn, g, wk, wt=None, *, tm, tn, gates=None, rope=None, rope_cols=0):
    m, d = x.shape
    nk = wk.shape[1]
    tm = _tile(m, tm)
    tnk = _tile(nk, tn)
    n_tok = nk // tnk
    assert rope_cols % tnk == 0

    def tok_tile(j):
        return jnp.minimum(j, n_tok - 1)

    def feat_tile(j):
        return jnp.maximum(j - n_tok, 0)

    in_specs = [
        pl.BlockSpec((tm, d), lambda i, j: (i, 0)),
        _mod_spec(layer, 0, row_fn, d, 2),
        _mod_spec(layer, 1, row_fn, d, 2),
        pl.BlockSpec((1, d), lambda i, j: (0, 0)),
        pl.BlockSpec((d, tnk), lambda i, j: (0, tok_tile(j))),
    ]
    args = [x, mods, mods, g, wk]
    out_specs = [pl.BlockSpec((tm, tnk), lambda i, j: (i, tok_tile(j)))]
    out_shape = [jax.ShapeDtypeStruct((m, nk), BF16)]
    n_feat = tnf = 0
    if wt is not None:
        nf = wt.shape[0]
        tnf = _tile(nf, tn)
        n_feat = nf // tnf
        in_specs.append(pl.BlockSpec((tnf, d), lambda i, j: (feat_tile(j), 0)))
        args.append(wt)
        out_specs.append(pl.BlockSpec((tnf, tm), lambda i, j: (feat_tile(j), i)))
        out_shape.append(jax.ShapeDtypeStruct((nf, m), BF16))
    if gates is not None:
        in_specs += [pl.BlockSpec((d, LANES), lambda i, j: (0, 0)),
                     pl.BlockSpec((1, LANES), lambda i, j: (0, 0))]
        args += list(gates)
        out_specs.append(pl.BlockSpec((tm, LANES), lambda i, j: (i, 0)))
        out_shape.append(jax.ShapeDtypeStruct((m, LANES), F32))
    if rope is not None:
        n_pos_tiles = rope[0].shape[0] // tm
        assert rope[0].shape[0] % tm == 0
        in_specs += [pl.BlockSpec((tm, LANES), lambda i, j: (i % n_pos_tiles, 0)),
                     pl.BlockSpec((tm, LANES), lambda i, j: (i % n_pos_tiles, 0))]
        args += list(rope)
    vmem = 2 * tm * d * 4 + tm * d * 2 + 2 * d * tnk * 2 + 2 * tnf * d * 2 + 2 * tm * tnk * 2 + 2 * tnf * tm * 2 \
        + 2 * tm * max(MXU_COLS, tnf) * 4 + 8 * tm * LANES * 4 + 2 * d * LANES * 2 + 3 * EDGE_ROWS * d * 4
    out = pl.pallas_call(
        functools.partial(_proj_kernel, n_tok=n_tok, n_rope_tok=rope_cols // tnk if rope is not None else 0,
                          with_feat=wt is not None, with_gates=gates is not None),
        grid=(m // tm, n_tok + n_feat),
        in_specs=in_specs,
        out_specs=out_specs,
        out_shape=out_shape,
        scratch_shapes=[pltpu.VMEM((tm, d), BF16)],
        compiler_params=_params(2, vmem),
        name="proj",
    )(*args)
    return out if len(out) > 1 else out[0]


def _outproj_kernel(y_ref, w_ref, x_ref, gate_ref, g_ref, o_ref):
    t = jnp.dot(y_ref[...], w_ref[...], preferred_element_type=F32)
    ms = jnp.mean(t * t, axis=-1, keepdims=True)
    o_ref[...] = x_ref[...] + gate_ref[...] * (t * lax.rsqrt(ms + NORM_EPS) * g_ref[...])


def _outproj(y, w, x, mods, layer, row_fn, g, *, tm):
    m, d = x.shape
    kdim = y.shape[1]
    tm = _tile(m, tm)
    vmem = 2 * tm * kdim * 2 + 2 * kdim * d * 2 + 4 * tm * d * 4 + 3 * tm * d * 4
    return pl.pallas_call(
        _outproj_kernel,
        grid=(m // tm,),
        in_specs=[
            pl.BlockSpec((tm, kdim), lambda i: (i, 0)),
            pl.BlockSpec((kdim, d), lambda i: (0, 0)),
            pl.BlockSpec((tm, d), lambda i: (i, 0)),
            _mod_spec(layer, 2, row_fn, d, 1),
            pl.BlockSpec((1, d), lambda i: (0, 0)),
        ],
        out_specs=pl.BlockSpec((tm, d), lambda i: (i, 0)),
        out_shape=jax.ShapeDtypeStruct((m, d), F32),
        compiler_params=_params(1, vmem),
        name="outproj",
    )(y, w, x, mods, g)


def _mlp_kernel(x_ref, sh_ref, sc_ref, gate_ref, g_in_ref, g_out_ref, w1_ref, w2_ref, o_ref, h_scr):
    j = pl.program_id(1)
    last = pl.num_programs(1) - 1
    tm = o_ref.shape[0]
    sub = _tile(tm, EDGE_ROWS)

    def hidden(h):
        pieces = []
        for c in range(w1_ref.shape[1] // MXU_COLS):
            a = jnp.dot(h, w1_ref[:, c * MXU_COLS:(c + 1) * MXU_COLS], preferred_element_type=F32)
            a = jnp.maximum(a, 0.0)
            pieces.append((a * a).astype(BF16))
        return jnp.dot(jnp.concatenate(pieces, axis=1), w2_ref[...], preferred_element_type=F32)

    @pl.when(j == 0)
    def _():
        for r0 in range(0, tm, sub):
            rows = slice(r0, r0 + sub)
            h_scr[rows, :] = _modulated(x_ref.at[rows, :], g_in_ref, sh_ref, sc_ref)
            o_ref[rows, :] = hidden(h_scr[rows, :])

    @pl.when(jnp.logical_and(j > 0, j < last))
    def _():
        o_ref[...] += hidden(h_scr[...])

    @pl.when(j == last)
    def _():
        for r0 in range(0, tm, sub):
            rows = slice(r0, r0 + sub)
            f = o_ref[rows, :] + hidden(h_scr[rows, :])
            ms = jnp.mean(f * f, axis=-1, keepdims=True)
            o_ref[rows, :] = x_ref[rows, :] + gate_ref[...] * (f * lax.rsqrt(ms + NORM_EPS) * g_out_ref[...])


def _mlp(x, mods, layer, row_fn, g_in, g_out, w1, w2, *, tm, tf):
    m, d = x.shape
    dff = w1.shape[1]
    tm = _tile(m, tm)
    tf = _tile(dff, tf)
    vmem = 4 * tm * d * 4 + tm * d * 2 + 4 * d * tf * 2 + tm * tf * 2 + 2 * tm * MXU_COLS * 4 + 2 * 1024 * 1024
    return pl.pallas_call(
        _mlp_kernel,
        grid=(m // tm, dff // tf),
        in_specs=[
            pl.BlockSpec((tm, d), lambda i, j: (i, 0)),
            _mod_spec(layer, 3, row_fn, d, 2),
            _mod_spec(layer, 4, row_fn, d, 2),
            _mod_spec(layer, 5, row_fn, d, 2),
            pl.BlockSpec((1, d), lambda i, j: (0, 0)),
            pl.BlockSpec((1, d), lambda i, j: (0, 0)),
            pl.BlockSpec((d, tf), lambda i, j: (0, j)),
            pl.BlockSpec((tf, d), lambda i, j: (j, 0)),
        ],
        out_specs=pl.BlockSpec((tm, d), lambda i, j: (i, 0)),
        out_shape=jax.ShapeDtypeStruct((m, d), F32),
        scratch_shapes=[pltpu.VMEM((tm, d), BF16)],
        compiler_params=_params(2, vmem),
        name="mlp",
    )(x, mods, mods, mods, g_in, g_out, w1, w2)


def _split3(x):
    x1 = x.astype(BF16)
    r1 = x - x1.astype(F32)
    x2 = r1.astype(BF16)
    x3 = (r1 - x2.astype(F32)).astype(BF16)
    return x1, x2, x3


def _log_sigmoid(x):
    return jnp.minimum(x, 0.0) - jnp.log(1.0 + jnp.exp(-jnp.abs(x)))


def _running_max_lanes(x, reverse):
    lane = lax.broadcasted_iota(jnp.int32, (x.shape[0], LANES), 1)
    slabs = [x[:, j:j + LANES] for j in range(0, x.shape[1], LANES)]
    order = range(len(slabs) - 1, -1, -1) if reverse else range(len(slabs))
    carry = None
    for j in order:
        y = slabs[j]
        k = 1
        while k < LANES:
            if reverse:
                y = jnp.maximum(y, jnp.where(lane < LANES - k, pltpu.roll(y, LANES - k, axis=1), -jnp.inf))
            else:
                y = jnp.maximum(y, jnp.where(lane >= k, pltpu.roll(y, k, axis=1), -jnp.inf))
            k *= 2
        if carry is not None:
            y = jnp.maximum(y, carry)
        carry = jnp.max(y, axis=-1, keepdims=True)
        slabs[j] = y
    return jnp.concatenate(slabs, axis=1)


def _gate_scan_kernel(x_ref, o_ref):
    L = x_ref.shape[-1]
    ui = lax.broadcasted_iota(jnp.int32, (L, L), 0)
    si = lax.broadcasted_iota(jnp.int32, (L, L), 1)

    def cumulative(z, tri):
        return sum(jnp.dot(p, tri.astype(BF16), preferred_element_type=F32) for p in _split3(z))

    b_f = cumulative(_log_sigmoid(x_ref[1]), ui <= si) * LOG2E
    b_b = cumulative(_log_sigmoid(x_ref[3]), ui >= si) * LOG2E
    c_f = x_ref[0] * LOG2E - b_f
    c_b = x_ref[2] * LOG2E - b_b
    o_ref[0] = b_f
    o_ref[1] = c_f
    o_ref[2] = _running_max_lanes(c_f, False)
    o_ref[3] = b_b
    o_ref[4] = c_b
    o_ref[5] = _running_max_lanes(c_b, True)


def _gate_scan(gates, nb, t, L):
    nc = t // L
    g = gates[:, :4 * ML_HEADS].reshape(nb, nc, L, 4, ML_HEADS)
    g = g.transpose(3, 0, 4, 1, 2).reshape(4, nb * ML_HEADS * nc, L)
    nr = g.shape[1]
    tr = _tile(nr, 256)
    n_out = 6
    s = pl.pallas_call(
        _gate_scan_kernel,
        grid=(nr // tr,),
        in_specs=[pl.BlockSpec((4, tr, L), lambda i: (0, i, 0))],
        out_specs=pl.BlockSpec((n_out, tr, L), lambda i: (0, i, 0)),
        out_shape=jax.ShapeDtypeStruct((n_out, nr, L), F32),
        compiler_params=_params(1, 48 * tr * max(L, LANES) * 4),
        name="gate_scan",
    )(g)
    s = s.reshape(n_out, nb, ML_HEADS, nc, L)
    rows = jnp.pad(s.transpose(1, 2, 3, 0, 4), ((0, 0), (0, 0), (0, 0), (0, GATE_ROWS - n_out), (0, 0)))
    return rows


def _mlstm_kernel(kc_ref, qtc_ref, vtc_ref, ogtc_ref, grc_ref,
                  kl_ref, qtl_ref, vtl_ref, ogtl_ref, grl_ref, hg_ref,
                  oc_ref, ol_ref, hf_c, hb_c, hf_l, hb_l):
    L = grc_ref.shape[-1]
    dk = kc_ref.shape[1] // ML_HB
    dv = vtc_ref.shape[0] // ML_HB
    ncc = kc_ref.shape[0] // L
    ncl = kl_ref.shape[0] // L

    si = lax.broadcasted_iota(jnp.int32, (L, L), 0)
    ti = lax.broadcasted_iota(jnp.int32, (L, L), 1)
    ones_rows = (lax.broadcasted_iota(jnp.int32, (ML_AUG, L), 0) == 0).astype(BF16)

    gate_tiles = {}
    for gr_ref, nc in ((grc_ref, ncc), (grl_ref, ncl)):
        for hh in range(ML_HB):
            for c in range(nc):
                tile = gr_ref[hh, c]
                gate_tiles[(id(gr_ref), hh, c)] = (tile, tile.T)

    st_cache = {}

    def scores_t(k_ref, qt_ref, hh, c):
        key = (id(k_ref), hh, c)
        if key not in st_cache:
            tok = slice(c * L, (c + 1) * L)
            st_cache[key] = jnp.dot(k_ref[tok, hh * dk:(hh + 1) * dk], qt_ref[hh * dk:(hh + 1) * dk, tok],
                                    preferred_element_type=F32)
        return st_cache[key]

    delta_cache = {}

    def state_delta(refs, c, hh, backward):
        k_ref, _, vt_ref, gr_ref, _ = refs
        key = (id(k_ref), hh, c, backward)
        if key not in delta_cache:
            tok = slice(c * L, (c + 1) * L)
            rows, _ = gate_tiles[(id(gr_ref), hh, c)]
            r = 3 if backward else 0
            b_row, c_row, c_max = rows[r:r + 1, :], rows[r + 1:r + 2, :], rows[r + 2:r + 3, :]
            b_end = b_row[:, 0:1] if backward else b_row[:, L - 1:L]
            g_max = b_end + (c_max[:, 0:1] if backward else c_max[:, L - 1:L])
            ws = jnp.exp2(b_end + c_row - g_max).astype(BF16)
            vt_aug = jnp.concatenate([vt_ref[hh * dv:(hh + 1) * dv, tok], ones_rows], axis=0)
            delta = jnp.dot(vt_aug * ws, k_ref[tok, hh * dk:(hh + 1) * dk], preferred_element_type=F32)
            delta_cache[key] = (delta, g_max)
        return delta_cache[key]

    def chunk(refs, c, hh, backward, state):
        k_ref, qt_ref, vt_ref, gr_ref, hs_ref = refs
        caug, m = state
        tok = slice(c * L, (c + 1) * L)
        qt = qt_ref[hh * dk:(hh + 1) * dk, tok]
        vt_aug = jnp.concatenate([vt_ref[hh * dv:(hh + 1) * dv, tok], ones_rows], axis=0)
        rows, cols = gate_tiles[(id(gr_ref), hh, c)]
        r = 3 if backward else 0
        b_row = rows[r:r + 1, :]
        c_max = rows[r + 2:r + 3, :]
        c_col = cols[:, r + 1:r + 2]
        b_end = b_row[:, 0:1] if backward else b_row[:, L - 1:L]

        inter = b_row + m
        m_row = jnp.maximum(inter, b_row + c_max)
        mask = (si >= ti) if backward else (si <= ti)
        decay = jnp.where(mask, jnp.exp2(c_col + (b_row - m_row)), 0.0)
        w_row = jnp.exp2(inter - m_row)

        p_t = (scores_t(k_ref, qt_ref, hh, c) * decay).astype(BF16)
        lhs = jnp.concatenate([caug.astype(BF16), vt_aug], axis=1)
        rhs = jnp.concatenate([qt * w_row.astype(BF16), p_t], axis=0)
        out = jnp.dot(lhs, rhs, preferred_element_type=F32)
        den = out[dv:dv + 1, :]
        hs_ref[hh * dv:(hh + 1) * dv, tok] = out[0:dv, :] * (1.0 / jnp.maximum(jnp.abs(den), jnp.exp2(-m_row)))

        d_state, g_max = state_delta(refs, c, hh, backward)
        m_new = jnp.maximum(b_end + m, g_max)
        caug_new = jnp.exp2(b_end + m - m_new) * caug + jnp.exp2(g_max - m_new) * d_state
        return caug_new, m_new

    def finish_chunk(hf_ref, hb_ref, ogt_ref, o_ref, c):
        tok = slice(c * L, (c + 1) * L)
        for hh in range(ML_HB):
            feat = slice(hh * dv, (hh + 1) * dv)
            hs = hf_ref[feat, tok] + hb_ref[feat, tok]
            scale = lax.rsqrt(jnp.mean(hs * hs, axis=0, keepdims=True) + NORM_EPS)
            head_g = jnp.concatenate([hg_ref[feat, :]] * (L // LANES), axis=1)
            y_t = hs * scale * head_g * jax.nn.sigmoid(ogt_ref[feat, tok].astype(F32))
            o_ref[tok, feat] = y_t.T.astype(o_ref.dtype)

    ctx_f = (kc_ref, qtc_ref, vtc_ref, grc_ref, hf_c)
    ctx_b = (kc_ref, qtc_ref, vtc_ref, grc_ref, hb_c)
    lat_f = (kl_ref, qtl_ref, vtl_ref, grl_ref, hf_l)
    lat_b = (kl_ref, qtl_ref, vtl_ref, grl_ref, hb_l)
    steps = [(ctx_f, s, ctx_b, ncc - 1 - s) for s in range(ncc)] \
        + [(lat_f, s, lat_b, ncl - 1 - s) for s in range(ncl)]

    def issue_scores(step):
        refs_f, c_f, refs_b, c_b = step
        for hh in range(ML_HB):
            scores_t(refs_f[0], refs_f[1], hh, c_f)
            scores_t(refs_b[0], refs_b[1], hh, c_b)
            state_delta(refs_f, c_f, hh, False)
            state_delta(refs_b, c_b, hh, True)

    zero = (jnp.zeros((dv + ML_AUG, dk), F32), jnp.zeros((1, 1), F32))
    states = [zero] * (2 * ML_HB)
    issue_scores(steps[0])
    for n, (refs_f, c_f, refs_b, c_b) in enumerate(steps):
        if n + 1 < len(steps):
            issue_scores(steps[n + 1])
        nxt = []
        for hh in range(ML_HB):
            nxt.append(chunk(refs_f, c_f, hh, False, states[2 * hh]))
            nxt.append(chunk(refs_b, c_b, hh, True, states[2 * hh + 1]))
        states = nxt
        if c_f >= c_b:
            ogt_ref, o_ref = (ogtc_ref, oc_ref) if refs_f is ctx_f else (ogtl_ref, ol_ref)
            for c in sorted({c_f, c_b}):
                finish_chunk(refs_f[4], refs_b[4], ogt_ref, o_ref, c)


def _mlstm(k_ctx, ft_ctx, g_ctx, k_lat, ft_lat, g_lat, head_g, nb, d):
    ctx_len = k_ctx.shape[0] // nb
    seq = k_lat.shape[0] // nb
    dv = ML_HB * (d // ML_HEADS)
    dk = dv // 2
    nh = ML_HEADS // ML_HB
    q_rows = ML_HEADS * (d // ML_HEADS // 2)
    v_blk = q_rows // dv
    og_blk = (q_rows + d) // dv
    L = math.gcd(math.gcd(ctx_len, seq), ML_BLOCK)
    assert L % LANES == 0
    grc = _gate_scan(g_ctx, nb, ctx_len, L)
    grl = _gate_scan(g_lat, nb, seq, L)
    hg = jnp.broadcast_to(head_g.reshape(d, 1), (d, LANES))

    def stream_specs(t):
        return [
            pl.BlockSpec((t, dk), lambda b, i: (b, i)),
            pl.BlockSpec((dk, t), lambda b, i: (i, b)),
            pl.BlockSpec((dv, t), lambda b, i: (v_blk + i, b)),
            pl.BlockSpec((dv, t), lambda b, i: (og_blk + i, b)),
            pl.BlockSpec((None, ML_HB, t // L, GATE_ROWS, L), lambda b, i: (b, i, 0, 0, 0)),
        ]

    t_all = ctx_len + seq
    gate_tiles = 2 * ML_HB * (t_all // L) * GATE_ROWS * L * 4
    vmem = 2 * t_all * (2 * dk * 2 + 3 * dv * 2) + 2 * t_all * dv * 4 + gate_tiles + 12 * 1024 * 1024
    return pl.pallas_call(
        _mlstm_kernel,
        grid=(nb, nh),
        in_specs=stream_specs(ctx_len) + stream_specs(seq) + [pl.BlockSpec((dv, LANES), lambda b, i: (i, 0))],
        out_specs=[pl.BlockSpec((ctx_len, dv), lambda b, i: (b, i)),
                   pl.BlockSpec((seq, dv), lambda b, i: (b, i))],
        out_shape=[jax.ShapeDtypeStruct((nb * ctx_len, d), BF16),
                   jax.ShapeDtypeStruct((nb * seq, d), BF16)],
        scratch_shapes=[pltpu.VMEM((dv, ctx_len), F32), pltpu.VMEM((dv, ctx_len), F32),
                        pltpu.VMEM((dv, seq), F32), pltpu.VMEM((dv, seq), F32)],
        compiler_params=_params(2, vmem),
        name="mlstm",
    )(k_ctx, ft_ctx, ft_ctx, ft_ctx, grc, k_lat, ft_lat, ft_lat, ft_lat, grl, hg)


def _attn_kernel(q_ref, kl_ref, vl_ref, kc_ref, vc_ref, lam_ref, gs_ref, o_ref, *, lam_init):
    dh = q_ref.shape[1] // 2
    seq = q_ref.shape[0]
    ctx_len = kc_ref.shape[0]
    lq = lam_ref[...]
    lam = jnp.exp(jnp.sum(lq[0:1] * lq[1:2], axis=-1, keepdims=True)) \
        - jnp.exp(jnp.sum(lq[2:3] * lq[3:4], axis=-1, keepdims=True)) + lam_init

    tq = _tile(seq, ATTN_TQ)
    key_sets = ((kl_ref, vl_ref), (kc_ref, vc_ref))

    def scores(t):
        rows = slice(t * tq, (t + 1) * tq)
        return [[_dot_nt(q_ref[rows, c * dh:(c + 1) * dh], k_ref[:, c * dh:(c + 1) * dh]) for c in range(2)]
                for k_ref, _ in key_sets]

    def lane_slabs(x):
        return [x[:, c:c + LANES] for c in range(0, x.shape[1], LANES)]

    def softmax_terms(s_sets):
        mx = functools.reduce(jnp.maximum, [slab for s in s_sets for slab in lane_slabs(s)])
        mx = jnp.max(mx, axis=-1, keepdims=True)
        p_sets = [jnp.exp2((s - mx).astype(BF16)) for s in s_sets]
        slabs = [slab for p in p_sets for slab in lane_slabs(p)]
        if len(slabs) % 2 == 0:
            slabs = [a + b for a, b in zip(slabs[0::2], slabs[1::2])]
        total = functools.reduce(jnp.add, [x.astype(F32) for x in slabs])
        return p_sets, 1.0 / jnp.sum(total, axis=-1, keepdims=True)

    n_tiles = seq // tq
    pending = [scores(t) for t in range(min(ATTN_AHEAD, n_tiles))]
    for t in range(n_tiles):
        if t + ATTN_AHEAD < n_tiles:
            pending.append(scores(t + ATTN_AHEAD))
        s = pending.pop(0)
        p0, r0 = softmax_terms([s[ks][0] for ks in range(2)])
        p1, r1 = softmax_terms([s[ks][1] for ks in range(2)])
        w0 = r0.astype(BF16)
        w1 = (lam * r1).astype(BF16)
        o = sum(jnp.dot(p0[ks] * w0 - p1[ks] * w1, v_ref[...], preferred_element_type=F32)
                for ks, (_, v_ref) in enumerate(key_sets))
        scale = lax.rsqrt(jnp.mean(o * o, axis=-1, keepdims=True) + NORM_EPS)
        o_ref[t * tq:(t + 1) * tq, :] = (o * scale * gs_ref[...] * (1.0 - lam_init)).astype(o_ref.dtype)


def _attn(p_lat, p_ctx, lam_qk, g_sub, nb, d, lam_init):
    seq = p_lat.shape[0] // nb
    ctx_len = p_ctx.shape[0] // nb
    h = DA_HEADS
    hd = d // h
    tq = _tile(seq, ATTN_TQ)
    vmem = 2 * (4 * seq + 2 * ctx_len) * hd * 2 + (ATTN_AHEAD + 3) * 2 * tq * (seq + ctx_len) * 4 \
        + 32 * tq * hd * 4 + 8 * 1024 * 1024
    return pl.pallas_call(
        functools.partial(_attn_kernel, lam_init=lam_init),
        grid=(nb, h),
        in_specs=[
            pl.BlockSpec((seq, hd), lambda b, i: (b, i)),
            pl.BlockSpec((seq, hd), lambda b, i: (b, h + i)),
            pl.BlockSpec((seq, hd), lambda b, i: (b, 2 * h + i)),
            pl.BlockSpec((ctx_len, hd), lambda b, i: (b, i)),
            pl.BlockSpec((ctx_len, hd), lambda b, i: (b, h + i)),
            pl.BlockSpec(lam_qk.shape, lambda b, i: (0, 0)),
            pl.BlockSpec((1, hd), lambda b, i: (0, 0)),
        ],
        out_specs=pl.BlockSpec((seq, hd), lambda b, i: (b, i)),
        out_shape=jax.ShapeDtypeStruct((nb * seq, d), BF16),
        compiler_params=_params(2, vmem),
        name="attn",
    )(p_lat, p_lat, p_lat, p_ctx, p_ctx, lam_qk, g_sub)


def _rope_tables(seq, dh):
    nf = dh // 4
    n = jnp.arange(seq)
    row = (n // GRID_W).astype(F32)
    col = (n % GRID_W).astype(F32)
    freq = ROPE_BASE ** (-jnp.arange(nf, dtype=F32) / nf)
    ar = row[:, None] * freq
    ac = col[:, None] * freq
    cos = jnp.concatenate([jnp.cos(ar), jnp.cos(ac), jnp.cos(ar), jnp.cos(ac)], axis=1)
    sin = jnp.concatenate([-jnp.sin(ar), -jnp.sin(ac), jnp.sin(ar), jnp.sin(ac)], axis=1)
    return cos, sin


def _rope_order(w, dh):
    nf = dh // 4
    g = w.reshape(w.shape[0], -1, 4, nf)
    return g[:, :, jnp.array([0, 2, 1, 3]), :].reshape(w.shape)


def kernel(x, c, ctx, c_ctx, ada_w, ada_b, norm_g, mlp_w1, mlp_w2, ml_w_in, ml_b_gates, ml_head_g,
           ml_w_out, da_w_in, da_lambda, da_sub_g, da_w_out):
    nb, seq, d = x.shape
    ctx_len = ctx.shape[1]
    depth = ada_w.shape[0]
    assert depth == 2 and nb < COND_ROWS and d // ML_HEADS == 2 * LANES
    assert seq % LANES == 0 and ctx_len % LANES == 0 and seq % GRID_W == 0

    xl = x.reshape(nb * seq, d)
    xc = ctx.reshape(nb * ctx_len, d)

    cond = jnp.zeros((COND_ROWS, d), F32).at[:nb].set(c).at[nb].set(c_ctx)
    mods = _ada(cond, ada_w, ada_b).reshape(depth, COND_ROWS, 1, N_MOD * d)

    tm = _tile(seq, 512)
    tm_mlp = _tile(seq, 1024)
    tmc = _tile(nb * ctx_len, 1024)

    def lat_row_of(tile_rows):
        return lambda i: i // (seq // tile_rows)

    lat_row = lat_row_of(tm)
    lat_row_mlp = lat_row_of(tm_mlp)

    def ctx_row(i):
        return nb

    g = norm_g.reshape(depth, 4, 1, d)

    ml_qk = ML_HEADS * (d // ML_HEADS // 2)
    n_main = 2 * ml_qk + 2 * d
    w_in = ml_w_in[0]
    q_scale = (d // ML_HEADS // 2) ** -0.5
    w_k = w_in[:, ml_qk:2 * ml_qk].astype(BF16)
    w_feat_t = jnp.concatenate([w_in[:, :ml_qk] * q_scale, w_in[:, 2 * ml_qk:n_main]], axis=1).T.astype(BF16)
    n_gate = 4 * ML_HEADS
    w_gate = jnp.pad(w_in[:, n_main:], ((0, 0), (0, LANES - n_gate))).astype(BF16)
    b_gate = jnp.pad(ml_b_gates[0], (0, LANES - n_gate)).reshape(1, LANES)
    w_out0 = ml_w_out[0].astype(BF16)
    w1_0 = mlp_w1[0].astype(BF16)
    w2_0 = mlp_w2[0].astype(BF16)

    k_lat, ft_lat, g_lat = _proj(xl, mods, 0, lat_row_mlp, g[0, 0], w_k, w_feat_t, tm=tm_mlp, tn=1024,
                                 gates=(w_gate, b_gate))
    k_ctx, ft_ctx, g_ctx = _proj(xc, mods, 0, ctx_row, g[0, 0], w_k, w_feat_t, tm=tmc, tn=1024,
                                 gates=(w_gate, b_gate))
    y_ctx, y_lat = _mlstm(k_ctx, ft_ctx, g_ctx, k_lat, ft_lat, g_lat, ml_head_g[0], nb, d)
    xl = _outproj(y_lat, w_out0, xl, mods, 0, lat_row, g[0, 1], tm=tm)
    xc = _outproj(y_ctx, w_out0, xc, mods, 0, ctx_row, g[0, 1], tm=tm)
    xl = _mlp(xl, mods, 0, lat_row_mlp, g[0, 2], g[0, 3], w1_0, w2_0, tm=tm_mlp, tf=1024)
    xc = _mlp(xc, mods, 0, ctx_row, g[0, 2], g[0, 3], w1_0, w2_0, tm=tmc, tf=512)

    dh = d // (2 * DA_HEADS)
    w_in = da_w_in[0]
    w_q = _rope_order(w_in[:, :d] * (dh ** -0.5 * LOG2E), dh)
    w_k = _rope_order(w_in[:, d:2 * d], dh)
    w_qkv = jnp.concatenate([w_q, w_k, w_in[:, 2 * d:]], axis=1).astype(BF16)
    w_kv = jnp.concatenate([w_k, w_in[:, 2 * d:]], axis=1).astype(BF16)
    w_out1 = da_w_out[0].astype(BF16)
    w1_1 = mlp_w1[1].astype(BF16)
    w2_1 = mlp_w2[1].astype(BF16)
    layer_idx = 1
    lam_init = 0.8 - 0.6 * math.exp(-0.3 * layer_idx)

    p_lat = _proj(xl, mods, 1, lat_row_mlp, g[1, 0], w_qkv, tm=tm_mlp, tn=2048,
                  rope=_rope_tables(seq, dh), rope_cols=2 * d)
    p_ctx = _proj(xc, mods, 1, ctx_row, g[1, 0], w_kv, tm=tmc, tn=2048)
    y_lat = _attn(p_lat, p_ctx, da_lambda[0], da_sub_g[0].reshape(1, 2 * dh), nb, d, lam_init)
    xl = _outproj(y_lat, w_out1, xl, mods, 1, lat_row, g[1, 1], tm=tm)
    xl = _mlp(xl, mods, 1, lat_row_mlp, g[1, 2], g[1, 3], w1_1, w2_1, tm=tm_mlp, tf=1024)
    return xl.reshape(nb, seq, d)
```
